```python
import math
import jax
import jax.numpy as jnp
from jax import lax
import numpy as np

D_MODEL = 4096
BATCH = 4
SEQ = 2048
DEPTH = 2
DEC_BATCH = 8
DEC_SEQ = 8
PAST_LEN = 16384
PAGE_SIZE = 128

HEAD_DIM = 128
NSA_WIDTH = D_MODEL // 2
NSA_HEADS = NSA_WIDTH // HEAD_DIM
NSA_KV_HEADS = NSA_HEADS // 4
NSA_GROUP = NSA_HEADS // NSA_KV_HEADS
CMP_BLOCK = 32
SEL_BLOCK = 64
SEL_RATIO = SEL_BLOCK // CMP_BLOCK
N_SELECT = 16
WINDOW = 512
NSA_QBLOCK = 32
FORCE_BONUS = 1.0e4
CONV_DIM = D_MODEL // 4
CONV_K = 3
MLSTM_WIDTH = D_MODEL // 4
MLSTM_HEADS = 4
MLSTM_DK = MLSTM_WIDTH // MLSTM_HEADS
MLSTM_DV = MLSTM_WIDTH // MLSTM_HEADS
MLSTM_CHUNK = 64
FORGET_BIAS = 3.0
MIX_WIDTH = NSA_WIDTH + CONV_DIM + MLSTM_WIDTH
KV_WIDTH = 2 * NSA_KV_HEADS * HEAD_DIM
ROPE_THETA = 10000.0
RMS_EPS = 1e-6
NEG_BIG = -1e30
TINY = 1e-30

IN_LAYOUT = (
    ('nsa_q', NSA_WIDTH), ('kv_cmp', KV_WIDTH), ('kv_sel', KV_WIDTH), ('kv_win', KV_WIDTH),
    ('nsa_gate', 3 * NSA_HEADS), ('nsa_z', NSA_WIDTH),
    ('conv_x', CONV_DIM), ('conv_b', CONV_DIM), ('conv_c', CONV_DIM), ('conv_z', CONV_DIM),
    ('m_q', MLSTM_WIDTH), ('m_k', MLSTM_WIDTH), ('m_v', MLSTM_WIDTH),
    ('m_i', MLSTM_HEADS), ('m_f', MLSTM_HEADS), ('m_o', MLSTM_WIDTH), ('m_z', MLSTM_WIDTH),
)
IN_WIDTH = sum(size for _, size in IN_LAYOUT)

kernel_name = 'nsa_conv_mlstm_hybrid_step'


def _col_offset(name):
    off = 0
    for n, s in IN_LAYOUT:
        if n == name:
            return off
        off += s
    raise KeyError(name)


def _split_proj(p):
    out = []
    off = 0
    for _, s in IN_LAYOUT:
        out.append(p[..., off:off + s])
        off += s
    return out


def _rmsnorm(x, w):
    xf = x.astype(jnp.float32)
    y = xf * lax.rsqrt(jnp.mean(xf * xf, axis=-1, keepdims=True) + RMS_EPS)
    return (y * w.astype(jnp.float32)).astype(x.dtype)


def _rope(x, pos):
    half = x.shape[-1] // 2
    inv = ROPE_THETA ** (-jnp.arange(half, dtype=jnp.float32) / half)
    ang = pos.astype(jnp.float32)[:, None] * inv[None, :]
    cos = jnp.cos(ang)[None, :, None, :]
    sin = jnp.sin(ang)[None, :, None, :]
    xf = x.astype(jnp.float32)
    x1, x2 = xf[..., :half], xf[..., half:]
    return jnp.concatenate([x1 * cos - x2 * sin, x2 * cos + x1 * sin], axis=-1).astype(x.dtype)


def _masked_softmax(s, mask):
    s = jnp.where(mask, s, NEG_BIG)
    s = s - jnp.max(s, axis=-1, keepdims=True)
    p = jnp.where(mask, jnp.exp(s), 0.0)
    return p / jnp.maximum(jnp.sum(p, axis=-1, keepdims=True), TINY)


def _nsa(q, kv_cmp, kv_sel, kv_win, gate_pre, past_cmp, past_sel, win_buf, pe, w_phi, past_len):
    B, T, H, Dh = q.shape
    dt = q.dtype
    L = past_cmp.shape[1] + T
    pos = past_len + jnp.arange(T, dtype=jnp.int32)
    scale = Dh ** -0.5
    q_c = q.reshape(B, T, NSA_KV_HEADS, NSA_GROUP, Dh)
    q_r = _rope(q, pos).reshape(B, T, NSA_KV_HEADS, NSA_GROUP, Dh)
    kv_sel_r = jnp.stack([_rope(kv_sel[:, :, 0], pos), kv_sel[:, :, 1]], axis=2)
    kv_win_r = jnp.stack([_rope(kv_win[:, :, 0], pos), kv_win[:, :, 1]], axis=2)
    g = jax.nn.sigmoid(gate_pre.astype(jnp.float32)).reshape(B, T, 3, NSA_KV_HEADS, NSA_GROUP, 1)

    cmp_all = jnp.concatenate([past_cmp, kv_cmp], axis=1)
    n_cmp = L // CMP_BLOCK
    blocks = cmp_all[:, :n_cmp * CMP_BLOCK].reshape(B, n_cmp, CMP_BLOCK, 2, NSA_KV_HEADS, Dh)
    pe_b = jnp.transpose(pe, (1, 0, 2))[:, :, None, :]
    summ = jnp.mean((blocks + pe_b).astype(jnp.float32), axis=2).astype(dt)
    summ = jnp.einsum('bnckd,cde->bncke', summ, w_phi)
    k_cmp, v_cmp = summ[:, :, 0], summ[:, :, 1]
    s_cmp = jnp.einsum('btkgd,bnkd->btkgn', q_c, k_cmp).astype(jnp.float32) * scale
    cmp_ok = (jnp.arange(n_cmp)[None, :] + 1) * CMP_BLOCK - 1 <= pos[:, None]
    p_cmp = _masked_softmax(s_cmp, cmp_ok[None, :, None, None, :])
    o_cmp = jnp.einsum('btkgn,bnkd->btkgd', p_cmp.astype(dt), v_cmp)

    n_sel = -(-L // SEL_BLOCK)
    imp = jnp.sum(p_cmp, axis=3)
    imp = jnp.pad(imp, ((0, 0), (0, 0), (0, 0), (0, n_sel * SEL_RATIO - n_cmp)))
    imp = imp.reshape(B, T, NSA_KV_HEADS, n_sel, SEL_RATIO).sum(-1)
    blk = jnp.arange(n_sel)[None, :]
    q_blk = (pos // SEL_BLOCK)[:, None]
    forced = (blk == 0) | (blk == q_blk) | (blk == q_blk - 1)
    future = blk * SEL_BLOCK > pos[:, None]
    score = jnp.where(forced[None, :, None, :], imp + FORCE_BONUS, imp)
    score = jnp.where(future[None, :, None, :], -jnp.inf, score)
    k_top = min(N_SELECT, n_sel)
    _, sel_idx = lax.top_k(score, k_top)
    sel_all = jnp.concatenate([past_sel, kv_sel_r], axis=1)
    sel_all = jnp.pad(sel_all, ((0, 0), (0, n_sel * SEL_BLOCK - L), (0, 0), (0, 0), (0, 0)))
    pool = sel_all.reshape(B, n_sel, SEL_BLOCK, 2, NSA_KV_HEADS, Dh)

    n_buf = win_buf.shape[1]
    win_all = jnp.concatenate([win_buf, kv_win_r], axis=1)
    ext = jnp.pad(win_all, ((0, 0), (WINDOW - n_buf, 0), (0, 0), (0, 0), (0, 0)))
    ext_idx = jnp.arange(WINDOW + T)
    ext_pos = past_len - WINDOW + ext_idx
    ext_ok = ext_idx >= WINDOW - n_buf

    QB = math.gcd(T, NSA_QBLOCK)
    nqb = T // QB
    bi = jnp.arange(B)[:, None, None, None]
    gi = jnp.arange(NSA_KV_HEADS)[None, None, :, None]

    def block(i):
        t0 = i * QB
        qb = lax.dynamic_slice_in_dim(q_r, t0, QB, axis=1)
        pb = lax.dynamic_slice_in_dim(pos, t0, QB, axis=0)
        ib = lax.dynamic_slice_in_dim(sel_idx, t0, QB, axis=1)
        gb = lax.dynamic_slice_in_dim(g, t0, QB, axis=1)
        kvs = pool[bi, ib, :, :, gi]
        ks = kvs[..., 0, :]
        vs = kvs[..., 1, :].reshape(B, QB, NSA_KV_HEADS, k_top * SEL_BLOCK, Dh)
        s = jnp.einsum('bqkgd,bqknsd->bqkgns', qb, ks).astype(jnp.float32) * scale
        s = s.reshape(B, QB, NSA_KV_HEADS, NSA_GROUP, k_top * SEL_BLOCK)
        kpos = ib[..., None] * SEL_BLOCK + jnp.arange(SEL_BLOCK)
        msk = (kpos <= pb[None, :, None, None, None]).reshape(B, QB, NSA_KV_HEADS, 1, k_top * SEL_BLOCK)
        p = _masked_softmax(s, msk)
        o_sel = jnp.einsum('bqkgm,bqkmd->bqkgd', p.astype(dt), vs)
        kw = lax.dynamic_slice_in_dim(ext, t0, WINDOW + QB, axis=1)
        wpos = lax.dynamic_slice_in_dim(ext_pos, t0, WINDOW + QB, axis=0)
        wok = lax.dynamic_slice_in_dim(ext_ok, t0, WINDOW + QB, axis=0)
        s = jnp.einsum('bqkgd,bmkd->bqkgm', qb, kw[:, :, 0]).astype(jnp.float32) * scale
        mw = wok[None, :] & (wpos[None, :] <= pb[:, None]) & (pb[:, None] - wpos[None, :] < WINDOW)
        p = _masked_softmax(s, mw[None, :, None, None, :])
        o_win = jnp.einsum('bqkgm,bmkd->bqkgd', p.astype(dt), kw[:, :, 1])
        return gb[:, :, 1] * o_sel + gb[:, :, 2] * o_win

    out = lax.map(block, jnp.arange(nqb))
    out = jnp.moveaxis(out, 0, 1).reshape(B, T, NSA_KV_HEADS, NSA_GROUP, Dh)
    o = (g[:, :, 0] * o_cmp + out).reshape(B, T, H * Dh).astype(dt)
    return o, kv_cmp, kv_sel_r, kv_win_r


def _short_conv(x_in, b_gate, c_gate, buf, w):
    u = c_gate * x_in
    ue = jnp.concatenate([buf, u], axis=1)
    T = u.shape[1]
    y = ue[:, 0:T] * w[0]
    for j in range(1, CONV_K):
        y = y + ue[:, j:j + T] * w[j]
    return b_gate * y, ue[:, ue.shape[1] - (CONV_K - 1):]


def _mlstm(q, k, v, i_pre, f_pre, C0, n0, m0):
    B, T, MH, DK = q.shape
    Lc = math.gcd(T, MLSTM_CHUNK)
    nc = T // Lc
    f32 = jnp.float32
    qf = q.astype(f32)
    kf = k.astype(f32) * (DK ** -0.5)
    vf = v.astype(f32)
    ig = i_pre.astype(f32)
    lf = jax.nn.log_sigmoid(f_pre.astype(f32))

    def chunks(a):
        return jnp.moveaxis(a.reshape((B, nc, Lc) + a.shape[2:]), 1, 0)

    causal = jnp.tril(jnp.ones((Lc, Lc), dtype=bool))

    def step(carry, inp):
        C, n, m = carry
        qc, kc, vc, ic, lfc = inp
        b = jnp.transpose(jnp.cumsum(lfc, axis=1), (0, 2, 1))
        ih = jnp.transpose(ic, (0, 2, 1))
        logD = jnp.where(causal, b[..., :, None] - b[..., None, :] + ih[..., None, :], -jnp.inf)
        m_t = jnp.maximum(b + m[..., None], jnp.max(logD, axis=-1))
        inter = jnp.exp(b + m[..., None] - m_t)
        S = jnp.einsum('bthd,bshd->bhts', qc, kc) * jnp.exp(logD - m_t[..., None])
        num = inter[..., None] * jnp.einsum('bthd,bhde->bhte', qc, C) + jnp.einsum('bhts,bshe->bhte', S, vc)
        den = inter * jnp.einsum('bthd,bhd->bht', qc, n) + jnp.sum(S, axis=-1)
        h = num / jnp.maximum(jnp.abs(den), jnp.exp(-m_t))[..., None]
        m_new = m_t[..., -1]
        wdec = jnp.exp(b[..., -1:] - b + ih - m_new[..., None])
        cdec = jnp.exp(b[..., -1] + m - m_new)
        C_new = cdec[..., None, None] * C + jnp.einsum('bhs,bshd,bshe->bhde', wdec, kc, vc)
        n_new = cdec[..., None] * n + jnp.einsum('bhs,bshd->bhd', wdec, kc)
        return (C_new, n_new, m_new), jnp.transpose(h, (0, 2, 1, 3))

    init = (C0.astype(f32), n0.astype(f32), m0.astype(f32))
    (C1, n1, m1), hs = lax.scan(step, init, (chunks(qf), chunks(kf), chunks(vf), chunks(ig), chunks(lf)))
    h = jnp.moveaxis(hs, 0, 1).reshape(B, T, MH, v.shape[-1])
    return h, C1, n1, m1


def _layer(l, x, c, past_len, past_cmp, past_sel, win_buf, conv_buf, C0, n0, m0,
           norm_w, w_ada, b_ada, w_in, b_in, cmp_pe, cmp_w, conv_w, mlstm_norm_w, w_out):
    B, T, _ = x.shape
    dt = x.dtype
    mod = jnp.einsum('bd,de->be', jax.nn.silu(c), w_ada[l]) + b_ada[l]
    shift, scale, gate = jnp.split(mod[:, None, :], 3, axis=-1)
    h = _rmsnorm(x, norm_w[l]) * (1 + scale) + shift
    proj = jnp.einsum('btd,de->bte', h, w_in[l]) + b_in[l]
    (nq, kvc, kvs, kvw, ng, nz, cx, cb, cc, cz, mq, mk, mv, mi, mf, mo, mz) = _split_proj(proj)
    kv_shape = (B, T, 2, NSA_KV_HEADS, HEAD_DIM)
    o_nsa, new_cmp, new_sel, new_win = _nsa(
        nq.reshape(B, T, NSA_HEADS, HEAD_DIM), kvc.reshape(kv_shape), kvs.reshape(kv_shape),
        kvw.reshape(kv_shape), ng, past_cmp, past_sel, win_buf, cmp_pe[l], cmp_w[l], past_len)
    o_nsa = o_nsa * jax.nn.silu(nz)
    o_conv, new_conv = _short_conv(cx, cb, cc, conv_buf, conv_w[l])
    o_conv = o_conv * jax.nn.silu(cz)
    mh = (B, T, MLSTM_HEADS, MLSTM_DK)
    h_m, C1, n1, m1 = _mlstm(mq.reshape(mh), mk.reshape(mh), mv.reshape(B, T, MLSTM_HEADS, MLSTM_DV),
                             mi, mf, C0, n0, m0)
    h_m = _rmsnorm(h_m.astype(dt), mlstm_norm_w[l].reshape(MLSTM_HEADS, MLSTM_DV)).reshape(B, T, MLSTM_WIDTH)
    o_ml = jax.nn.sigmoid(mo) * h_m * jax.nn.silu(mz)
    mixed = jnp.concatenate([o_nsa, o_conv, o_ml], axis=-1)
    y = x + gate * jnp.einsum('bte,ed->btd', mixed, w_out[l])
    return y, (new_cmp, new_sel, new_win, new_conv, C1.astype(dt), n1.astype(dt), m1.astype(dt))


def setup_inputs(seed: int = 0) -> dict:
    key = jax.random.key(seed)
    ks = jax.random.split(key, 24)
    nrm = jax.random.normal
    f32 = jnp.float32
    n_pages = PAST_LEN // PAGE_SIZE
    n_phys = (DEC_BATCH * n_pages * 5) // 4
    win_buf = min(WINDOW, PAST_LEN)
    kv_tail = (2, NSA_KV_HEADS, HEAD_DIM)
    f_off = _col_offset('m_f')
    b_in = 0.02 * nrm(ks[16], (DEPTH, IN_WIDTH), f32)
    b_in = b_in.at[:, f_off:f_off + MLSTM_HEADS].add(FORGET_BIAS)
    page_table = jax.random.permutation(ks[9], n_phys)[:DEC_BATCH * n_pages]
    page_table = page_table.reshape(DEC_BATCH, n_pages).astype(jnp.int32)
    return {
        'x_prompt': nrm(ks[0], (BATCH, SEQ, D_MODEL), f32),
        'x_sample': nrm(ks[1], (DEC_BATCH, DEC_SEQ, D_MODEL), f32),
        'cache_cmp_kv': nrm(ks[2], (DEPTH, n_phys, PAGE_SIZE) + kv_tail, f32),
        'cache_sel_kv': nrm(ks[3], (DEPTH, n_phys, PAGE_SIZE) + kv_tail, f32),
        'state_win_kv': nrm(ks[4], (DEPTH, DEC_BATCH, win_buf) + kv_tail, f32),
        'state_conv': nrm(ks[5], (DEPTH, DEC_BATCH, CONV_K - 1, CONV_DIM), f32),
        'state_mlstm_C': 0.1 * nrm(ks[6], (DEPTH, DEC_BATCH, MLSTM_HEADS, MLSTM_DK, MLSTM_DV), f32),
        'state_mlstm_n': 0.1 * nrm(ks[7], (DEPTH, DEC_BATCH, MLSTM_HEADS, MLSTM_DK), f32),
        'state_mlstm_m': nrm(ks[8], (DEPTH, DEC_BATCH, MLSTM_HEADS), f32),
        'page_table': page_table,
        'c_prompt': nrm(ks[10], (BATCH, D_MODEL), f32),
        'c_sample': nrm(ks[11], (DEC_BATCH, D_MODEL), f32),
        'norm_w': 1.0 + 0.05 * nrm(ks[12], (DEPTH, D_MODEL), f32),
        'w_ada': nrm(ks[13], (DEPTH, D_MODEL, 3 * D_MODEL), f32) * (0.5 * D_MODEL ** -0.5),
        'b_ada': 0.02 * nrm(ks[14], (DEPTH, 3 * D_MODEL), f32),
        'w_in': nrm(ks[15], (DEPTH, D_MODEL, IN_WIDTH), f32) * (D_MODEL ** -0.5),
        'b_in': b_in,
        'cmp_pe': 0.5 * nrm(ks[17], (DEPTH, 2, CMP_BLOCK, HEAD_DIM), f32),
        'cmp_w': nrm(ks[18], (DEPTH, 2, HEAD_DIM, HEAD_DIM), f32) * (HEAD_DIM ** -0.5),
        'conv_w': nrm(ks[19], (DEPTH, CONV_K, CONV_DIM), f32) * (CONV_K ** -0.5),
        'mlstm_norm_w': 1.0 + 0.05 * nrm(ks[20], (DEPTH, MLSTM_WIDTH), f32),
        'w_out': nrm(ks[21], (DEPTH, MIX_WIDTH, D_MODEL), f32) * (MIX_WIDTH ** -0.5),
        'final_norm_w': 1.0 + 0.05 * nrm(ks[22], (D_MODEL,), f32),
    }


def reference(x_prompt, x_sample, cache_cmp_kv, cache_sel_kv, state_win_kv, state_conv,
              state_mlstm_C, state_mlstm_n, state_mlstm_m, page_table, c_prompt, c_sample,
              norm_w, w_ada, b_ada, w_in, b_in, cmp_pe, cmp_w, conv_w, mlstm_norm_w, w_out, final_norm_w):
    dt = x_prompt.dtype
    B, T, _ = x_prompt.shape
    kv_empty = jnp.zeros((B, 0, 2, NSA_KV_HEADS, HEAD_DIM), dt)
    conv0 = jnp.zeros((B, CONV_K - 1, CONV_DIM), dt)
    C0 = jnp.zeros((B, MLSTM_HEADS, MLSTM_DK, MLSTM_DV), jnp.float32)
    n0 = jnp.zeros((B, MLSTM_HEADS, MLSTM_DK), jnp.float32)
    m0 = jnp.zeros((B, MLSTM_HEADS), jnp.float32)
    n_keep = min(WINDOW, T)

    xp = x_prompt
    ps = []
    for l in range(DEPTH):
        xp, st = _layer(l, xp, c_prompt, 0, kv_empty, kv_empty, kv_empty, conv0, C0, n0, m0,
                        norm_w, w_ada, b_ada, w_in, b_in, cmp_pe, cmp_w, conv_w, mlstm_norm_w, w_out)
        ps.append(st)

    DB = x_sample.shape[0]
    n_pages = page_table.shape[1]
    xs = x_sample
    ss = []
    for l in range(DEPTH):
        past_cmp = cache_cmp_kv[l][page_table].reshape(DB, n_pages * PAGE_SIZE, 2, NSA_KV_HEADS, HEAD_DIM)
        past_sel = cache_sel_kv[l][page_table].reshape(DB, n_pages * PAGE_SIZE, 2, NSA_KV_HEADS, HEAD_DIM)
        xs, st = _layer(l, xs, c_sample, PAST_LEN, past_cmp, past_sel, state_win_kv[l], state_conv[l],
                        state_mlstm_C[l], state_mlstm_n[l], state_mlstm_m[l],
                        norm_w, w_ada, b_ada, w_in, b_in, cmp_pe, cmp_w, conv_w, mlstm_norm_w, w_out)
        ss.append(st)

    y_prompt = _rmsnorm(xp, final_norm_w)
    y_sample = _rmsnorm(xs, final_norm_w)
    p_cmp = jnp.stack([s[0] for s in ps])
    p_sel = jnp.stack([s[1] for s in ps])
    p_win = jnp.stack([s[2][:, T - n_keep:] for s in ps])
    p_conv = jnp.stack([s[3] for s in ps])
    p_C = jnp.stack([s[4] for s in ps])
    p_n = jnp.stack([s[5] for s in ps])
    p_m = jnp.stack([s[6] for s in ps])
    s_cmp = jnp.stack([s[0] for s in ss])
    s_sel = jnp.stack([s[1] for s in ss])
    s_win = jnp.stack([s[2] for s in ss])
    s_conv = jnp.stack([s[3] for s in ss])
    s_C = jnp.stack([s[4] for s in ss])
    s_n = jnp.stack([s[5] for s in ss])
    s_m = jnp.stack([s[6] for s in ss])
    return (y_prompt, y_sample, p_cmp, p_sel, p_win, p_conv, p_C, p_n, p_m,
            s_cmp, s_sel, s_win, s_conv, s_C, s_n, s_m)
```

```python
import functools

import jax
import jax.numpy as jnp
from jax import lax
from jax.experimental import pallas as pl
from jax.experimental.pallas import tpu as pltpu

HEAD_DIM = 128
NSA_GROUP = 4
CMP_BLOCK = 32
SEL_BLOCK = 64
N_SELECT = 16
WINDOW = 512
FORCE_BONUS = 1.0e4
CONV_K = 3
MLSTM_HEADS = 4
ROPE_THETA = 10000.0
RMS_EPS = 1e-6
NEG_BIG = -1e30
TINY = 1e-30

LANES = 128
VMEM_LIMIT = 56 * 1024 * 1024

F32 = jnp.float32
BF16 = jnp.bfloat16

MAIN_NAMES = ('nsa_q', 'kv_cmp', 'kv_sel', 'kv_win', 'nsa_z', 'conv_x', 'conv_b', 'conv_c', 'conv_z',
              'm_q', 'm_k', 'm_v', 'm_o', 'm_z')
SMALL_NAMES = ('nsa_gate', 'm_i', 'm_f')


def _in_layout(d_model):
    nsa_w, kv_w, conv_w, ml_w = d_model // 2, d_model // 4, d_model // 4, d_model // 4
    nsa_heads = nsa_w // HEAD_DIM
    return (('nsa_q', nsa_w), ('kv_cmp', kv_w), ('kv_sel', kv_w), ('kv_win', kv_w),
            ('nsa_gate', 3 * nsa_heads), ('nsa_z', nsa_w),
            ('conv_x', conv_w), ('conv_b', conv_w), ('conv_c', conv_w), ('conv_z', conv_w),
            ('m_q', ml_w), ('m_k', ml_w), ('m_v', ml_w),
            ('m_i', MLSTM_HEADS), ('m_f', MLSTM_HEADS), ('m_o', ml_w), ('m_z', ml_w))


def _offsets(names, sizes):
    off, out = 0, {}
    for n in names:
        out[n] = off
        off += sizes[n]
    return out, off


def _cparams(sem):
    return pltpu.CompilerParams(dimension_semantics=sem, vmem_limit_bytes=VMEM_LIMIT)


def _tile(dim, pref):
    t = min(dim, pref)
    assert dim % t == 0, (dim, pref)
    return t


def _nt(a, b):
    return lax.dot_general(a, b, (((1,), (1,)), ((), ())), preferred_element_type=F32)


def _silu(x):
    return x * jax.nn.sigmoid(x)


def _lane_pick(x, col):
    lane = lax.broadcasted_iota(jnp.int32, x.shape, 1)
    return jnp.sum(jnp.where(lane == col, x, 0.0), axis=1, keepdims=True)


def _ada_kernel(c_ref, w_ref, b_ref, o_ref):
    a = _silu(c_ref[...]).astype(BF16)
    o_ref[0] = jnp.dot(a, w_ref[0].astype(BF16), preferred_element_type=F32) + b_ref[0]


def _ada(c_all, w_ada, b_ada):
    rows, d = c_all.shape
    depth, _, n = w_ada.shape
    tn = _tile(n, 512)
    return pl.pallas_call(
        _ada_kernel,
        grid=(depth, n // tn),
        in_specs=[pl.BlockSpec((rows, d), lambda l, j: (0, 0)),
                  pl.BlockSpec((1, d, tn), lambda l, j: (l, 0, j)),
                  pl.BlockSpec((1, 1, tn), lambda l, j: (l, 0, j))],
        out_specs=pl.BlockSpec((1, rows, tn), lambda l, j: (l, 0, j)),
        out_shape=jax.ShapeDtypeStruct((depth, rows, n), F32),
        compiler_params=_cparams(("arbitrary", "arbitrary")),
        name="ada",
    )(c_all, w_ada, b_ada.reshape(depth, 1, n))


def _inproj_kernel(x_ref, sc_ref, sh_ref, nw_ref, wm_ref, bm_ref, ws_ref, bs_ref, o_ref, os_ref, h_scr):
    @pl.when(pl.program_id(1) == 0)
    def _():
        x = x_ref[...]
        y = x * lax.rsqrt(jnp.mean(x * x, axis=-1, keepdims=True) + RMS_EPS) * nw_ref[...]
        h = (y * (1.0 + sc_ref[0]) + sh_ref[0]).astype(BF16)
        h_scr[...] = h
        os_ref[...] = jnp.dot(h, ws_ref[...], preferred_element_type=F32) + bs_ref[...]

    o_ref[...] = jnp.dot(h_scr[...], wm_ref[...], preferred_element_type=F32) + bm_ref[...]


def _inproj(x2, scale3, shift3, nw, w_main, b_main, w_small, b_small, tm, rows_per_mod):
    m, d = x2.shape
    nm = w_main.shape[1]
    tn = _tile(nm, 1024)
    r = scale3.shape[1]
    mod_spec = pl.BlockSpec((1, r, d), lambda i, j: (i // rows_per_mod, 0, 0))
    return pl.pallas_call(
        _inproj_kernel,
        grid=(m // tm, nm // tn),
        in_specs=[pl.BlockSpec((tm, d), lambda i, j: (i, 0)), mod_spec, mod_spec,
                  pl.BlockSpec((1, d), lambda i, j: (0, 0)),
                  pl.BlockSpec((d, tn), lambda i, j: (0, j)),
                  pl.BlockSpec((1, tn), lambda i, j: (0, j)),
                  pl.BlockSpec((d, LANES), lambda i, j: (0, 0)),
                  pl.BlockSpec((1, LANES), lambda i, j: (0, 0))],
        out_specs=[pl.BlockSpec((tm, tn), lambda i, j: (i, j)),
                   pl.BlockSpec((tm, LANES), lambda i, j: (i, 0))],
        out_shape=[jax.ShapeDtypeStruct((m, nm), F32), jax.ShapeDtypeStruct((m, LANES), F32)],
        scratch_shapes=[pltpu.VMEM((tm, d), BF16)],
        compiler_params=_cparams(("arbitrary", "arbitrary")),
        name="inproj",
    )(x2, scale3, shift3, nw, w_main, b_main, w_small, b_small)


def _rope_kernel(q_ref, ks_ref, kw_ref, cos_ref, sin_ref, qr_ref, ksr_ref, kwr_ref, *, n_q, n_kv):
    cos, sin = cos_ref[...], sin_ref[...]

    def rot(x):
        return x * cos + pltpu.roll(x, HEAD_DIM // 2, axis=1) * sin

    for h in range(n_q):
        sl = slice(h * HEAD_DIM, (h + 1) * HEAD_DIM)
        qr_ref[:, sl] = rot(q_ref[:, sl]).astype(qr_ref.dtype)
    for src, dst in ((ks_ref, ksr_ref), (kw_ref, kwr_ref)):
        for h in range(n_kv):
            sl = slice(h * HEAD_DIM, (h + 1) * HEAD_DIM)
            dst[:, sl] = rot(src[:, sl])
        dst[:, n_kv * HEAD_DIM:] = src[:, n_kv * HEAD_DIM:]


def _rope(proj, cos, sin, mo, nsa_w, kv_w):
    m = proj.shape[0]
    tr = _tile(m, 256)
    n_q, n_kv = nsa_w // HEAD_DIM, kv_w // (2 * HEAD_DIM)
    return pl.pallas_call(
        functools.partial(_rope_kernel, n_q=n_q, n_kv=n_kv),
        grid=(m // tr,),
        in_specs=[pl.BlockSpec((tr, nsa_w), lambda i: (i, mo['nsa_q'] // nsa_w)),
                  pl.BlockSpec((tr, kv_w), lambda i: (i, mo['kv_sel'] // kv_w)),
                  pl.BlockSpec((tr, kv_w), lambda i: (i, mo['kv_win'] // kv_w)),
                  pl.BlockSpec((tr, HEAD_DIM), lambda i: (i, 0)),
                  pl.BlockSpec((tr, HEAD_DIM), lambda i: (i, 0))],
        out_specs=[pl.BlockSpec((tr, nsa_w), lambda i: (i, 0)),
                   pl.BlockSpec((tr, kv_w), lambda i: (i, 0)),
                   pl.BlockSpec((tr, kv_w), lambda i: (i, 0))],
        out_shape=[jax.ShapeDtypeStruct((m, nsa_w), BF16),
                   jax.ShapeDtypeStruct((m, kv_w), F32),
                   jax.ShapeDtypeStruct((m, kv_w), F32)],
        compiler_params=_cparams(("arbitrary",)),
        name="rope",
    )(proj, proj, proj, cos, sin)


def _rope_tables(pos):
    half = HEAD_DIM // 2
    inv = ROPE_THETA ** (-jnp.arange(half, dtype=F32) / half)
    ang = pos.astype(F32)[:, None] * inv[None, :]
    cos, sin = jnp.cos(ang), jnp.sin(ang)
    return jnp.concatenate([cos, cos], axis=1), jnp.concatenate([-sin, sin], axis=1)


def _flash_step(carry, s, msk, v_bf):
    m, l, acc = carry
    s = jnp.where(msk, s, NEG_BIG)
    m_new = jnp.maximum(m, jnp.max(s, axis=-1, keepdims=True))
    p = jnp.where(msk, jnp.exp(s - m_new), 0.0)
    alpha = jnp.exp(m - m_new)
    l = alpha * l + jnp.sum(p, axis=-1, keepdims=True)
    pv = jnp.dot(p.reshape(-1, p.shape[-1]).astype(BF16), v_bf, preferred_element_type=F32)
    acc = alpha * acc + pv.reshape(acc.shape)
    return m_new, l, acc


def _flash_init(shape_rows):
    return (jnp.full(shape_rows + (1,), NEG_BIG, F32), jnp.zeros(shape_rows + (1,), F32),
            jnp.zeros(shape_rows + (HEAD_DIM,), F32))


def _flash_out(carry):
    _, l, acc = carry
    return acc / jnp.maximum(l, TINY)


def _topk_rank(score, blk, n_blocks):
    rank = jnp.zeros(score.shape, F32)
    for i in range(n_blocks):
        si = score[:, i:i + 1]
        ge = jnp.where(si >= score, 1.0, 0.0)
        gt = jnp.where(si > score, 1.0, 0.0)
        rank = rank + jnp.where(blk > i, ge, gt)
    return rank


def _nsa_prompt_kernel(qc_ref, qr_ref, kc_ref, vc_ref, ks_ref, vs_ref, kw_ref, vw_ref, nz_ref, sm_ref,
                       pe_ref, wphi_ref, o_ref, kcs, vcs, *, seq, tq, tk, n_heads):
    kvh, qi = pl.program_id(1), pl.program_id(2)
    n_sel = seq // SEL_BLOCK
    n_cmp = 2 * n_sel
    grp = NSA_GROUP
    scale = HEAD_DIM ** -0.5

    @pl.when(qi == 0)
    def _():
        for c, (src, dst) in enumerate(((kc_ref, kcs), (vc_ref, vcs))):
            x = src[0].reshape(n_sel, 2 * CMP_BLOCK, HEAD_DIM)
            pe = pe_ref[c][None]
            ev = jnp.mean(x[:, :CMP_BLOCK, :] + pe, axis=1)
            od = jnp.mean(x[:, CMP_BLOCK:, :] + pe, axis=1)
            summ = jnp.concatenate([ev, od], axis=0).astype(BF16)
            dst[...] = jnp.dot(summ, wphi_ref[c].astype(BF16), preferred_element_type=F32)

    t0 = qi * tq
    pos = t0 + lax.broadcasted_iota(jnp.int32, (tq, 1), 0)

    lane_c = lax.broadcasted_iota(jnp.int32, (1, n_cmp), 1)
    cblk = jnp.where(lane_c < n_sel, 2 * lane_c, 2 * (lane_c - n_sel) + 1)
    cmp_ok = (cblk + 1) * CMP_BLOCK - 1 <= pos
    kcb, vcb = kcs[...].astype(BF16), vcs[...].astype(BF16)
    qc = qc_ref[0]
    imp = jnp.zeros((tq, n_cmp), F32)
    o_cmp = []
    for g in range(grp):
        s = _nt(qc[:, g * HEAD_DIM:(g + 1) * HEAD_DIM].astype(BF16), kcb) * scale
        s = jnp.where(cmp_ok, s, NEG_BIG)
        s = s - jnp.max(s, axis=-1, keepdims=True)
        p = jnp.where(cmp_ok, jnp.exp(s), 0.0)
        p = p / jnp.maximum(jnp.sum(p, axis=-1, keepdims=True), TINY)
        imp = imp + p
        o_cmp.append(jnp.dot(p.astype(BF16), vcb, preferred_element_type=F32))

    imp_sel = imp[:, :n_sel] + imp[:, n_sel:]
    blk = lax.broadcasted_iota(jnp.int32, (1, n_sel), 1)
    q_blk = pos // SEL_BLOCK
    forced = (blk == 0) | (blk == q_blk) | (blk == q_blk - 1)
    score = jnp.where(forced, imp_sel + FORCE_BONUS, imp_sel)
    score = jnp.where(blk * SEL_BLOCK > pos, -jnp.inf, score)
    rank = _topk_rank(score, blk, n_sel)
    selb = jnp.where(rank < N_SELECT, 1.0, 0.0).astype(BF16)

    qr = qr_ref[0]
    q4 = jnp.concatenate([qr[:, g * HEAD_DIM:(g + 1) * HEAD_DIM] for g in range(grp)], axis=0)

    def sel_body(c, carry):
        k0 = pl.multiple_of(c * tk, tk)
        kch = ks_ref[0, pl.ds(k0, tk), :].astype(BF16)
        vch = vs_ref[0, pl.ds(k0, tk), :].astype(BF16)
        kpos = k0 + lax.broadcasted_iota(jnp.int32, (1, tk), 1)
        row = lax.broadcasted_iota(jnp.int32, (n_sel, 1), 0)
        expand = jnp.where(kpos // SEL_BLOCK == row, 1.0, 0.0).astype(BF16)
        member = jnp.dot(selb, expand, preferred_element_type=F32)
        msk = jnp.where(kpos <= pos, member, 0.0) > 0.5
        s = (_nt(q4, kch) * scale).reshape(grp, tq, tk)
        return _flash_step(carry, s, msk[None], vch)

    n_chunks = (t0 + tq + tk - 1) // tk
    o_sel = _flash_out(lax.fori_loop(0, n_chunks, sel_body, _flash_init((grp, tq))))

    def win_body(c, carry):
        k0 = pl.multiple_of(c * tq, tq)
        kch = kw_ref[0, pl.ds(k0, tq), :].astype(BF16)
        vch = vw_ref[0, pl.ds(k0, tq), :].astype(BF16)
        kpos = k0 + lax.broadcasted_iota(jnp.int32, (1, tq), 1)
        msk = jnp.where(kpos <= pos, pos - kpos, WINDOW) < WINDOW
        s = (_nt(q4, kch) * scale).reshape(grp, tq, tq)
        return _flash_step(carry, s, msk[None], vch)

    c_lo = jnp.maximum(qi - (WINDOW + tq - 1) // tq, 0)
    o_win = _flash_out(lax.fori_loop(c_lo, qi + 1, win_body, _flash_init((grp, tq))))

    gates = jax.nn.sigmoid(sm_ref[0])
    nz = nz_ref[0]
    for g in range(grp):
        head = kvh * grp + g
        g_cmp = _lane_pick(gates, head)
        g_sel = _lane_pick(gates, n_heads + head)
        g_win = _lane_pick(gates, 2 * n_heads + head)
        o = g_cmp * o_cmp[g] + (g_sel * o_sel[g] + g_win * o_win[g])
        sl = slice(g * HEAD_DIM, (g + 1) * HEAD_DIM)
        o_ref[0, :, sl] = (o * _silu(nz[:, sl])).astype(o_ref.dtype)


def _nsa_prompt(proj3, small3, qr3, ksr3, kwr3, pe, wphi, mo, nsa_w):
    b, seq, _ = proj3.shape
    n_heads = nsa_w // HEAD_DIM
    kvh = n_heads // NSA_GROUP
    gw = NSA_GROUP * HEAD_DIM
    tq = _tile(seq, 256)
    tk = _tile(seq, 512)
    assert seq % (2 * CMP_BLOCK) == 0 and tk % SEL_BLOCK == 0
    hb = lambda name: mo[name] // HEAD_DIM

    def head_spec(off_blocks):
        return pl.BlockSpec((1, seq, HEAD_DIM), lambda bi, k, qi: (bi, 0, off_blocks + k))

    return pl.pallas_call(
        functools.partial(_nsa_prompt_kernel, seq=seq, tq=tq, tk=tk, n_heads=n_heads),
        grid=(b, kvh, seq // tq),
        in_specs=[pl.BlockSpec((1, tq, gw), lambda bi, k, qi: (bi, qi, mo['nsa_q'] // gw + k)),
                  pl.BlockSpec((1, tq, gw), lambda bi, k, qi: (bi, qi, k)),
                  head_spec(hb('kv_cmp')), head_spec(hb('kv_cmp') + kvh),
                  head_spec(0), head_spec(kvh), head_spec(0), head_spec(kvh),
                  pl.BlockSpec((1, tq, gw), lambda bi, k, qi: (bi, qi, mo['nsa_z'] // gw + k)),
                  pl.BlockSpec((1, tq, LANES), lambda bi, k, qi: (bi, qi, 0)),
                  pl.BlockSpec((2, CMP_BLOCK, HEAD_DIM), lambda bi, k, qi: (0, 0, 0)),
                  pl.BlockSpec((2, HEAD_DIM, HEAD_DIM), lambda bi, k, qi: (0, 0, 0))],
        out_specs=pl.BlockSpec((1, tq, gw), lambda bi, k, qi: (bi, qi, k)),
        out_shape=jax.ShapeDtypeStruct((b, seq, nsa_w), BF16),
        scratch_shapes=[pltpu.VMEM((seq // CMP_BLOCK, HEAD_DIM), F32),
                        pltpu.VMEM((seq // CMP_BLOCK, HEAD_DIM), F32)],
        compiler_params=_cparams(("arbitrary", "arbitrary", "arbitrary")),
        name="nsa_prompt",
    )(proj3, qr3, proj3, proj3, ksr3, ksr3, kwr3, kwr3, proj3, small3, pe, wphi)


def _summ_kernel(pt_ref, *refs, n_pg):
    del pt_ref
    x_refs, pe_ref, ev_ref, od_ref = refs[:n_pg], refs[n_pg], refs[n_pg + 1], refs[n_pg + 2]
    pe = pe_ref[...][None]
    evs, ods = [], []
    for r in x_refs:
        page = r[0, 0]
        x = page.reshape(page.shape[0] // (2 * CMP_BLOCK), 2 * CMP_BLOCK, page.shape[1])
        evs.append(jnp.mean(x[:, :CMP_BLOCK, :] + pe, axis=1))
        ods.append(jnp.mean(x[:, CMP_BLOCK:, :] + pe, axis=1))
    ev_ref[0] = jnp.concatenate(evs, axis=0)
    od_ref[0] = jnp.concatenate(ods, axis=0)


def _summaries(cache4, layer, pt_flat, pe_full, n_batch, n_pages):
    _, _, page, width = cache4.shape
    per_page = page // (2 * CMP_BLOCK)
    n_pg = max(1, 8 // per_page)
    assert n_pages % n_pg == 0 and page % (2 * CMP_BLOCK) == 0
    rows = n_pg * per_page

    def page_spec(k):
        return pl.BlockSpec((1, 1, page, width),
                            lambda bi, j, pt: (layer, pt[bi * n_pages + j * n_pg + k], 0, 0))

    out_spec = pl.BlockSpec((1, rows, width), lambda bi, j, pt: (bi, j, 0))
    out_sds = jax.ShapeDtypeStruct((n_batch, n_pages * per_page, width), F32)
    return pl.pallas_call(
        functools.partial(_summ_kernel, n_pg=n_pg),
        grid_spec=pltpu.PrefetchScalarGridSpec(
            num_scalar_prefetch=1,
            grid=(n_batch, n_pages // n_pg),
            in_specs=[page_spec(k) for k in range(n_pg)]
            + [pl.BlockSpec((CMP_BLOCK, width), lambda bi, j, pt: (0, 0))],
            out_specs=[out_spec, out_spec]),
        out_shape=[out_sds, out_sds],
        compiler_params=_cparams(("arbitrary", "arbitrary")),
        name="cmp_summaries",
    )(pt_flat, *([cache4] * n_pg), pe_full)


def _cmpsel_kernel(se_ref, so_ref, q_ref, wphi_ref, ocmp_ref, idx_ref, *, past, t_new, kvh_n, n_selp):
    grp = NSA_GROUP
    scale = HEAD_DIM ** -0.5
    n_half = se_ref.shape[1]
    n_sel = n_half + 1
    tpos = lax.broadcasted_iota(jnp.int32, (t_new, 1), 0) + past
    pos = jnp.concatenate([tpos] * grp, axis=0)
    lane = lax.broadcasted_iota(jnp.int32, (1, n_half), 1)
    ok_e = (2 * lane + 1) * CMP_BLOCK - 1 <= pos
    ok_o = (2 * lane + 2) * CMP_BLOCK - 1 <= pos
    w_k, w_v = wphi_ref[0].astype(BF16), wphi_ref[1].astype(BF16)
    q = q_ref[0]
    scores = []
    for k in range(kvh_n):
        ksl = slice(k * HEAD_DIM, (k + 1) * HEAD_DIM)
        vsl = slice((kvh_n + k) * HEAD_DIM, (kvh_n + k + 1) * HEAD_DIM)
        proj = lambda ref, sl, w: jnp.dot(ref[0][:, sl].astype(BF16), w, preferred_element_type=F32).astype(BF16)
        k_e, k_o = proj(se_ref, ksl, w_k), proj(so_ref, ksl, w_k)
        v_e, v_o = proj(se_ref, vsl, w_v), proj(so_ref, vsl, w_v)
        qk = jnp.concatenate(
            [q[:, (k * grp + g) * HEAD_DIM:(k * grp + g + 1) * HEAD_DIM] for g in range(grp)], axis=0).astype(BF16)
        s_e = jnp.where(ok_e, _nt(qk, k_e) * scale, NEG_BIG)
        s_o = jnp.where(ok_o, _nt(qk, k_o) * scale, NEG_BIG)
        m = jnp.maximum(jnp.max(s_e, axis=-1, keepdims=True), jnp.max(s_o, axis=-1, keepdims=True))
        p_e = jnp.where(ok_e, jnp.exp(s_e - m), 0.0)
        p_o = jnp.where(ok_o, jnp.exp(s_o - m), 0.0)
        den = jnp.maximum(jnp.sum(p_e, axis=-1, keepdims=True) + jnp.sum(p_o, axis=-1, keepdims=True), TINY)
        p_e, p_o = p_e / den, p_o / den
        o = (jnp.dot(p_e.astype(BF16), v_e, preferred_element_type=F32)
             + jnp.dot(p_o.astype(BF16), v_o, preferred_element_type=F32))
        imp = jnp.zeros((t_new, n_half), F32)
        for g in range(grp):
            hsl = slice((k * grp + g) * HEAD_DIM, (k * grp + g + 1) * HEAD_DIM)
            ocmp_ref[0, :, hsl] = o[g * t_new:(g + 1) * t_new]
            imp = imp + (p_e[g * t_new:(g + 1) * t_new] + p_o[g * t_new:(g + 1) * t_new])
        tail_lane = lax.broadcasted_iota(jnp.int32, (t_new, n_selp - n_half), 1)
        tail = jnp.where(tail_lane == 0, 0.0, -jnp.inf)
        scores.append(jnp.concatenate([imp, tail], axis=1))
    score = jnp.concatenate(scores, axis=0)
    pos_r = jnp.concatenate([tpos] * kvh_n, axis=0)
    blk = lax.broadcasted_iota(jnp.int32, (1, n_selp), 1)
    q_blk = pos_r // SEL_BLOCK
    forced = (blk == 0) | (blk == q_blk) | (blk == q_blk - 1)
    score = jnp.where(forced, score + FORCE_BONUS, score)
    score = jnp.where((blk * SEL_BLOCK > pos_r) | (blk >= n_sel), -jnp.inf, score)
    rank = _topk_rank(score, blk, n_sel)
    blk_f = blk.astype(F32)
    out_lane = lax.broadcasted_iota(jnp.int32, (kvh_n * t_new, LANES), 1)
    out = jnp.zeros((kvh_n * t_new, LANES), F32)
    for r in range(min(N_SELECT, n_sel)):
        pick = jnp.sum(jnp.where(rank == r, blk_f, 0.0), axis=1, keepdims=True)
        out = jnp.where(out_lane == r, pick, out)
    idx_ref[0] = out.astype(jnp.int32)


def _cmpsel(summ_e, summ_o, proj3, wphi, mo, nsa_w, past):
    b, n_half, width = summ_e.shape
    t_new = proj3.shape[1]
    kvh_n = nsa_w // HEAD_DIM // NSA_GROUP
    n_selp = -(-(n_half + 1) // LANES) * LANES
    return pl.pallas_call(
        functools.partial(_cmpsel_kernel, past=past, t_new=t_new, kvh_n=kvh_n, n_selp=n_selp),
        grid=(b,),
        in_specs=[pl.BlockSpec((1, n_half, width), lambda bi: (bi, 0, 0)),
                  pl.BlockSpec((1, n_half, width), lambda bi: (bi, 0, 0)),
                  pl.BlockSpec((1, t_new, nsa_w), lambda bi: (bi, 0, mo['nsa_q'] // nsa_w)),
                  pl.BlockSpec((2, HEAD_DIM, HEAD_DIM), lambda bi: (0, 0, 0))],
        out_specs=[pl.BlockSpec((1, t_new, nsa_w), lambda bi: (bi, 0, 0)),
                   pl.BlockSpec((1, kvh_n * t_new, LANES), lambda bi: (bi, 0, 0))],
        out_shape=[jax.ShapeDtypeStruct((b, t_new, nsa_w), F32),
                   jax.ShapeDtypeStruct((b, kvh_n * t_new, LANES), jnp.int32)],
        compiler_params=_cparams(("arbitrary",)),
        name="cmp_select",
    )(summ_e, summ_o, proj3, wphi)


def _two_part_attention(parts):
    m = None
    for s, msk, _ in parts:
        mx = jnp.max(jnp.where(msk, s, NEG_BIG), axis=-1, keepdims=True)
        m = mx if m is None else jnp.maximum(m, mx)
    den, acc = 0.0, 0.0
    for s, msk, v in parts:
        p = jnp.where(msk, jnp.exp(jnp.where(msk, s, NEG_BIG) - m), 0.0)
        den = den + jnp.sum(p, axis=-1, keepdims=True)
        acc = acc + jnp.dot(p.astype(BF16), v, preferred_element_type=F32)
    return acc / jnp.maximum(den, TINY)


def _selwin_kernel(idx_ref, pt_ref, cache_ref, qr_ref, ksn_ref, vsn_ref, kwb_ref, vwb_ref, kwn_ref, vwn_ref,
                   sm_ref, ocmp_ref, nz_ref, o_ref, kbuf, vbuf, sem,
                   *, layer, past, t_new, kvh_n, n_pages, page, n_heads):
    bi, kvh = pl.program_id(0), pl.program_id(1)
    grp = NSA_GROUP
    scale = HEAD_DIM ** -0.5
    n_top = kbuf.shape[0] // t_new
    n_past_blk = past // SEL_BLOCK
    per_page = page // SEL_BLOCK
    base = (bi * kvh_n + kvh) * t_new * n_top
    k_col = pl.multiple_of(kvh * HEAD_DIM, HEAD_DIM)
    v_col = pl.multiple_of((kvh_n + kvh) * HEAD_DIM, HEAD_DIM)

    def copies(i, row0):
        return (pltpu.make_async_copy(cache_ref.at[layer, pl.ds(row0, SEL_BLOCK), pl.ds(k_col, HEAD_DIM)],
                                      kbuf.at[i], sem.at[0]),
                pltpu.make_async_copy(cache_ref.at[layer, pl.ds(row0, SEL_BLOCK), pl.ds(v_col, HEAD_DIM)],
                                      vbuf.at[i], sem.at[1]))

    def issue(i, carry):
        idc = jnp.minimum(idx_ref[base + i], n_past_blk - 1)
        phys = pt_ref[bi * n_pages + idc // per_page]
        row0 = pl.multiple_of(phys * page + (idc % per_page) * SEL_BLOCK, SEL_BLOCK)
        for cp in copies(i, row0):
            cp.start()
        return carry

    def wait(i, carry):
        for cp in copies(i, 0):
            cp.wait()
        return carry

    lax.fori_loop(0, t_new * n_top, issue, 0)
    lax.fori_loop(0, t_new * n_top, wait, 0)

    qr = qr_ref[0]
    q4 = jnp.concatenate([qr[:, g * HEAD_DIM:(g + 1) * HEAD_DIM] for g in range(grp)], axis=0)
    rows = grp * t_new
    row_t = jnp.concatenate([lax.broadcasted_iota(jnp.int32, (t_new, 1), 0)] * grp, axis=0)
    pos = row_t + past

    def padded(ref):
        x = ref[0]
        return jnp.concatenate([x, jnp.zeros((SEL_BLOCK - t_new, HEAD_DIM), x.dtype)], axis=0).astype(BF16)

    new_j = lax.broadcasted_iota(jnp.int32, (1, SEL_BLOCK), 1)
    new_pos = past + new_j
    new_real = new_j < t_new

    ksn, vsn = padded(ksn_ref), padded(vsn_ref)
    s_new = _nt(q4, ksn) * scale
    lane = lax.broadcasted_iota(jnp.int32, (1, n_top * SEL_BLOCK), 1)
    o_sel = jnp.zeros((rows, HEAD_DIM), F32)
    for t in range(t_new):
        idx_vec = jnp.zeros((1, n_top * SEL_BLOCK), jnp.int32)
        has_new = jnp.int32(0)
        for n in range(n_top):
            idx = idx_ref[base + t * n_top + n]
            idx_vec = jnp.where(lane // SEL_BLOCK == n, idx, idx_vec)
            has_new = has_new | (idx == n_past_blk).astype(jnp.int32)
        kpos = idx_vec * SEL_BLOCK + lane % SEL_BLOCK
        msk = jnp.where(idx_vec < n_past_blk, kpos, past + t + 1) <= past + t
        k_t = kbuf[t * n_top:(t + 1) * n_top].reshape(n_top * SEL_BLOCK, HEAD_DIM).astype(BF16)
        v_t = vbuf[t * n_top:(t + 1) * n_top].reshape(n_top * SEL_BLOCK, HEAD_DIM).astype(BF16)
        lim_new = jnp.where(has_new > 0, past + t, -1)
        msk_new = jnp.where(new_real, new_pos, past + SEL_BLOCK) <= lim_new
        o_t = _two_part_attention([(_nt(q4, k_t) * scale, msk, v_t), (s_new, msk_new, vsn)])
        o_sel = jnp.where(row_t == t, o_t, o_sel)

    n_buf = kwb_ref.shape[2]
    buf_pos = past - n_buf + lax.broadcasted_iota(jnp.int32, (1, n_buf), 1)
    msk_buf = jnp.where(buf_pos <= pos, pos - buf_pos, WINDOW) < WINDOW
    msk_wn = jnp.where(new_real & (new_pos <= pos), pos - new_pos, WINDOW) < WINDOW
    o_win = _two_part_attention([
        (_nt(q4, kwb_ref[0, 0].astype(BF16)) * scale, msk_buf, vwb_ref[0, 0].astype(BF16)),
        (_nt(q4, padded(kwn_ref)) * scale, msk_wn, padded(vwn_ref))])

    gates = jax.nn.sigmoid(sm_ref[0])
    ocmp, nz = ocmp_ref[0], nz_ref[0]
    for g in range(grp):
        head = kvh * grp + g
        rs = slice(g * t_new, (g + 1) * t_new)
        sl = slice(g * HEAD_DIM, (g + 1) * HEAD_DIM)
        o = (_lane_pick(gates, head) * ocmp[:, sl]
             + (_lane_pick(gates, n_heads + head) * o_sel[rs] + _lane_pick(gates, 2 * n_heads + head) * o_win[rs]))
        o_ref[0, :, sl] = (o * _silu(nz[:, sl])).astype(o_ref.dtype)


def _selwin(sel_idx_flat, pt_flat, cache3, layer, qr3, ksr3, win_state4, kwr3, small3, ocmp3, proj3, mo, nsa_w,
            past, n_pages, page):
    b, t_new, _ = qr3.shape
    n_heads = nsa_w // HEAD_DIM
    kvh_n = n_heads // NSA_GROUP
    gw = NSA_GROUP * HEAD_DIM
    n_sel = past // SEL_BLOCK + 1
    n_top = min(N_SELECT, n_sel)
    n_buf = win_state4.shape[2]
    new_k = pl.BlockSpec((1, t_new, HEAD_DIM), lambda bi, k, *_: (bi, 0, k))
    new_v = pl.BlockSpec((1, t_new, HEAD_DIM), lambda bi, k, *_: (bi, 0, kvh_n + k))
    return pl.pallas_call(
        functools.partial(_selwin_kernel, layer=layer, past=past, t_new=t_new, kvh_n=kvh_n, n_pages=n_pages,
                          page=page, n_heads=n_heads),
        grid_spec=pltpu.PrefetchScalarGridSpec(
            num_scalar_prefetch=2,
            grid=(b, kvh_n),
            in_specs=[pl.BlockSpec(memory_space=pl.ANY),
                      pl.BlockSpec((1, t_new, gw), lambda bi, k, *_: (bi, 0, k)),
                      new_k, new_v,
                      pl.BlockSpec((1, 1, n_buf, HEAD_DIM), lambda bi, k, *_: (layer, bi, 0, k)),
                      pl.BlockSpec((1, 1, n_buf, HEAD_DIM), lambda bi, k, *_: (layer, bi, 0, kvh_n + k)),
                      new_k, new_v,
                      pl.BlockSpec((1, t_new, LANES), lambda bi, k, *_: (bi, 0, 0)),
                      pl.BlockSpec((1, t_new, gw), lambda bi, k, *_: (bi, 0, k)),
                      pl.BlockSpec((1, t_new, gw), lambda bi, k, *_: (bi, 0, mo['nsa_z'] // gw + k))],
            out_specs=pl.BlockSpec((1, t_new, gw), lambda bi, k, *_: (bi, 0, k)),
            scratch_shapes=[pltpu.VMEM((t_new * n_top, SEL_BLOCK, HEAD_DIM), F32),
                            pltpu.VMEM((t_new * n_top, SEL_BLOCK, HEAD_DIM), F32),
                            pltpu.SemaphoreType.DMA((2,))]),
        out_shape=jax.ShapeDtypeStruct((b, t_new, nsa_w), BF16),
        compiler_params=_cparams(("arbitrary", "arbitrary")),
        name="sel_win_sample",
    )(sel_idx_flat, pt_flat, cache3, qr3, ksr3, ksr3, win_state4, win_state4, kwr3, kwr3, small3, ocmp3, proj3)


def _conv_kernel(x_ref, b_ref, c_ref, z_ref, buf_ref, w_ref, o_ref, st_ref, prev):
    tc = x_ref.shape[1]

    @pl.when(pl.program_id(1) == 0)
    def _():
        prev[0:CONV_K - 1, :] = buf_ref[0]

    u = c_ref[0] * x_ref[0]
    row = lax.broadcasted_iota(jnp.int32, (tc, 1), 0)
    w = w_ref[...]
    y = u * w[CONV_K - 1:CONV_K, :]
    for back in range(1, CONV_K):
        shifted = pltpu.roll(u, back, axis=0)
        for r in range(back):
            shifted = jnp.where(row == r, prev[CONV_K - 1 - back + r:CONV_K - back + r, :], shifted)
        y = y + shifted * w[CONV_K - 1 - back:CONV_K - back, :]
    o_ref[0] = (b_ref[0] * y * _silu(z_ref[0])).astype(o_ref.dtype)
    tail = u[tc - (CONV_K - 1):, :]
    prev[0:CONV_K - 1, :] = tail
    st_ref[0] = tail


def _conv(proj3, buf, w, mo, conv_w):
    b, seq, _ = proj3.shape
    tc = _tile(seq, 512)
    assert tc >= CONV_K - 1
    col = lambda name: pl.BlockSpec((1, tc, conv_w), lambda bi, ti: (bi, ti, mo[name] // conv_w))
    return pl.pallas_call(
        _conv_kernel,
        grid=(b, seq // tc),
        in_specs=[col('conv_x'), col('conv_b'), col('conv_c'), col('conv_z'),
                  pl.BlockSpec((1, CONV_K - 1, conv_w), lambda bi, ti: (bi, 0, 0)),
                  pl.BlockSpec((CONV_K, conv_w), lambda bi, ti: (0, 0))],
        out_specs=[pl.BlockSpec((1, tc, conv_w), lambda bi, ti: (bi, ti, 0)),
                   pl.BlockSpec((1, CONV_K - 1, conv_w), lambda bi, ti: (bi, 0, 0))],
        out_shape=[jax.ShapeDtypeStruct((b, seq, conv_w), BF16),
                   jax.ShapeDtypeStruct((b, CONV_K - 1, conv_w), F32)],
        scratch_shapes=[pltpu.VMEM((8, conv_w), F32)],
        compiler_params=_cparams(("arbitrary", "arbitrary")),
        name="short_conv",
    )(proj3, proj3, proj3, proj3, buf, w)


def _log_sigmoid(x):
    return jnp.minimum(x, 0.0) - jnp.log(1.0 + jnp.exp(-jnp.abs(x)))


def _mlstm_kernel(q_ref, k_ref, v_ref, og_ref, z_ref, sm_ref, gr_ref, c0_ref, n0_ref, m0_ref, nw_ref,
                  o_ref, c_out, n_out, m_out, c_s, n_s, m_s, *, n_chunks, gate_i, gate_f):
    h, ci = pl.program_id(1), pl.program_id(2)
    lc, dk = q_ref.shape[1], q_ref.shape[2]

    @pl.when(ci == 0)
    def _():
        c_s[...] = c0_ref[0, 0]
        n_s[...] = n0_ref[0, 0]
        m_s[...] = m0_ref[0, 0]

    c_prev, n_prev, m_prev = c_s[...], n_s[...], m_s[...]
    q, k, v = q_ref[0], k_ref[0], v_ref[0]
    small = sm_ref[0]
    i_col = _lane_pick(small, gate_i + h)
    lf_col = _log_sigmoid(_lane_pick(small, gate_f + h))
    gr = gr_ref[0]
    sub = lax.broadcasted_iota(jnp.int32, gr.shape, 0)
    i_row = jnp.sum(jnp.where(sub == h, gr, 0.0), axis=0, keepdims=True)
    lf_row = _log_sigmoid(jnp.sum(jnp.where(sub == MLSTM_HEADS + h, gr, 0.0), axis=0, keepdims=True))

    ti = lax.broadcasted_iota(jnp.int32, (lc, 1), 0)
    si = lax.broadcasted_iota(jnp.int32, (1, lc), 1)
    causal = si <= ti
    b_col = jnp.sum(jnp.where(causal, lf_row, 0.0), axis=1, keepdims=True)
    b_row = jnp.sum(jnp.where(ti <= si, lf_col, 0.0), axis=0, keepdims=True)
    log_d = jnp.where(causal, b_col - b_row + i_row, -jnp.inf)
    m_t = jnp.maximum(b_col + m_prev, jnp.max(log_d, axis=1, keepdims=True))
    inter = jnp.exp(b_col + m_prev - m_t)
    kf = k * (dk ** -0.5)
    qb, kb, vb = q.astype(BF16), kf.astype(BF16), v.astype(BF16)
    s = _nt(qb, kb) * jnp.exp(log_d - m_t)
    num = (inter * jnp.dot(qb, c_prev.astype(BF16), preferred_element_type=F32)
           + jnp.dot(s.astype(BF16), vb, preferred_element_type=F32))
    den = inter * jnp.sum(q * n_prev, axis=1, keepdims=True) + jnp.sum(s, axis=1, keepdims=True)
    hid = num / jnp.maximum(jnp.abs(den), jnp.exp(-m_t))

    m_new = m_t[lc - 1:lc, :]
    b_last = b_col[lc - 1:lc, :]
    wdec = jnp.exp(b_last - b_col + i_col - m_new)
    cdec = jnp.exp(b_last + m_prev - m_new)
    c_new = cdec * c_prev + lax.dot_general(kb, (wdec * v).astype(BF16), (((0,), (0,)), ((), ())),
                                            preferred_element_type=F32)
    n_new = cdec * n_prev + jnp.sum(wdec * kf, axis=0, keepdims=True)
    c_s[...] = c_new
    n_s[...] = n_new
    m_s[...] = m_new

    hn = hid * lax.rsqrt(jnp.mean(hid * hid, axis=-1, keepdims=True) + RMS_EPS) * nw_ref[0]
    o_ref[0] = (jax.nn.sigmoid(og_ref[0]) * hn * _silu(z_ref[0])).astype(o_ref.dtype)

    @pl.when(ci == n_chunks - 1)
    def _():
        c_out[0, 0] = c_new
        n_out[0, 0] = n_new
        m_out[0, 0] = m_new


def _mlstm(proj3, small3, gates_row, c0, n0, m0, norm_w, mo, ml_w, gate_i, gate_f):
    b, seq, _ = proj3.shape
    dk = ml_w // MLSTM_HEADS
    lc = _tile(seq, 256)
    n_chunks = seq // lc
    col = lambda name: pl.BlockSpec((1, lc, dk), lambda bi, h, ci: (bi, ci, mo[name] // dk + h))
    state = lambda shape: pl.BlockSpec((1, 1) + shape, lambda bi, h, ci: (bi, h, 0, 0))
    return pl.pallas_call(
        functools.partial(_mlstm_kernel, n_chunks=n_chunks, gate_i=gate_i, gate_f=gate_f),
        grid=(b, MLSTM_HEADS, n_chunks),
        in_specs=[col('m_q'), col('m_k'), col('m_v'), col('m_o'), col('m_z'),
                  pl.BlockSpec((1, lc, LANES), lambda bi, h, ci: (bi, ci, 0)),
                  pl.BlockSpec((1, 2 * MLSTM_HEADS, lc), lambda bi, h, ci: (bi, 0, ci)),
                  state((dk, dk)), state((1, dk)), state((1, 1)),
                  pl.BlockSpec((1, 1, dk), lambda bi, h, ci: (h, 0, 0))],
        out_specs=[pl.BlockSpec((1, lc, dk), lambda bi, h, ci: (bi, ci, h)),
                   state((dk, dk)), state((1, dk)), state((1, 1))],
        out_shape=[jax.ShapeDtypeStruct((b, seq, ml_w), BF16),
                   jax.ShapeDtypeStruct((b, MLSTM_HEADS, dk, dk), F32),
                   jax.ShapeDtypeStruct((b, MLSTM_HEADS, 1, dk), F32),
                   jax.ShapeDtypeStruct((b, MLSTM_HEADS, 1, 1), F32)],
        scratch_shapes=[pltpu.VMEM((dk, dk), F32), pltpu.VMEM((1, dk), F32), pltpu.VMEM((1, 1), F32)],
        compiler_params=_cparams(("arbitrary", "arbitrary", "arbitrary")),
        name="mlstm",
    )(proj3, proj3, proj3, proj3, proj3, small3, gates_row, c0, n0, m0, norm_w)


def _outproj_kernel(x_ref, g_ref, a_ref, c_ref, m_ref, w_ref, o_ref):
    ka, kc = a_ref.shape[1], c_ref.shape[1]
    acc = jnp.dot(a_ref[...], w_ref[0:ka, :], preferred_element_type=F32)
    acc = acc + jnp.dot(c_ref[...], w_ref[ka:ka + kc, :], preferred_element_type=F32)
    acc = acc + jnp.dot(m_ref[...], w_ref[ka + kc:, :], preferred_element_type=F32)
    o_ref[...] = x_ref[...] + g_ref[0] * acc


def _outproj(x2, gate3, o_nsa, o_conv, o_ml, w_out, tm, rows_per_mod):
    m, d = x2.shape
    tn = _tile(d, 1024)
    r = gate3.shape[1]
    kdim = w_out.shape[0]
    lhs = lambda a: pl.BlockSpec((tm, a.shape[1]), lambda i, j: (i, 0))
    return pl.pallas_call(
        _outproj_kernel,
        grid=(m // tm, d // tn),
        in_specs=[pl.BlockSpec((tm, tn), lambda i, j: (i, j)),
                  pl.BlockSpec((1, r, tn), lambda i, j: (i // rows_per_mod, 0, j)),
                  lhs(o_nsa), lhs(o_conv), lhs(o_ml),
                  pl.BlockSpec((kdim, tn), lambda i, j: (0, j))],
        out_specs=pl.BlockSpec((tm, tn), lambda i, j: (i, j)),
        out_shape=jax.ShapeDtypeStruct((m, d), F32),
        compiler_params=_cparams(("arbitrary", "arbitrary")),
        name="outproj",
    )(x2, gate3, o_nsa, o_conv, o_ml, w_out)


def _rmsnorm_kernel(x_ref, w_ref, o_ref):
    x = x_ref[...]
    o_ref[...] = x * lax.rsqrt(jnp.mean(x * x, axis=-1, keepdims=True) + RMS_EPS) * w_ref[...]


def _rmsnorm(x2, w):
    m, d = x2.shape
    tm = _tile(m, 256)
    return pl.pallas_call(
        _rmsnorm_kernel,
        grid=(m // tm,),
        in_specs=[pl.BlockSpec((tm, d), lambda i: (i, 0)), pl.BlockSpec((1, d), lambda i: (0, 0))],
        out_specs=pl.BlockSpec((tm, d), lambda i: (i, 0)),
        out_shape=jax.ShapeDtypeStruct((m, d), F32),
        compiler_params=_cparams(("arbitrary",)),
        name="final_norm",
    )(x2, w.reshape(1, d))


def _layer(x3, mod, lw, cfg, past, states):
    b, seq, d = x3.shape
    m = b * seq
    mo, nsa_w, kv_w, conv_w, ml_w = cfg['mo'], cfg['nsa_w'], cfg['kv_w'], cfg['conv_w'], cfg['ml_w']
    shift, scale, gate = mod
    x2 = x3.reshape(m, d)
    if seq >= 512:
        tm, rows_per_mod = 512, seq // 512
        per = lambda a: a.reshape(b, 1, d)
    else:
        tm, rows_per_mod = m, 1
        per = lambda a: jnp.repeat(a, seq, axis=0).reshape(1, m, d)
    proj, small = _inproj(x2, per(scale), per(shift), lw['norm_w'], lw['w_main'], lw['b_main'], lw['w_small'],
                          lw['b_small'], tm, rows_per_mod)
    nm = proj.shape[1]
    proj3, small3 = proj.reshape(b, seq, nm), small.reshape(b, seq, LANES)

    pos = past + jnp.arange(seq, dtype=jnp.int32)
    cos, sin = _rope_tables(pos)
    qr, ksr, kwr = _rope(proj, jnp.tile(cos, (b, 1)), jnp.tile(sin, (b, 1)), mo, nsa_w, kv_w)
    qr3, ksr3, kwr3 = qr.reshape(b, seq, nsa_w), ksr.reshape(b, seq, kv_w), kwr.reshape(b, seq, kv_w)

    if states is None:
        o_nsa = _nsa_prompt(proj3, small3, qr3, ksr3, kwr3, lw['cmp_pe'], lw['cmp_w'], mo, nsa_w)
        conv_buf = jnp.zeros((b, CONV_K - 1, conv_w), F32)
        dk = ml_w // MLSTM_HEADS
        c0 = jnp.zeros((b, MLSTM_HEADS, dk, dk), F32)
        n0 = jnp.zeros((b, MLSTM_HEADS, 1, dk), F32)
        m0 = jnp.zeros((b, MLSTM_HEADS, 1, 1), F32)
    else:
        n_pages, page = states['n_pages'], states['page']
        summ_e, summ_o = _summaries(states['cache_cmp'], states['layer'], states['pt_flat'], lw['pe_full'], b, n_pages)
        o_cmp, sel_idx = _cmpsel(summ_e, summ_o, proj3, lw['cmp_w'], mo, nsa_w, past)
        n_top = min(N_SELECT, past // SEL_BLOCK + 1)
        sel_flat = sel_idx[:, :, :n_top].reshape(-1)
        o_nsa = _selwin(sel_flat, states['pt_flat'], states['cache_sel'], states['layer'], qr3, ksr3,
                        states['win'], kwr3, small3, o_cmp, proj3, mo, nsa_w, past, n_pages, page)
        conv_buf = states['conv']
        dk = ml_w // MLSTM_HEADS
        c0 = states['C']
        n0 = states['n'].reshape(b, MLSTM_HEADS, 1, dk)
        m0 = states['m'].reshape(b, MLSTM_HEADS, 1, 1)

    o_conv, new_conv = _conv(proj3, conv_buf, lw['conv_w'], mo, conv_w)
    gates_row = jnp.swapaxes(small3[:, :, cfg['gate_i']:cfg['gate_i'] + 2 * MLSTM_HEADS], 1, 2)
    o_ml, c1, n1, m1 = _mlstm(proj3, small3, gates_row, c0, n0, m0, lw['mlstm_norm_w'], mo, ml_w,
                              cfg['gate_i'], cfg['gate_f'])

    y = _outproj(x2, per(gate), o_nsa.reshape(m, nsa_w), o_conv.reshape(m, conv_w), o_ml.reshape(m, ml_w),
                 lw['w_out'], tm, rows_per_mod)
    kv_shape = (b, seq, 2, kv_w // (2 * HEAD_DIM), HEAD_DIM)
    new_cmp = proj3[:, :, mo['kv_cmp']:mo['kv_cmp'] + kv_w].reshape(kv_shape)
    dk = ml_w // MLSTM_HEADS
    st = (new_cmp, ksr3.reshape(kv_shape), kwr3.reshape(kv_shape), new_conv, c1,
          n1.reshape(b, MLSTM_HEADS, dk), m1.reshape(b, MLSTM_HEADS))
    return y.reshape(b, seq, d), st


def kernel(x_prompt, x_sample, cache_cmp_kv, cache_sel_kv, state_win_kv, state_conv, state_mlstm_C, state_mlstm_n,
           state_mlstm_m, page_table, c_prompt, c_sample, norm_w, w_ada, b_ada, w_in, b_in, cmp_pe, cmp_w, conv_w,
           mlstm_norm_w, w_out, final_norm_w):
    bp, seq, d = x_prompt.shape
    bs, t_new, _ = x_sample.shape
    depth = w_in.shape[0]
    layout = _in_layout(d)
    sizes = dict(layout)
    in_off, _ = _offsets([n for n, _ in layout], sizes)
    mo, n_main = _offsets(MAIN_NAMES, sizes)
    so, n_small = _offsets(SMALL_NAMES, sizes)
    assert n_small <= LANES
    nsa_w, kv_w, conv_wd, ml_w = sizes['nsa_q'], sizes['kv_cmp'], sizes['conv_x'], sizes['m_q']
    cfg = dict(mo=mo, nsa_w=nsa_w, kv_w=kv_w, conv_w=conv_wd, ml_w=ml_w, gate_i=so['m_i'], gate_f=so['m_f'])
    n_kv = kv_w // (2 * HEAD_DIM)

    def gather_cols(a, names, pad_to):
        parts = [a[..., in_off[n]:in_off[n] + sizes[n]] for n in names]
        width = sum(sizes[n] for n in names)
        if pad_to > width:
            parts.append(jnp.zeros(a.shape[:-1] + (pad_to - width,), a.dtype))
        return jnp.concatenate(parts, axis=-1)

    layers = []
    for l in range(depth):
        pe_full = jnp.concatenate([jnp.tile(cmp_pe[l, c], (1, n_kv)) for c in range(2)], axis=1)
        layers.append(dict(
            norm_w=norm_w[l].reshape(1, d),
            w_main=gather_cols(w_in[l], MAIN_NAMES, n_main).astype(BF16),
            b_main=gather_cols(b_in[l], MAIN_NAMES, n_main).reshape(1, n_main),
            w_small=gather_cols(w_in[l], SMALL_NAMES, LANES).astype(BF16),
            b_small=gather_cols(b_in[l], SMALL_NAMES, LANES).reshape(1, LANES),
            cmp_pe=cmp_pe[l], cmp_w=cmp_w[l], pe_full=pe_full, conv_w=conv_w[l],
            mlstm_norm_w=mlstm_norm_w[l].reshape(MLSTM_HEADS, 1, ml_w // MLSTM_HEADS),
            w_out=w_out[l].astype(BF16)))

    n_rows = bp + bs
    pad_rows = -(-n_rows // 8) * 8
    c_all = jnp.concatenate([c_prompt, c_sample, jnp.zeros((pad_rows - n_rows, d), F32)], axis=0)
    mod = _ada(c_all, w_ada, b_ada)

    def mods(l, lo, hi):
        return tuple(mod[l, lo:hi, i * d:(i + 1) * d] for i in range(3))

    xp = x_prompt
    ps = []
    for l in range(depth):
        xp, st = _layer(xp, mods(l, 0, bp), layers[l], cfg, 0, None)
        ps.append(st)

    n_phys, page = cache_cmp_kv.shape[1], cache_cmp_kv.shape[2]
    n_pages = page_table.shape[1]
    past = n_pages * page
    assert past % SEL_BLOCK == 0 and t_new <= CMP_BLOCK and state_win_kv.shape[2] <= WINDOW
    pt_flat = page_table.reshape(-1).astype(jnp.int32)
    cache_cmp4 = cache_cmp_kv.reshape(depth, n_phys, page, kv_w)
    cache_sel3 = cache_sel_kv.reshape(depth, n_phys * page, kv_w)
    win4 = state_win_kv.reshape(depth, bs, state_win_kv.shape[2], kv_w)
    xs = x_sample
    ss = []
    for l in range(depth):
        states = dict(layer=l, n_pages=n_pages, page=page, pt_flat=pt_flat, cache_cmp=cache_cmp4,
                      cache_sel=cache_sel3, win=win4, conv=state_conv[l], C=state_mlstm_C[l], n=state_mlstm_n[l],
                      m=state_mlstm_m[l])
        xs, st = _layer(xs, mods(l, bp, bp + bs), layers[l], cfg, past, states)
        ss.append(st)

    y_prompt = _rmsnorm(xp.reshape(bp * seq, d), final_norm_w).reshape(bp, seq, d)
    y_sample = _rmsnorm(xs.reshape(bs * t_new, d), final_norm_w).reshape(bs, t_new, d)
    n_keep = min(WINDOW, seq)
    stack = lambda sts, i: jnp.stack([s[i] for s in sts])
    p_win = jnp.stack([s[2][:, seq - n_keep:] for s in ps])
    return (y_prompt, y_sample, stack(ps, 0), stack(ps, 1), p_win, stack(ps, 3), stack(ps, 4), stack(ps, 5),
            stack(ps, 6), stack(ss, 0), stack(ss, 1), stack(ss, 2), stack(ss, 3), stack(ss, 4), stack(ss, 5),
            stack(ss, 6))
```

```python
import functools

import jax
import jax.numpy as jnp
from jax import lax
from jax.experimental import pallas as pl
from jax.experimental.pallas import tpu as pltpu

HEAD_DIM = 128
NSA_GROUP = 4
CMP_BLOCK = 32
SEL_BLOCK = 64
N_SELECT = 16
WINDOW = 512
FORCE_BONUS = 1.0e4
CONV_K = 3
MLSTM_HEADS = 4
ROPE_THETA = 10000.0
RMS_EPS = 1e-6
NEG_BIG = -1e30
TINY = 1e-30

LANES = 128
VMEM_LIMIT = 56 * 1024 * 1024

F32 = jnp.float32
BF16 = jnp.bfloat16

MAIN_NAMES = ('nsa_q', 'kv_cmp', 'kv_sel', 'kv_win', 'nsa_z', 'conv_x', 'conv_b', 'conv_c', 'conv_z',
              'm_q', 'm_k', 'm_v', 'm_o', 'm_z')
SMALL_NAMES = ('nsa_gate', 'm_i', 'm_f')


def _in_layout(d_model):
    nsa_w, kv_w, conv_w, ml_w = d_model // 2, d_model // 4, d_model // 4, d_model // 4
    nsa_heads = nsa_w // HEAD_DIM
    return (('nsa_q', nsa_w), ('kv_cmp', kv_w), ('kv_sel', kv_w), ('kv_win', kv_w),
            ('nsa_gate', 3 * nsa_heads), ('nsa_z', nsa_w),
            ('conv_x', conv_w), ('conv_b', conv_w), ('conv_c', conv_w), ('conv_z', conv_w),
            ('m_q', ml_w), ('m_k', ml_w), ('m_v', ml_w),
            ('m_i', MLSTM_HEADS), ('m_f', MLSTM_HEADS), ('m_o', ml_w), ('m_z', ml_w))


def _offsets(names, sizes):
    off, out = 0, {}
    for n in names:
        out[n] = off
        off += sizes[n]
    return out, off


def _cparams(sem):
    return pltpu.CompilerParams(dimension_semantics=sem, vmem_limit_bytes=VMEM_LIMIT)


def _tile(dim, pref):
    t = min(dim, pref)
    assert dim % t == 0, (dim, pref)
    return t


def _nt(a, b):
    return lax.dot_general(a, b, (((1,), (1,)), ((), ())), preferred_element_type=F32)


def _silu(x):
    return x * jax.nn.sigmoid(x)


def _lane_pick(x, col):
    lane = lax.broadcasted_iota(jnp.int32, x.shape, 1)
    return jnp.sum(jnp.where(lane == col, x, 0.0), axis=1, keepdims=True)


def _ada_kernel(c_ref, w_ref, b_ref, o_ref):
    a = _silu(c_ref[...]).astype(BF16)
    o_ref[0] = jnp.dot(a, w_ref[0].astype(BF16), preferred_element_type=F32) + b_ref[0]


def _ada(c_all, w_ada, b_ada):
    rows, d = c_all.shape
    depth, _, n = w_ada.shape
    tn = _tile(n, 512)
    return pl.pallas_call(
        _ada_kernel,
        grid=(depth, n // tn),
        in_specs=[pl.BlockSpec((rows, d), lambda l, j: (0, 0)),
                  pl.BlockSpec((1, d, tn), lambda l, j: (l, 0, j)),
                  pl.BlockSpec((1, 1, tn), lambda l, j: (l, 0, j))],
        out_specs=pl.BlockSpec((1, rows, tn), lambda l, j: (l, 0, j)),
        out_shape=jax.ShapeDtypeStruct((depth, rows, n), F32),
        compiler_params=_cparams(("arbitrary", "arbitrary")),
        name="ada",
    )(c_all, w_ada, b_ada.reshape(depth, 1, n))


def _modnorm_kernel(x_ref, sc_ref, sh_ref, nw_ref, h_ref):
    x = x_ref[...]
    y = x * lax.rsqrt(jnp.mean(x * x, axis=-1, keepdims=True) + RMS_EPS) * nw_ref[...]
    h_ref[...] = (y * (1.0 + sc_ref[0]) + sh_ref[0]).astype(h_ref.dtype)


def _modnorm(x2, scale3, shift3, nw, tm, rows_per_mod):
    m, d = x2.shape
    r = scale3.shape[1]
    mod_spec = pl.BlockSpec((1, r, d), lambda i: (i // rows_per_mod, 0, 0))
    return pl.pallas_call(
        _modnorm_kernel,
        grid=(m // tm,),
        in_specs=[pl.BlockSpec((tm, d), lambda i: (i, 0)), mod_spec, mod_spec,
                  pl.BlockSpec((1, d), lambda i: (0, 0))],
        out_specs=pl.BlockSpec((tm, d), lambda i: (i, 0)),
        out_shape=jax.ShapeDtypeStruct((m, d), BF16),
        compiler_params=_cparams(("arbitrary",)),
        name="modnorm",
    )(x2, scale3, shift3, nw)


def _inproj_kernel(h_ref, wm_ref, bm_ref, ws_ref, bs_ref, o_ref, os_ref):
    @pl.when(pl.program_id(1) == 0)
    def _():
        os_ref[...] = jnp.dot(h_ref[...], ws_ref[...], preferred_element_type=F32) + bs_ref[...]

    o_ref[...] = jnp.dot(h_ref[...], wm_ref[...], preferred_element_type=F32) + bm_ref[...]


def _inproj(h2, w_main, b_main, w_small, b_small):
    m, d = h2.shape
    nm = w_main.shape[1]
    tm, tn = _tile(m, 1024), _tile(nm, 1024)
    return pl.pallas_call(
        _inproj_kernel,
        grid=(m // tm, nm // tn),
        in_specs=[pl.BlockSpec((tm, d), lambda i, j: (i, 0)),
                  pl.BlockSpec((d, tn), lambda i, j: (0, j)),
                  pl.BlockSpec((1, tn), lambda i, j: (0, j)),
                  pl.BlockSpec((d, LANES), lambda i, j: (0, 0)),
                  pl.BlockSpec((1, LANES), lambda i, j: (0, 0))],
        out_specs=[pl.BlockSpec((tm, tn), lambda i, j: (i, j)),
                   pl.BlockSpec((tm, LANES), lambda i, j: (i, 0))],
        out_shape=[jax.ShapeDtypeStruct((m, nm), F32), jax.ShapeDtypeStruct((m, LANES), F32)],
        compiler_params=_cparams(("arbitrary", "arbitrary")),
        name="inproj",
    )(h2, w_main, b_main, w_small, b_small)


def _rope_kernel(q_ref, ks_ref, kw_ref, cos_ref, sin_ref, qr_ref, ksr_ref, kwr_ref, *, n_q, n_kv):
    cos, sin = cos_ref[...], sin_ref[...]

    def rot(x):
        return x * cos + pltpu.roll(x, HEAD_DIM // 2, axis=1) * sin

    for h in range(n_q):
        sl = slice(h * HEAD_DIM, (h + 1) * HEAD_DIM)
        qr_ref[:, sl] = (rot(q_ref[:, sl]) * HEAD_DIM ** -0.5).astype(qr_ref.dtype)
    for src, dst in ((ks_ref, ksr_ref), (kw_ref, kwr_ref)):
        for h in range(n_kv):
            sl = slice(h * HEAD_DIM, (h + 1) * HEAD_DIM)
            dst[:, sl] = rot(src[:, sl])
        dst[:, n_kv * HEAD_DIM:] = src[:, n_kv * HEAD_DIM:]


def _rope(proj, cos, sin, mo, nsa_w, kv_w):
    m = proj.shape[0]
    tr = _tile(m, 256)
    n_q, n_kv = nsa_w // HEAD_DIM, kv_w // (2 * HEAD_DIM)
    return pl.pallas_call(
        functools.partial(_rope_kernel, n_q=n_q, n_kv=n_kv),
        grid=(m // tr,),
        in_specs=[pl.BlockSpec((tr, nsa_w), lambda i: (i, mo['nsa_q'] // nsa_w)),
                  pl.BlockSpec((tr, kv_w), lambda i: (i, mo['kv_sel'] // kv_w)),
                  pl.BlockSpec((tr, kv_w), lambda i: (i, mo['kv_win'] // kv_w)),
                  pl.BlockSpec((tr, HEAD_DIM), lambda i: (i, 0)),
                  pl.BlockSpec((tr, HEAD_DIM), lambda i: (i, 0))],
        out_specs=[pl.BlockSpec((tr, nsa_w), lambda i: (i, 0)),
                   pl.BlockSpec((tr, kv_w), lambda i: (i, 0)),
                   pl.BlockSpec((tr, kv_w), lambda i: (i, 0))],
        out_shape=[jax.ShapeDtypeStruct((m, nsa_w), BF16),
                   jax.ShapeDtypeStruct((m, kv_w), F32),
                   jax.ShapeDtypeStruct((m, kv_w), F32)],
        compiler_params=_cparams(("arbitrary",)),
        name="rope",
    )(proj, proj, proj, cos, sin)


def _rope_tables(pos):
    half = HEAD_DIM // 2
    inv = ROPE_THETA ** (-jnp.arange(half, dtype=F32) / half)
    ang = pos.astype(F32)[:, None] * inv[None, :]
    cos, sin = jnp.cos(ang), jnp.sin(ang)
    return jnp.concatenate([cos, cos], axis=1), jnp.concatenate([-sin, sin], axis=1)


def _flash_step(carry, s, bias, v_bf):
    m, l, acc = carry
    s = s + bias
    m_new = jnp.maximum(m, jnp.max(s, axis=-1, keepdims=True))
    p = jnp.exp(s - m_new)
    alpha = jnp.exp(m - m_new)
    l = alpha * l + jnp.sum(p, axis=-1, keepdims=True)
    pv = jnp.dot(p.reshape(-1, p.shape[-1]).astype(BF16), v_bf, preferred_element_type=F32)
    acc = alpha * acc + pv.reshape(acc.shape)
    return m_new, l, acc


def _flash_init(shape_rows):
    return (jnp.full(shape_rows + (1,), NEG_BIG, F32), jnp.zeros(shape_rows + (1,), F32),
            jnp.zeros(shape_rows + (HEAD_DIM,), F32))


def _flash_out(carry):
    _, l, acc = carry
    return acc / jnp.maximum(l, TINY)


def _topk_rank(score, blk, n_blocks, axis=1):
    rank = jnp.zeros(score.shape, F32)
    for i in range(n_blocks):
        si = score[:, i:i + 1] if axis == 1 else score[i:i + 1, :]
        ge = jnp.where(si >= score, 1.0, 0.0)
        gt = jnp.where(si > score, 1.0, 0.0)
        rank = rank + jnp.where(blk > i, ge, gt)
    return rank


def _nsa_prompt_kernel(qc_ref, qr_ref, kc_ref, vc_ref, ks_ref, vs_ref, kw_ref, vw_ref, nz_ref, sm_ref,
                       pe_ref, wphi_ref, o_ref, kcs, vcs, *, seq, tq, tk, n_heads):
    kvh, qi = pl.program_id(1), pl.program_id(2)
    n_sel = seq // SEL_BLOCK
    n_cmp = 2 * n_sel
    grp = NSA_GROUP
    scale = HEAD_DIM ** -0.5

    @pl.when(qi == 0)
    def _():
        for c, (src, dst) in enumerate(((kc_ref, kcs), (vc_ref, vcs))):
            x = src[0].reshape(n_sel, 2 * CMP_BLOCK, HEAD_DIM)
            pe = pe_ref[c][None]
            ev = jnp.mean(x[:, :CMP_BLOCK, :] + pe, axis=1)
            od = jnp.mean(x[:, CMP_BLOCK:, :] + pe, axis=1)
            summ = jnp.concatenate([ev, od], axis=0).astype(BF16)
            dst[...] = jnp.dot(summ, wphi_ref[c].astype(BF16), preferred_element_type=F32)

    t0 = qi * tq
    pos = t0 + lax.broadcasted_iota(jnp.int32, (tq, 1), 0)
    pos_l = t0 + lax.broadcasted_iota(jnp.int32, (1, tq), 1)

    row_c = lax.broadcasted_iota(jnp.int32, (n_cmp, 1), 0)
    cblk = jnp.where(row_c < n_sel, 2 * row_c, 2 * (row_c - n_sel) + 1)
    cmp_ok = (cblk + 1) * CMP_BLOCK - 1 <= pos_l
    kcb, vcb = kcs[...].astype(BF16), vcs[...].astype(BF16)
    qc = qc_ref[0]
    imp = jnp.zeros((n_cmp, tq), F32)
    o_cmp = []
    for g in range(grp):
        s = _nt(kcb, qc[:, g * HEAD_DIM:(g + 1) * HEAD_DIM].astype(BF16)) * scale
        s = jnp.where(cmp_ok, s, NEG_BIG)
        s = s - jnp.max(s, axis=0, keepdims=True)
        p = jnp.where(cmp_ok, jnp.exp(s), 0.0)
        p = p / jnp.maximum(jnp.sum(p, axis=0, keepdims=True), TINY)
        imp = imp + p
        o_cmp.append(lax.dot_general(p.astype(BF16), vcb, (((0,), (0,)), ((), ())), preferred_element_type=F32))

    imp_sel = imp[:n_sel] + imp[n_sel:]
    blk = lax.broadcasted_iota(jnp.int32, (n_sel, 1), 0)
    q_blk = pos_l // SEL_BLOCK
    forced = (blk == 0) | (blk == q_blk) | (blk == q_blk - 1)
    score = jnp.where(forced, imp_sel + FORCE_BONUS, imp_sel)
    score = jnp.where(blk * SEL_BLOCK > pos_l, -jnp.inf, score)
    rank = _topk_rank(score, blk, n_sel, axis=0)
    selb = jnp.where(rank < N_SELECT, 1.0, 0.0).T.astype(BF16)

    qr = qr_ref[0]
    q4 = jnp.concatenate([qr[:, g * HEAD_DIM:(g + 1) * HEAD_DIM] for g in range(grp)], axis=0)

    def sel_body(c, carry):
        k0 = pl.multiple_of(c * tk, tk)
        kch = ks_ref[0, pl.ds(k0, tk), :].astype(BF16)
        vch = vs_ref[0, pl.ds(k0, tk), :].astype(BF16)
        kpos = k0 + lax.broadcasted_iota(jnp.int32, (1, tk), 1)
        row = lax.broadcasted_iota(jnp.int32, (n_sel, 1), 0)
        expand = jnp.where(kpos // SEL_BLOCK == row, 1.0, 0.0).astype(BF16)
        member = jnp.dot(selb, expand, preferred_element_type=F32)
        bias = jnp.where(jnp.where(kpos <= pos, member, 0.0) > 0.5, 0.0, NEG_BIG)
        s = _nt(q4, kch).reshape(grp, tq, tk)
        return _flash_step(carry, s, bias[None], vch)

    n_chunks = (t0 + tq + tk - 1) // tk
    o_sel = _flash_out(lax.fori_loop(0, n_chunks, sel_body, _flash_init((grp, tq))))

    def win_body(i, carry):
        k0 = pl.multiple_of((qi - i) * tq, tq)
        kch = kw_ref[0, pl.ds(k0, tq), :].astype(BF16)
        vch = vw_ref[0, pl.ds(k0, tq), :].astype(BF16)
        kpos = k0 + lax.broadcasted_iota(jnp.int32, (1, tq), 1)
        bias = jnp.where(jnp.where(kpos <= pos, pos - kpos, WINDOW) < WINDOW, 0.0, NEG_BIG)
        s = _nt(q4, kch).reshape(grp, tq, tq)
        return _flash_step(carry, s, bias[None], vch)

    n_back = jnp.minimum(qi, (WINDOW + tq - 1) // tq)
    o_win = _flash_out(lax.fori_loop(0, n_back + 1, win_body, _flash_init((grp, tq))))

    gates = jax.nn.sigmoid(sm_ref[0])
    nz = nz_ref[0]
    for g in range(grp):
        head = kvh * grp + g
        g_cmp = _lane_pick(gates, head)
        g_sel = _lane_pick(gates, n_heads + head)
        g_win = _lane_pick(gates, 2 * n_heads + head)
        o = g_cmp * o_cmp[g] + (g_sel * o_sel[g] + g_win * o_win[g])
        sl = slice(g * HEAD_DIM, (g + 1) * HEAD_DIM)
        o_ref[0, :, sl] = (o * _silu(nz[:, sl])).astype(o_ref.dtype)


def _nsa_prompt(proj3, small3, qr3, ksr3, kwr3, pe, wphi, mo, nsa_w):
    b, seq, _ = proj3.shape
    n_heads = nsa_w // HEAD_DIM
    kvh = n_heads // NSA_GROUP
    gw = NSA_GROUP * HEAD_DIM
    tq = _tile(seq, 256)
    tk = _tile(seq, 512)
    assert seq % (2 * CMP_BLOCK) == 0 and tk % SEL_BLOCK == 0
    assert N_SELECT >= 3, "the forced blocks (first, current, previous) must always fit in the selection"
    hb = lambda name: mo[name] // HEAD_DIM

    def head_spec(off_blocks):
        return pl.BlockSpec((1, seq, HEAD_DIM), lambda bi, k, qi: (bi, 0, off_blocks + k))

    return pl.pallas_call(
        functools.partial(_nsa_prompt_kernel, seq=seq, tq=tq, tk=tk, n_heads=n_heads),
        grid=(b, kvh, seq // tq),
        in_specs=[pl.BlockSpec((1, tq, gw), lambda bi, k, qi: (bi, qi, mo['nsa_q'] // gw + k)),
                  pl.BlockSpec((1, tq, gw), lambda bi, k, qi: (bi, qi, k)),
                  head_spec(hb('kv_cmp')), head_spec(hb('kv_cmp') + kvh),
                  head_spec(0), head_spec(kvh), head_spec(0), head_spec(kvh),
                  pl.BlockSpec((1, tq, gw), lambda bi, k, qi: (bi, qi, mo['nsa_z'] // gw + k)),
                  pl.BlockSpec((1, tq, LANES), lambda bi, k, qi: (bi, qi, 0)),
                  pl.BlockSpec((2, CMP_BLOCK, HEAD_DIM), lambda bi, k, qi: (0, 0, 0)),
                  pl.BlockSpec((2, HEAD_DIM, HEAD_DIM), lambda bi, k, qi: (0, 0, 0))],
        out_specs=pl.BlockSpec((1, tq, gw), lambda bi, k, qi: (bi, qi, k)),
        out_shape=jax.ShapeDtypeStruct((b, seq, nsa_w), BF16),
        scratch_shapes=[pltpu.VMEM((seq // CMP_BLOCK, HEAD_DIM), F32),
                        pltpu.VMEM((seq // CMP_BLOCK, HEAD_DIM), F32)],
        compiler_params=_cparams(("arbitrary", "arbitrary", "arbitrary")),
        name="nsa_prompt",
    )(proj3, qr3, proj3, proj3, ksr3, ksr3, kwr3, kwr3, proj3, small3, pe, wphi)


def _summ_kernel(pt_ref, *refs, n_pg, n_ch):
    del pt_ref
    x_refs, pe_ref, o_ref = refs[:n_pg], refs[n_pg], refs[n_pg + 1]
    pe = pe_ref[...][None]
    outs = []
    for r in x_refs:
        page = r[0, 0]
        n_blk = page.shape[0] // (CMP_BLOCK * n_ch)
        x = page.reshape(n_blk, CMP_BLOCK, n_ch, HEAD_DIM)
        outs.append(jnp.mean(x + pe, axis=1).reshape(n_blk * n_ch, HEAD_DIM))
    o_ref[0] = jnp.concatenate(outs, axis=0)


def _summaries(cache4, layer, pt_flat, pe_ch, n_batch, n_pages, n_ch):
    page_rows = cache4.shape[2]
    blk_per_page = page_rows // (CMP_BLOCK * n_ch)
    n_pg = _tile(n_pages, 8)
    rows = n_pg * blk_per_page * n_ch

    def page_spec(k):
        return pl.BlockSpec((1, 1, page_rows, HEAD_DIM),
                            lambda bi, j, pt: (layer, pt[bi * n_pages + j * n_pg + k], 0, 0))

    return pl.pallas_call(
        functools.partial(_summ_kernel, n_pg=n_pg, n_ch=n_ch),
        grid_spec=pltpu.PrefetchScalarGridSpec(
            num_scalar_prefetch=1,
            grid=(n_batch, n_pages // n_pg),
            in_specs=[page_spec(k) for k in range(n_pg)]
            + [pl.BlockSpec((CMP_BLOCK, n_ch, HEAD_DIM), lambda bi, j, pt: (0, 0, 0))],
            out_specs=pl.BlockSpec((1, rows, HEAD_DIM), lambda bi, j, pt: (bi, j, 0))),
        out_shape=jax.ShapeDtypeStruct((n_batch, n_pages * blk_per_page * n_ch, HEAD_DIM), F32),
        compiler_params=_cparams(("arbitrary", "arbitrary")),
        name="cmp_summaries",
    )(pt_flat, *([cache4] * n_pg), pe_ch)


def _cmpsel_kernel(summ_ref, q_ref, wphi_ref, ocmp_ref, idx_ref, *, past, t_new, kvh_n, n_selp):
    grp = NSA_GROUP
    scale = HEAD_DIM ** -0.5
    n_ch = 2 * kvh_n
    n_half = summ_ref.shape[1] // (2 * n_ch)
    n_sel = n_half + 1

    def summaries(ch, odd):
        return summ_ref[0, pl.ds(odd * n_ch + ch, n_half, stride=2 * n_ch), :]
    tpos = lax.broadcasted_iota(jnp.int32, (t_new, 1), 0) + past
    pos = jnp.concatenate([tpos] * grp, axis=0)
    lane = lax.broadcasted_iota(jnp.int32, (1, n_half), 1)
    ok_e = (2 * lane + 1) * CMP_BLOCK - 1 <= pos
    ok_o = (2 * lane + 2) * CMP_BLOCK - 1 <= pos
    w_k, w_v = wphi_ref[0].astype(BF16), wphi_ref[1].astype(BF16)
    q = q_ref[0]
    scores = []
    for k in range(kvh_n):
        proj = lambda ch, odd, w: jnp.dot(summaries(ch, odd).astype(BF16), w,
                                          preferred_element_type=F32).astype(BF16)
        k_e, k_o = proj(k, 0, w_k), proj(k, 1, w_k)
        v_e, v_o = proj(kvh_n + k, 0, w_v), proj(kvh_n + k, 1, w_v)
        qk = jnp.concatenate(
            [q[:, (k * grp + g) * HEAD_DIM:(k * grp + g + 1) * HEAD_DIM] for g in range(grp)], axis=0).astype(BF16)
        s_e = jnp.where(ok_e, _nt(qk, k_e) * scale, NEG_BIG)
        s_o = jnp.where(ok_o, _nt(qk, k_o) * scale, NEG_BIG)
        m = jnp.maximum(jnp.max(s_e, axis=-1, keepdims=True), jnp.max(s_o, axis=-1, keepdims=True))
        p_e = jnp.where(ok_e, jnp.exp(s_e - m), 0.0)
        p_o = jnp.where(ok_o, jnp.exp(s_o - m), 0.0)
        den = jnp.maximum(jnp.sum(p_e, axis=-1, keepdims=True) + jnp.sum(p_o, axis=-1, keepdims=True), TINY)
        p_e, p_o = p_e / den, p_o / den
        o = (jnp.dot(p_e.astype(BF16), v_e, preferred_element_type=F32)
             + jnp.dot(p_o.astype(BF16), v_o, preferred_element_type=F32))
        imp = jnp.zeros((t_new, n_half), F32)
        for g in range(grp):
            hsl = slice((k * grp + g) * HEAD_DIM, (k * grp + g + 1) * HEAD_DIM)
            ocmp_ref[0, :, hsl] = o[g * t_new:(g + 1) * t_new]
            imp = imp + (p_e[g * t_new:(g + 1) * t_new] + p_o[g * t_new:(g + 1) * t_new])
        tail_lane = lax.broadcasted_iota(jnp.int32, (t_new, n_selp - n_half), 1)
        tail = jnp.where(tail_lane == 0, 0.0, -jnp.inf)
        scores.append(jnp.concatenate([imp, tail], axis=1))
    score = jnp.concatenate(scores, axis=0)
    pos_r = jnp.concatenate([tpos] * kvh_n, axis=0)
    blk = lax.broadcasted_iota(jnp.int32, (1, n_selp), 1)
    q_blk = pos_r // SEL_BLOCK
    forced = (blk == 0) | (blk == q_blk) | (blk == q_blk - 1)
    score = jnp.where(forced, score + FORCE_BONUS, score)
    score = jnp.where((blk * SEL_BLOCK > pos_r) | (blk >= n_sel), -jnp.inf, score)
    rank = _topk_rank(score, blk, n_sel)
    blk_f = blk.astype(F32)
    out_lane = lax.broadcasted_iota(jnp.int32, (kvh_n * t_new, LANES), 1)
    out = jnp.zeros((kvh_n * t_new, LANES), F32)
    for r in range(min(N_SELECT, n_sel)):
        pick = jnp.sum(jnp.where(rank == r, blk_f, 0.0), axis=1, keepdims=True)
        out = jnp.where(out_lane == r, pick, out)
    idx_ref[0] = out.astype(jnp.int32)


def _cmpsel(summ, proj3, wphi, mo, nsa_w, past):
    b, summ_rows, _ = summ.shape
    t_new = proj3.shape[1]
    kvh_n = nsa_w // HEAD_DIM // NSA_GROUP
    n_half = summ_rows // (4 * kvh_n)
    assert n_half == past // SEL_BLOCK
    n_selp = -(-(n_half + 1) // LANES) * LANES
    return pl.pallas_call(
        functools.partial(_cmpsel_kernel, past=past, t_new=t_new, kvh_n=kvh_n, n_selp=n_selp),
        grid=(b,),
        in_specs=[pl.BlockSpec((1, summ_rows, HEAD_DIM), lambda bi: (bi, 0, 0)),
                  pl.BlockSpec((1, t_new, nsa_w), lambda bi: (bi, 0, mo['nsa_q'] // nsa_w)),
                  pl.BlockSpec((2, HEAD_DIM, HEAD_DIM), lambda bi: (0, 0, 0))],
        out_specs=[pl.BlockSpec((1, t_new, nsa_w), lambda bi: (bi, 0, 0)),
                   pl.BlockSpec((1, kvh_n * t_new, LANES), lambda bi: (bi, 0, 0))],
        out_shape=[jax.ShapeDtypeStruct((b, t_new, nsa_w), F32),
                   jax.ShapeDtypeStruct((b, kvh_n * t_new, LANES), jnp.int32)],
        compiler_params=_cparams(("arbitrary",)),
        name="cmp_select",
    )(summ, proj3, wphi)


def _two_part_attention(parts):
    m = None
    for s, msk, _ in parts:
        mx = jnp.max(jnp.where(msk, s, NEG_BIG), axis=-1, keepdims=True)
        m = mx if m is None else jnp.maximum(m, mx)
    den, acc = 0.0, 0.0
    for s, msk, v in parts:
        p = jnp.where(msk, jnp.exp(jnp.where(msk, s, NEG_BIG) - m), 0.0)
        den = den + jnp.sum(p, axis=-1, keepdims=True)
        acc = acc + jnp.dot(p.astype(BF16), v, preferred_element_type=F32)
    return acc / jnp.maximum(den, TINY)


def _selwin_kernel(idx_ref, pt_ref, cache_ref, qr_ref, ksn_ref, vsn_ref, wbuf_ref, kwn_ref, vwn_ref,
                   sm_ref, ocmp_ref, nz_ref, o_ref, kbuf, vbuf, sem,
                   *, layer, past, t_new, kvh_n, n_pages, page, n_heads):
    bi, kvh = pl.program_id(0), pl.program_id(1)
    grp = NSA_GROUP
    n_ch = 2 * kvh_n
    n_top = kbuf.shape[0] // t_new
    n_past_blk = past // SEL_BLOCK
    per_page = page // SEL_BLOCK
    base = (bi * kvh_n + kvh) * t_new * n_top

    def copies(i, row0):
        return (pltpu.make_async_copy(cache_ref.at[layer, pl.ds(row0, SEL_BLOCK), kvh, :], kbuf.at[i], sem.at[0]),
                pltpu.make_async_copy(cache_ref.at[layer, pl.ds(row0, SEL_BLOCK), kvh_n + kvh, :], vbuf.at[i],
                                      sem.at[1]))

    def issue(i, carry):
        idc = jnp.minimum(idx_ref[base + i], n_past_blk - 1)
        phys = pt_ref[bi * n_pages + idc // per_page]
        row0 = pl.multiple_of(phys * page + (idc % per_page) * SEL_BLOCK, SEL_BLOCK)
        for cp in copies(i, row0):
            cp.start()
        return carry

    def wait(i, carry):
        for cp in copies(i, 0):
            cp.wait()
        return carry

    lax.fori_loop(0, t_new * n_top, issue, 0)
    lax.fori_loop(0, t_new * n_top, wait, 0)

    qr = qr_ref[0]
    q4 = jnp.concatenate([qr[:, g * HEAD_DIM:(g + 1) * HEAD_DIM] for g in range(grp)], axis=0)
    rows = grp * t_new
    row_t = jnp.concatenate([lax.broadcasted_iota(jnp.int32, (t_new, 1), 0)] * grp, axis=0)
    pos = row_t + past

    def padded(ref):
        x = ref[0]
        return jnp.concatenate([x, jnp.zeros((SEL_BLOCK - t_new, HEAD_DIM), x.dtype)], axis=0).astype(BF16)

    new_j = lax.broadcasted_iota(jnp.int32, (1, SEL_BLOCK), 1)
    new_pos = past + new_j
    new_real = new_j < t_new

    ksn, vsn = padded(ksn_ref), padded(vsn_ref)
    s_new = _nt(q4, ksn)
    lane = lax.broadcasted_iota(jnp.int32, (1, n_top * SEL_BLOCK), 1)
    o_sel = jnp.zeros((rows, HEAD_DIM), F32)
    for t in range(t_new):
        idx_vec = jnp.zeros((1, n_top * SEL_BLOCK), jnp.int32)
        has_new = jnp.int32(0)
        for n in range(n_top):
            idx = idx_ref[base + t * n_top + n]
            idx_vec = jnp.where(lane // SEL_BLOCK == n, idx, idx_vec)
            has_new = has_new | (idx == n_past_blk).astype(jnp.int32)
        kpos = idx_vec * SEL_BLOCK + lane % SEL_BLOCK
        msk = jnp.where(idx_vec < n_past_blk, kpos, past + t + 1) <= past + t
        k_t = kbuf[t * n_top:(t + 1) * n_top].reshape(n_top * SEL_BLOCK, HEAD_DIM).astype(BF16)
        v_t = vbuf[t * n_top:(t + 1) * n_top].reshape(n_top * SEL_BLOCK, HEAD_DIM).astype(BF16)
        lim_new = jnp.where(has_new > 0, past + t, -1)
        msk_new = jnp.where(new_real, new_pos, past + SEL_BLOCK) <= lim_new
        o_t = _two_part_attention([(_nt(q4, k_t), msk, v_t), (s_new, msk_new, vsn)])
        o_sel = jnp.where(row_t == t, o_t, o_sel)

    n_buf = wbuf_ref.shape[1] // n_ch
    kwb = wbuf_ref[0, pl.ds(kvh, n_buf, stride=n_ch), :].astype(BF16)
    vwb = wbuf_ref[0, pl.ds(kvh_n + kvh, n_buf, stride=n_ch), :].astype(BF16)
    buf_pos = past - n_buf + lax.broadcasted_iota(jnp.int32, (1, n_buf), 1)
    msk_buf = jnp.where(buf_pos <= pos, pos - buf_pos, WINDOW) < WINDOW
    msk_wn = jnp.where(new_real & (new_pos <= pos), pos - new_pos, WINDOW) < WINDOW
    o_win = _two_part_attention([(_nt(q4, kwb), msk_buf, vwb),
                                 (_nt(q4, padded(kwn_ref)), msk_wn, padded(vwn_ref))])

    gates = jax.nn.sigmoid(sm_ref[0])
    ocmp, nz = ocmp_ref[0], nz_ref[0]
    for g in range(grp):
        head = kvh * grp + g
        rs = slice(g * t_new, (g + 1) * t_new)
        sl = slice(g * HEAD_DIM, (g + 1) * HEAD_DIM)
        o = (_lane_pick(gates, head) * ocmp[:, sl]
             + (_lane_pick(gates, n_heads + head) * o_sel[rs] + _lane_pick(gates, 2 * n_heads + head) * o_win[rs]))
        o_ref[0, :, sl] = (o * _silu(nz[:, sl])).astype(o_ref.dtype)


def _selwin(sel_idx_flat, pt_flat, cache4, layer, qr3, ksr3, win_state3, kwr3, small3, ocmp3, proj3, mo, nsa_w,
            past, n_pages, page):
    b, t_new, _ = qr3.shape
    n_heads = nsa_w // HEAD_DIM
    kvh_n = n_heads // NSA_GROUP
    gw = NSA_GROUP * HEAD_DIM
    n_sel = past // SEL_BLOCK + 1
    n_top = min(N_SELECT, n_sel)
    win_rows = win_state3.shape[1]
    new_k = pl.BlockSpec((1, t_new, HEAD_DIM), lambda bi, k, *_: (bi, 0, k))
    new_v = pl.BlockSpec((1, t_new, HEAD_DIM), lambda bi, k, *_: (bi, 0, kvh_n + k))
    return pl.pallas_call(
        functools.partial(_selwin_kernel, layer=layer, past=past, t_new=t_new, kvh_n=kvh_n, n_pages=n_pages,
                          page=page, n_heads=n_heads),
        grid_spec=pltpu.PrefetchScalarGridSpec(
            num_scalar_prefetch=2,
            grid=(b, kvh_n),
            in_specs=[pl.BlockSpec(memory_space=pl.ANY),
                      pl.BlockSpec((1, t_new, gw), lambda bi, k, *_: (bi, 0, k)),
                      new_k, new_v,
                      pl.BlockSpec((1, win_rows, HEAD_DIM), lambda bi, k, *_: (layer * b + bi, 0, 0)),
                      new_k, new_v,
                      pl.BlockSpec((1, t_new, LANES), lambda bi, k, *_: (bi, 0, 0)),
                      pl.BlockSpec((1, t_new, gw), lambda bi, k, *_: (bi, 0, k)),
                      pl.BlockSpec((1, t_new, gw), lambda bi, k, *_: (bi, 0, mo['nsa_z'] // gw + k))],
            out_specs=pl.BlockSpec((1, t_new, gw), lambda bi, k, *_: (bi, 0, k)),
            scratch_shapes=[pltpu.VMEM((t_new * n_top, SEL_BLOCK, HEAD_DIM), F32),
                            pltpu.VMEM((t_new * n_top, SEL_BLOCK, HEAD_DIM), F32),
                            pltpu.SemaphoreType.DMA((2,))]),
        out_shape=jax.ShapeDtypeStruct((b, t_new, nsa_w), BF16),
        compiler_params=_cparams(("arbitrary", "arbitrary")),
        name="sel_win_sample",
    )(sel_idx_flat, pt_flat, cache4, qr3, ksr3, ksr3, win_state3, kwr3, kwr3, small3, ocmp3, proj3)


def _conv_kernel(x_ref, b_ref, c_ref, z_ref, buf_ref, w_ref, o_ref, st_ref, prev):
    tc = x_ref.shape[1]

    @pl.when(pl.program_id(1) == 0)
    def _():
        prev[0:CONV_K - 1, :] = buf_ref[0]

    u = c_ref[0] * x_ref[0]
    row = lax.broadcasted_iota(jnp.int32, (tc, 1), 0)
    w = w_ref[...]
    y = u * w[CONV_K - 1:CONV_K, :]
    for back in range(1, CONV_K):
        shifted = pltpu.roll(u, back, axis=0)
        for r in range(back):
            shifted = jnp.where(row == r, prev[CONV_K - 1 - back + r:CONV_K - back + r, :], shifted)
        y = y + shifted * w[CONV_K - 1 - back:CONV_K - back, :]
    o_ref[0] = (b_ref[0] * y * _silu(z_ref[0])).astype(o_ref.dtype)
    tail = u[tc - (CONV_K - 1):, :]
    prev[0:CONV_K - 1, :] = tail
    st_ref[0] = tail


def _conv(proj3, buf, w, mo, conv_w):
    b, seq, _ = proj3.shape
    tc = _tile(seq, 512)
    assert tc >= CONV_K - 1
    col = lambda name: pl.BlockSpec((1, tc, conv_w), lambda bi, ti: (bi, ti, mo[name] // conv_w))
    return pl.pallas_call(
        _conv_kernel,
        grid=(b, seq // tc),
        in_specs=[col('conv_x'), col('conv_b'), col('conv_c'), col('conv_z'),
                  pl.BlockSpec((1, CONV_K - 1, conv_w), lambda bi, ti: (bi, 0, 0)),
                  pl.BlockSpec((CONV_K, conv_w), lambda bi, ti: (0, 0))],
        out_specs=[pl.BlockSpec((1, tc, conv_w), lambda bi, ti: (bi, ti, 0)),
                   pl.BlockSpec((1, CONV_K - 1, conv_w), lambda bi, ti: (bi, 0, 0))],
        out_shape=[jax.ShapeDtypeStruct((b, seq, conv_w), BF16),
                   jax.ShapeDtypeStruct((b, CONV_K - 1, conv_w), F32)],
        scratch_shapes=[pltpu.VMEM((8, conv_w), F32)],
        compiler_params=_cparams(("arbitrary", "arbitrary")),
        name="short_conv",
    )(proj3, proj3, proj3, proj3, buf, w)


def _log_sigmoid(x):
    return jnp.minimum(x, 0.0) - jnp.log(1.0 + jnp.exp(-jnp.abs(x)))


def _mlstm_kernel(q_ref, k_ref, v_ref, og_ref, z_ref, sm_ref, gr_ref, c0_ref, n0_ref, m0_ref, nw_ref,
                  o_ref, c_out, n_out, m_out, c_s, n_s, m_s, *, n_chunks, gate_i, gate_f):
    h, ci = pl.program_id(1), pl.program_id(2)
    lc, dk = q_ref.shape[1], q_ref.shape[2]

    @pl.when(ci == 0)
    def _():
        c_s[...] = c0_ref[0, 0]
        n_s[...] = n0_ref[0, 0]
        m_s[...] = m0_ref[0, 0]

    c_prev, n_prev, m_prev = c_s[...], n_s[...], m_s[...]
    q, k, v = q_ref[0], k_ref[0], v_ref[0]
    small = sm_ref[0]
    i_col = _lane_pick(small, gate_i + h)
    lf_col = _log_sigmoid(_lane_pick(small, gate_f + h))
    gr = gr_ref[0]
    sub = lax.broadcasted_iota(jnp.int32, gr.shape, 0)
    i_row = jnp.sum(jnp.where(sub == h, gr, 0.0), axis=0, keepdims=True)
    lf_row = _log_sigmoid(jnp.sum(jnp.where(sub == MLSTM_HEADS + h, gr, 0.0), axis=0, keepdims=True))

    ti = lax.broadcasted_iota(jnp.int32, (lc, 1), 0)
    si = lax.broadcasted_iota(jnp.int32, (1, lc), 1)
    causal = si <= ti
    b_col = jnp.sum(jnp.where(causal, lf_row, 0.0), axis=1, keepdims=True)
    b_row = jnp.sum(jnp.where(ti <= si, lf_col, 0.0), axis=0, keepdims=True)
    log_d = jnp.where(causal, b_col - b_row + i_row, -jnp.inf)
    m_t = jnp.maximum(b_col + m_prev, jnp.max(log_d, axis=1, keepdims=True))
    inter = jnp.exp(b_col + m_prev - m_t)
    kf = k * (dk ** -0.5)
    qb, kb, vb = q.astype(BF16), kf.astype(BF16), v.astype(BF16)
    s = _nt(qb, kb) * jnp.exp(log_d - m_t)
    num = (inter * jnp.dot(qb, c_prev.astype(BF16), preferred_element_type=F32)
           + jnp.dot(s.astype(BF16), vb, preferred_element_type=F32))
    den = inter * jnp.sum(q * n_prev, axis=1, keepdims=True) + jnp.sum(s, axis=1, keepdims=True)
    hid = num / jnp.maximum(jnp.abs(den), jnp.exp(-m_t))

    m_new = m_t[lc - 1:lc, :]
    b_last = b_col[lc - 1:lc, :]
    wdec = jnp.exp(b_last - b_col + i_col - m_new)
    cdec = jnp.exp(b_last + m_prev - m_new)
    c_new = cdec * c_prev + lax.dot_general(kb, (wdec * v).astype(BF16), (((0,), (0,)), ((), ())),
                                            preferred_element_type=F32)
    n_new = cdec * n_prev + jnp.sum(wdec * kf, axis=0, keepdims=True)
    c_s[...] = c_new
    n_s[...] = n_new
    m_s[...] = m_new

    hn = hid * lax.rsqrt(jnp.mean(hid * hid, axis=-1, keepdims=True) + RMS_EPS) * nw_ref[0]
    o_ref[0] = (jax.nn.sigmoid(og_ref[0]) * hn * _silu(z_ref[0])).astype(o_ref.dtype)

    @pl.when(ci == n_chunks - 1)
    def _():
        c_out[0, 0] = c_new
        n_out[0, 0] = n_new
        m_out[0, 0] = m_new


def _mlstm(proj3, small3, gates_row, c0, n0, m0, norm_w, mo, ml_w, gate_i, gate_f):
    b, seq, _ = proj3.shape
    dk = ml_w // MLSTM_HEADS
    lc = _tile(seq, 256)
    n_chunks = seq // lc
    col = lambda name: pl.BlockSpec((1, lc, dk), lambda bi, h, ci: (bi, ci, mo[name] // dk + h))
    state = lambda shape: pl.BlockSpec((1, 1) + shape, lambda bi, h, ci: (bi, h, 0, 0))
    return pl.pallas_call(
        functools.partial(_mlstm_kernel, n_chunks=n_chunks, gate_i=gate_i, gate_f=gate_f),
        grid=(b, MLSTM_HEADS, n_chunks),
        in_specs=[col('m_q'), col('m_k'), col('m_v'), col('m_o'), col('m_z'),
                  pl.BlockSpec((1, lc, LANES), lambda bi, h, ci: (bi, ci, 0)),
                  pl.BlockSpec((1, 2 * MLSTM_HEADS, lc), lambda bi, h, ci: (bi, 0, ci)),
                  state((dk, dk)), state((1, dk)), state((1, 1)),
                  pl.BlockSpec((1, 1, dk), lambda bi, h, ci: (h, 0, 0))],
        out_specs=[pl.BlockSpec((1, lc, dk), lambda bi, h, ci: (bi, ci, h)),
                   state((dk, dk)), state((1, dk)), state((1, 1))],
        out_shape=[jax.ShapeDtypeStruct((b, seq, ml_w), BF16),
                   jax.ShapeDtypeStruct((b, MLSTM_HEADS, dk, dk), F32),
                   jax.ShapeDtypeStruct((b, MLSTM_HEADS, 1, dk), F32),
                   jax.ShapeDtypeStruct((b, MLSTM_HEADS, 1, 1), F32)],
        scratch_shapes=[pltpu.VMEM((dk, dk), F32), pltpu.VMEM((1, dk), F32), pltpu.VMEM((1, 1), F32)],
        compiler_params=_cparams(("arbitrary", "arbitrary", "arbitrary")),
        name="mlstm",
    )(proj3, proj3, proj3, proj3, proj3, small3, gates_row, c0, n0, m0, norm_w)


def _outproj_kernel(x_ref, g_ref, a_ref, c_ref, m_ref, w_ref, o_ref):
    ka, kc = a_ref.shape[1], c_ref.shape[1]
    acc = jnp.dot(a_ref[...], w_ref[0:ka, :], preferred_element_type=F32)
    acc = acc + jnp.dot(c_ref[...], w_ref[ka:ka + kc, :], preferred_element_type=F32)
    acc = acc + jnp.dot(m_ref[...], w_ref[ka + kc:, :], preferred_element_type=F32)
    o_ref[...] = x_ref[...] + g_ref[0] * acc


def _outproj(x2, gate3, o_nsa, o_conv, o_ml, w_out, tm, rows_per_mod):
    m, d = x2.shape
    tn = _tile(d, 1024)
    r = gate3.shape[1]
    kdim = w_out.shape[0]
    lhs = lambda a: pl.BlockSpec((tm, a.shape[1]), lambda i, j: (i, 0))
    return pl.pallas_call(
        _outproj_kernel,
        grid=(m // tm, d // tn),
        in_specs=[pl.BlockSpec((tm, tn), lambda i, j: (i, j)),
                  pl.BlockSpec((1, r, tn), lambda i, j: (i // rows_per_mod, 0, j)),
                  lhs(o_nsa), lhs(o_conv), lhs(o_ml),
                  pl.BlockSpec((kdim, tn), lambda i, j: (0, j))],
        out_specs=pl.BlockSpec((tm, tn), lambda i, j: (i, j)),
        out_shape=jax.ShapeDtypeStruct((m, d), F32),
        compiler_params=_cparams(("arbitrary", "arbitrary")),
        name="outproj",
    )(x2, gate3, o_nsa, o_conv, o_ml, w_out)


def _rmsnorm_kernel(x_ref, w_ref, o_ref):
    x = x_ref[...]
    o_ref[...] = x * lax.rsqrt(jnp.mean(x * x, axis=-1, keepdims=True) + RMS_EPS) * w_ref[...]


def _rmsnorm(x2, w):
    m, d = x2.shape
    tm = _tile(m, 256)
    return pl.pallas_call(
        _rmsnorm_kernel,
        grid=(m // tm,),
        in_specs=[pl.BlockSpec((tm, d), lambda i: (i, 0)), pl.BlockSpec((1, d), lambda i: (0, 0))],
        out_specs=pl.BlockSpec((tm, d), lambda i: (i, 0)),
        out_shape=jax.ShapeDtypeStruct((m, d), F32),
        compiler_params=_cparams(("arbitrary",)),
        name="final_norm",
    )(x2, w.reshape(1, d))


def _layer(x3, mod, lw, cfg, past, states):
    b, seq, d = x3.shape
    m = b * seq
    mo, nsa_w, kv_w, conv_w, ml_w = cfg['mo'], cfg['nsa_w'], cfg['kv_w'], cfg['conv_w'], cfg['ml_w']
    shift, scale, gate = mod
    x2 = x3.reshape(m, d)
    if seq >= 512:
        tm, rows_per_mod = 512, seq // 512
        per = lambda a: a.reshape(b, 1, d)
    else:
        tm, rows_per_mod = m, 1
        per = lambda a: jnp.repeat(a, seq, axis=0).reshape(1, m, d)
    h2 = _modnorm(x2, per(scale), per(shift), lw['norm_w'], tm, rows_per_mod)
    proj, small = _inproj(h2, lw['w_main'], lw['b_main'], lw['w_small'], lw['b_small'])
    nm = proj.shape[1]
    proj3, small3 = proj.reshape(b, seq, nm), small.reshape(b, seq, LANES)

    pos = past + jnp.arange(seq, dtype=jnp.int32)
    cos, sin = _rope_tables(pos)
    qr, ksr, kwr = _rope(proj, jnp.tile(cos, (b, 1)), jnp.tile(sin, (b, 1)), mo, nsa_w, kv_w)
    qr3, ksr3, kwr3 = qr.reshape(b, seq, nsa_w), ksr.reshape(b, seq, kv_w), kwr.reshape(b, seq, kv_w)

    if states is None:
        o_nsa = _nsa_prompt(proj3, small3, qr3, ksr3, kwr3, lw['cmp_pe'], lw['cmp_w'], mo, nsa_w)
        conv_buf = jnp.zeros((b, CONV_K - 1, conv_w), F32)
        dk = ml_w // MLSTM_HEADS
        c0 = jnp.zeros((b, MLSTM_HEADS, dk, dk), F32)
        n0 = jnp.zeros((b, MLSTM_HEADS, 1, dk), F32)
        m0 = jnp.zeros((b, MLSTM_HEADS, 1, 1), F32)
    else:
        n_pages, page = states['n_pages'], states['page']
        summ = _summaries(states['cache_cmp'], states['layer'], states['pt_flat'], lw['pe_ch'], b, n_pages,
                          kv_w // HEAD_DIM)
        o_cmp, sel_idx = _cmpsel(summ, proj3, lw['cmp_w'], mo, nsa_w, past)
        n_top = min(N_SELECT, past // SEL_BLOCK + 1)
        sel_flat = sel_idx[:, :, :n_top].reshape(-1)
        o_nsa = _selwin(sel_flat, states['pt_flat'], states['cache_sel'], states['layer'], qr3, ksr3,
                        states['win'], kwr3, small3, o_cmp, proj3, mo, nsa_w, past, n_pages, page)
        conv_buf = states['conv']
        dk = ml_w // MLSTM_HEADS
        c0 = states['C']
        n0 = states['n'].reshape(b, MLSTM_HEADS, 1, dk)
        m0 = states['m'].reshape(b, MLSTM_HEADS, 1, 1)

    o_conv, new_conv = _conv(proj3, conv_buf, lw['conv_w'], mo, conv_w)
    gates_row = jnp.swapaxes(small3[:, :, cfg['gate_i']:cfg['gate_i'] + 2 * MLSTM_HEADS], 1, 2)
    o_ml, c1, n1, m1 = _mlstm(proj3, small3, gates_row, c0, n0, m0, lw['mlstm_norm_w'], mo, ml_w,
                              cfg['gate_i'], cfg['gate_f'])

    y = _outproj(x2, per(gate), o_nsa.reshape(m, nsa_w), o_conv.reshape(m, conv_w), o_ml.reshape(m, ml_w),
                 lw['w_out'], tm, rows_per_mod)
    kv_shape = (b, seq, 2, kv_w // (2 * HEAD_DIM), HEAD_DIM)
    new_cmp = proj3[:, :, mo['kv_cmp']:mo['kv_cmp'] + kv_w].reshape(kv_shape)
    dk = ml_w // MLSTM_HEADS
    st = (new_cmp, ksr3.reshape(kv_shape), kwr3.reshape(kv_shape), new_conv, c1,
          n1.reshape(b, MLSTM_HEADS, dk), m1.reshape(b, MLSTM_HEADS))
    return y.reshape(b, seq, d), st


def kernel(x_prompt, x_sample, cache_cmp_kv, cache_sel_kv, state_win_kv, state_conv, state_mlstm_C, state_mlstm_n,
           state_mlstm_m, page_table, c_prompt, c_sample, norm_w, w_ada, b_ada, w_in, b_in, cmp_pe, cmp_w, conv_w,
           mlstm_norm_w, w_out, final_norm_w):
    bp, seq, d = x_prompt.shape
    bs, t_new, _ = x_sample.shape
    depth = w_in.shape[0]
    layout = _in_layout(d)
    sizes = dict(layout)
    in_off, _ = _offsets([n for n, _ in layout], sizes)
    mo, n_main = _offsets(MAIN_NAMES, sizes)
    so, n_small = _offsets(SMALL_NAMES, sizes)
    assert n_small <= LANES
    nsa_w, kv_w, conv_wd, ml_w = sizes['nsa_q'], sizes['kv_cmp'], sizes['conv_x'], sizes['m_q']
    cfg = dict(mo=mo, nsa_w=nsa_w, kv_w=kv_w, conv_w=conv_wd, ml_w=ml_w, gate_i=so['m_i'], gate_f=so['m_f'])
    n_kv = kv_w // (2 * HEAD_DIM)

    def gather_cols(a, names, pad_to):
        parts = [a[..., in_off[n]:in_off[n] + sizes[n]] for n in names]
        width = sum(sizes[n] for n in names)
        if pad_to > width:
            parts.append(jnp.zeros(a.shape[:-1] + (pad_to - width,), a.dtype))
        return jnp.concatenate(parts, axis=-1)

    layers = []
    for l in range(depth):
        pe_ch = jnp.repeat(jnp.transpose(cmp_pe[l], (1, 0, 2)), n_kv, axis=1)
        layers.append(dict(
            norm_w=norm_w[l].reshape(1, d),
            w_main=gather_cols(w_in[l], MAIN_NAMES, n_main).astype(BF16),
            b_main=gather_cols(b_in[l], MAIN_NAMES, n_main).reshape(1, n_main),
            w_small=gather_cols(w_in[l], SMALL_NAMES, LANES).astype(BF16),
            b_small=gather_cols(b_in[l], SMALL_NAMES, LANES).reshape(1, LANES),
            cmp_pe=cmp_pe[l], cmp_w=cmp_w[l], pe_ch=pe_ch, conv_w=conv_w[l],
            mlstm_norm_w=mlstm_norm_w[l].reshape(MLSTM_HEADS, 1, ml_w // MLSTM_HEADS),
            w_out=w_out[l].astype(BF16)))

    n_rows = bp + bs
    pad_rows = -(-n_rows // 8) * 8
    c_all = jnp.concatenate([c_prompt, c_sample, jnp.zeros((pad_rows - n_rows, d), F32)], axis=0)
    mod = _ada(c_all, w_ada, b_ada)

    def mods(l, lo, hi):
        return tuple(mod[l, lo:hi, i * d:(i + 1) * d] for i in range(3))

    xp = x_prompt
    ps = []
    for l in range(depth):
        xp, st = _layer(xp, mods(l, 0, bp), layers[l], cfg, 0, None)
        ps.append(st)

    n_phys, page = cache_cmp_kv.shape[1], cache_cmp_kv.shape[2]
    n_pages = page_table.shape[1]
    past = n_pages * page
    assert past % SEL_BLOCK == 0 and t_new <= CMP_BLOCK and state_win_kv.shape[2] <= WINDOW
    pt_flat = page_table.reshape(-1).astype(jnp.int32)
    n_ch = 2 * n_kv
    cache_cmp4 = cache_cmp_kv.reshape(depth, n_phys, page * n_ch, HEAD_DIM)
    cache_sel3 = cache_sel_kv.reshape(depth, n_phys * page, n_ch, HEAD_DIM)
    win4 = state_win_kv.reshape(depth * bs, state_win_kv.shape[2] * n_ch, HEAD_DIM)
    xs = x_sample
    ss = []
    for l in range(depth):
        states = dict(layer=l, n_pages=n_pages, page=page, pt_flat=pt_flat, cache_cmp=cache_cmp4,
                      cache_sel=cache_sel3, win=win4, conv=state_conv[l], C=state_mlstm_C[l], n=state_mlstm_n[l],
                      m=state_mlstm_m[l])
        xs, st = _layer(xs, mods(l, bp, bp + bs), layers[l], cfg, past, states)
        ss.append(st)

    y_prompt = _rmsnorm(xp.reshape(bp * seq, d), final_norm_w).reshape(bp, seq, d)
    y_sample = _rmsnorm(xs.reshape(bs * t_new, d), final_norm_w).reshape(bs, t_new, d)
    n_keep = min(WINDOW, seq)
    stack = lambda sts, i: jnp.stack([s[i] for s in sts])
    p_win = jnp.stack([s[2][:, seq - n_keep:] for s in ps])
    return (y_prompt, y_sample, stack(ps, 0), stack(ps, 1), p_win, stack(ps, 3), stack(ps, 4), stack(ps, 5),
            stack(ps, 6), stack(ss, 0), stack(ss, 1), stack(ss, 2), stack(ss, 3), stack(ss, 4), stack(ss, 5),
            stack(ss, 6))
```

```python
import functools

import jax
import jax.numpy as jnp
from jax import lax
from jax.experimental import pallas as pl
from jax.experimental.pallas import tpu as pltpu

HEAD_DIM = 128
NSA_GROUP = 4
CMP_BLOCK = 32
SEL_BLOCK = 64
N_SELECT = 16
WINDOW = 512
FORCE_BONUS = 1.0e4
CONV_K = 3
MLSTM_HEADS = 4
ROPE_THETA = 10000.0
RMS_EPS = 1e-6
NEG_BIG = -1e30
TINY = 1e-30

LANES = 128
VMEM_LIMIT = 56 * 1024 * 1024

F32 = jnp.float32
BF16 = jnp.bfloat16

MAIN_NAMES = ('nsa_q', 'kv_cmp', 'kv_sel', 'kv_win', 'nsa_z', 'conv_x', 'conv_b', 'conv_c', 'conv_z',
              'm_q', 'm_k', 'm_v', 'm_o', 'm_z')
SMALL_NAMES = ('nsa_gate', 'm_i', 'm_f')


def _in_layout(d_model):
    nsa_w, kv_w, conv_w, ml_w = d_model // 2, d_model // 4, d_model // 4, d_model // 4
    nsa_heads = nsa_w // HEAD_DIM
    return (('nsa_q', nsa_w), ('kv_cmp', kv_w), ('kv_sel', kv_w), ('kv_win', kv_w),
            ('nsa_gate', 3 * nsa_heads), ('nsa_z', nsa_w),
            ('conv_x', conv_w), ('conv_b', conv_w), ('conv_c', conv_w), ('conv_z', conv_w),
            ('m_q', ml_w), ('m_k', ml_w), ('m_v', ml_w),
            ('m_i', MLSTM_HEADS), ('m_f', MLSTM_HEADS), ('m_o', ml_w), ('m_z', ml_w))


def _offsets(names, sizes):
    off, out = 0, {}
    for n in names:
        out[n] = off
        off += sizes[n]
    return out, off


def _cparams(sem):
    return pltpu.CompilerParams(dimension_semantics=sem, vmem_limit_bytes=VMEM_LIMIT)


def _tile(dim, pref):
    t = min(dim, pref)
    assert dim % t == 0, (dim, pref)
    return t


def _nt(a, b):
    return lax.dot_general(a, b, (((1,), (1,)), ((), ())), preferred_element_type=F32)


def _silu(x):
    return x * jax.nn.sigmoid(x)


def _lane_pick(x, col):
    lane = lax.broadcasted_iota(jnp.int32, x.shape, 1)
    return jnp.sum(jnp.where(lane == col, x, 0.0), axis=1, keepdims=True)


def _ada_kernel(c_ref, w_ref, b_ref, o_ref):
    a = _silu(c_ref[...]).astype(BF16)
    o_ref[0] = jnp.dot(a, w_ref[0].astype(BF16), preferred_element_type=F32) + b_ref[0]


def _ada(c_all, w_ada, b_ada):
    rows, d = c_all.shape
    depth, _, n = w_ada.shape
    tn = _tile(n, 512)
    return pl.pallas_call(
        _ada_kernel,
        grid=(depth, n // tn),
        in_specs=[pl.BlockSpec((rows, d), lambda l, j: (0, 0)),
                  pl.BlockSpec((1, d, tn), lambda l, j: (l, 0, j)),
                  pl.BlockSpec((1, 1, tn), lambda l, j: (l, 0, j))],
        out_specs=pl.BlockSpec((1, rows, tn), lambda l, j: (l, 0, j)),
        out_shape=jax.ShapeDtypeStruct((depth, rows, n), F32),
        compiler_params=_cparams(("arbitrary", "arbitrary")),
        name="ada",
    )(c_all, w_ada, b_ada.reshape(depth, 1, n))


def _wprep_kernel(w_ref, wm_ref, ws_ref, *, segs_main, segs_small):
    for src, size, dst in segs_main:
        wm_ref[0, :, dst:dst + size] = w_ref[0, :, src:src + size].astype(wm_ref.dtype)
    ws_ref[0] = jnp.zeros(ws_ref.shape[1:], ws_ref.dtype)
    for src, size, dst in segs_small:
        ws_ref[0, :, dst:dst + size] = w_ref[0, :, src:src + size].astype(ws_ref.dtype)


def _wprep(w_in, segs_main, segs_small, n_main):
    depth, d, n_in = w_in.shape
    tk = _tile(d, 128)
    return pl.pallas_call(
        functools.partial(_wprep_kernel, segs_main=segs_main, segs_small=segs_small),
        grid=(depth, d // tk),
        in_specs=[pl.BlockSpec((1, tk, n_in), lambda l, i: (l, i, 0))],
        out_specs=[pl.BlockSpec((1, tk, n_main), lambda l, i: (l, i, 0)),
                   pl.BlockSpec((1, tk, LANES), lambda l, i: (l, i, 0))],
        out_shape=[jax.ShapeDtypeStruct((depth, d, n_main), BF16),
                   jax.ShapeDtypeStruct((depth, d, LANES), BF16)],
        compiler_params=_cparams(("arbitrary", "arbitrary")),
        name="weight_prep",
    )(w_in)


def _modnorm_kernel(x_ref, sc_ref, sh_ref, nw_ref, h_ref):
    x = x_ref[...]
    y = x * lax.rsqrt(jnp.mean(x * x, axis=-1, keepdims=True) + RMS_EPS) * nw_ref[...]
    h_ref[...] = (y * (1.0 + sc_ref[0]) + sh_ref[0]).astype(h_ref.dtype)


def _modnorm(x2, scale3, shift3, nw, tm, rows_per_mod):
    m, d = x2.shape
    r = scale3.shape[1]
    mod_spec = pl.BlockSpec((1, r, d), lambda i: (i // rows_per_mod, 0, 0))
    return pl.pallas_call(
        _modnorm_kernel,
        grid=(m // tm,),
        in_specs=[pl.BlockSpec((tm, d), lambda i: (i, 0)), mod_spec, mod_spec,
                  pl.BlockSpec((1, d), lambda i: (0, 0))],
        out_specs=pl.BlockSpec((tm, d), lambda i: (i, 0)),
        out_shape=jax.ShapeDtypeStruct((m, d), BF16),
        compiler_params=_cparams(("arbitrary",)),
        name="modnorm",
    )(x2, scale3, shift3, nw)


def _inproj_kernel(h_ref, wm_ref, bm_ref, ws_ref, bs_ref, o_ref, os_ref):
    @pl.when(pl.program_id(1) == 0)
    def _():
        os_ref[...] = jnp.dot(h_ref[...], ws_ref[0], preferred_element_type=F32) + bs_ref[...]

    o_ref[...] = jnp.dot(h_ref[...], wm_ref[0], preferred_element_type=F32) + bm_ref[...]


def _inproj(h2, w_main, b_main, w_small, b_small, layer):
    m, d = h2.shape
    nm = w_main.shape[2]
    tm, tn = _tile(m, 1024), _tile(nm, 1024)
    return pl.pallas_call(
        _inproj_kernel,
        grid=(m // tm, nm // tn),
        in_specs=[pl.BlockSpec((tm, d), lambda i, j: (i, 0)),
                  pl.BlockSpec((1, d, tn), lambda i, j: (layer, 0, j)),
                  pl.BlockSpec((1, tn), lambda i, j: (0, j)),
                  pl.BlockSpec((1, d, LANES), lambda i, j: (layer, 0, 0)),
                  pl.BlockSpec((1, LANES), lambda i, j: (0, 0))],
        out_specs=[pl.BlockSpec((tm, tn), lambda i, j: (i, j)),
                   pl.BlockSpec((tm, LANES), lambda i, j: (i, 0))],
        out_shape=[jax.ShapeDtypeStruct((m, nm), F32), jax.ShapeDtypeStruct((m, LANES), F32)],
        compiler_params=_cparams(("arbitrary", "arbitrary")),
        name="inproj",
    )(h2, w_main, b_main, w_small, b_small)


def _rope_kernel(q_ref, ks_ref, kw_ref, cos_ref, sin_ref, qr_ref, ksr_ref, kwr_ref, *, n_q, n_kv):
    cos, sin = cos_ref[...], sin_ref[...]

    def rot(x):
        return x * cos + pltpu.roll(x, HEAD_DIM // 2, axis=1) * sin

    for h in range(n_q):
        sl = slice(h * HEAD_DIM, (h + 1) * HEAD_DIM)
        qr_ref[:, sl] = (rot(q_ref[:, sl]) * HEAD_DIM ** -0.5).astype(qr_ref.dtype)
    for src, dst in ((ks_ref, ksr_ref), (kw_ref, kwr_ref)):
        for h in range(n_kv):
            sl = slice(h * HEAD_DIM, (h + 1) * HEAD_DIM)
            dst[:, sl] = rot(src[:, sl])
        dst[:, n_kv * HEAD_DIM:] = src[:, n_kv * HEAD_DIM:]


def _rope(proj, cos, sin, mo, nsa_w, kv_w):
    m = proj.shape[0]
    tr = _tile(m, 256)
    n_q, n_kv = nsa_w // HEAD_DIM, kv_w // (2 * HEAD_DIM)
    return pl.pallas_call(
        functools.partial(_rope_kernel, n_q=n_q, n_kv=n_kv),
        grid=(m // tr,),
        in_specs=[pl.BlockSpec((tr, nsa_w), lambda i: (i, mo['nsa_q'] // nsa_w)),
                  pl.BlockSpec((tr, kv_w), lambda i: (i, mo['kv_sel'] // kv_w)),
                  pl.BlockSpec((tr, kv_w), lambda i: (i, mo['kv_win'] // kv_w)),
                  pl.BlockSpec((tr, HEAD_DIM), lambda i: (i, 0)),
                  pl.BlockSpec((tr, HEAD_DIM), lambda i: (i, 0))],
        out_specs=[pl.BlockSpec((tr, nsa_w), lambda i: (i, 0)),
                   pl.BlockSpec((tr, kv_w), lambda i: (i, 0)),
                   pl.BlockSpec((tr, kv_w), lambda i: (i, 0))],
        out_shape=[jax.ShapeDtypeStruct((m, nsa_w), BF16),
                   jax.ShapeDtypeStruct((m, kv_w), F32),
                   jax.ShapeDtypeStruct((m, kv_w), F32)],
        compiler_params=_cparams(("arbitrary",)),
        name="rope",
    )(proj, proj, proj, cos, sin)


def _rope_tables(pos):
    half = HEAD_DIM // 2
    inv = ROPE_THETA ** (-jnp.arange(half, dtype=F32) / half)
    ang = pos.astype(F32)[:, None] * inv[None, :]
    cos, sin = jnp.cos(ang), jnp.sin(ang)
    return jnp.concatenate([cos, cos], axis=1), jnp.concatenate([-sin, sin], axis=1)


def _flash_step(carry, s, bias, v_bf):
    m, l, acc = carry
    if bias is not None:
        s = s + bias[None]
    m_new = jnp.maximum(m, jnp.max(s, axis=-1, keepdims=True))
    p = jnp.exp(s - m_new)
    alpha = jnp.exp(m - m_new)
    l = alpha * l + jnp.sum(p, axis=-1, keepdims=True)
    pv = jnp.dot(p.reshape(-1, p.shape[-1]).astype(BF16), v_bf, preferred_element_type=F32)
    acc = alpha * acc + pv.reshape(acc.shape)
    return m_new, l, acc


def _flash_init(shape_rows):
    return (jnp.full(shape_rows + (1,), NEG_BIG, F32), jnp.zeros(shape_rows + (1,), F32),
            jnp.zeros(shape_rows + (HEAD_DIM,), F32))


def _topk_rank(score, blk, n_blocks, axis=1):
    rank = jnp.zeros(score.shape, F32)
    for i in range(n_blocks):
        si = score[:, i:i + 1] if axis == 1 else score[i:i + 1, :]
        ge = jnp.where(si >= score, 1.0, 0.0)
        gt = jnp.where(si > score, 1.0, 0.0)
        rank = rank + jnp.where(blk > i, ge, gt)
    return rank


def _nsa_prompt_kernel(qc_ref, qr_ref, kc_ref, vc_ref, ks_ref, vs_ref, kw_ref, vw_ref, nz_ref, sm_ref,
                       pe_ref, wphi_ref, o_ref, kcs, vcs, *, seq, tq, tk, n_heads):
    kvh, qi = pl.program_id(1), pl.program_id(2)
    n_sel = seq // SEL_BLOCK
    n_cmp = 2 * n_sel
    grp = NSA_GROUP
    scale = HEAD_DIM ** -0.5

    @pl.when(qi == 0)
    def _():
        for c, (src, dst) in enumerate(((kc_ref, kcs), (vc_ref, vcs))):
            x = src[0].reshape(n_sel, 2 * CMP_BLOCK, HEAD_DIM)
            pe = pe_ref[c][None]
            ev = jnp.mean(x[:, :CMP_BLOCK, :] + pe, axis=1)
            od = jnp.mean(x[:, CMP_BLOCK:, :] + pe, axis=1)
            summ = jnp.concatenate([ev, od], axis=0).astype(BF16)
            dst[...] = jnp.dot(summ, wphi_ref[c].astype(BF16), preferred_element_type=F32)

    t0 = qi * tq
    pos = t0 + lax.broadcasted_iota(jnp.int32, (tq, 1), 0)
    pos_l = t0 + lax.broadcasted_iota(jnp.int32, (1, tq), 1)

    row_c = lax.broadcasted_iota(jnp.int32, (n_cmp, 1), 0)
    cblk = jnp.where(row_c < n_sel, 2 * row_c, 2 * (row_c - n_sel) + 1)
    cmp_ok = (cblk + 1) * CMP_BLOCK - 1 <= pos_l
    kcb, vcb = kcs[...].astype(BF16), vcs[...].astype(BF16)
    qc = qc_ref[0]
    imp = jnp.zeros((n_cmp, tq), F32)
    o_cmp = []
    for g in range(grp):
        s = _nt(kcb, qc[:, g * HEAD_DIM:(g + 1) * HEAD_DIM].astype(BF16)) * scale
        s = jnp.where(cmp_ok, s, NEG_BIG)
        s = s - jnp.max(s, axis=0, keepdims=True)
        p = jnp.where(cmp_ok, jnp.exp(s), 0.0)
        p = p / jnp.maximum(jnp.sum(p, axis=0, keepdims=True), TINY)
        imp = imp + p
        o_cmp.append(lax.dot_general(p.astype(BF16), vcb, (((0,), (0,)), ((), ())), preferred_element_type=F32))

    imp_sel = imp[:n_sel] + imp[n_sel:]
    blk = lax.broadcasted_iota(jnp.int32, (n_sel, 1), 0)
    q_blk = pos_l // SEL_BLOCK
    forced = (blk == 0) | (blk == q_blk) | (blk == q_blk - 1)
    score = jnp.where(forced, imp_sel + FORCE_BONUS, imp_sel)
    score = jnp.where(blk * SEL_BLOCK > pos_l, -jnp.inf, score)
    rank = _topk_rank(score, blk, n_sel, axis=0)
    sel_neg = jnp.concatenate([jnp.where(rank < N_SELECT, 0.0, NEG_BIG), jnp.zeros((LANES - n_sel, tq), F32)], axis=0)
    sel_neg = sel_neg.T.astype(BF16)

    qr = qr_ref[0]
    q4 = jnp.concatenate([qr[:, g * HEAD_DIM:(g + 1) * HEAD_DIM] for g in range(grp)], axis=0)
    q4_sel = jnp.concatenate([q4, jnp.concatenate([sel_neg] * grp, axis=0)], axis=1)

    def sel_keys(k0):
        kch = ks_ref[0, pl.ds(k0, tk), :].astype(BF16)
        kblk = (k0 + lax.broadcasted_iota(jnp.int32, (tk, 1), 0)) // SEL_BLOCK
        one_hot = jnp.where(kblk == lax.broadcasted_iota(jnp.int32, (1, LANES), 1), 1.0, 0.0).astype(BF16)
        return jnp.concatenate([kch, one_hot], axis=1), vs_ref[0, pl.ds(k0, tk), :].astype(BF16)

    def sel_step(k0, carry, bias):
        k_aug, vch = sel_keys(k0)
        return _flash_step(carry, _nt(q4_sel, k_aug).reshape(grp, tq, tk), bias, vch)

    n_chunks = (t0 + tq + tk - 1) // tk
    carry = lax.fori_loop(0, n_chunks - 1, lambda c, cr: sel_step(pl.multiple_of(c * tk, tk), cr, None),
                          _flash_init((grp, tq)))
    k_last = pl.multiple_of((n_chunks - 1) * tk, tk)
    kpos = k_last + lax.broadcasted_iota(jnp.int32, (1, tk), 1)
    _, l_s, acc_s = sel_step(k_last, carry, jnp.where(kpos <= pos, 0.0, NEG_BIG))

    def win_body(i, carry):
        k0 = pl.multiple_of((qi - i) * tq, tq)
        kpos = k0 + lax.broadcasted_iota(jnp.int32, (1, tq), 1)
        bias = jnp.where(jnp.where(kpos <= pos, pos - kpos, WINDOW) < WINDOW, 0.0, NEG_BIG)
        s = _nt(q4, kw_ref[0, pl.ds(k0, tq), :].astype(BF16)).reshape(grp, tq, tq)
        return _flash_step(carry, s, bias, vw_ref[0, pl.ds(k0, tq), :].astype(BF16))

    n_back = jnp.minimum(qi, (WINDOW + tq - 1) // tq)
    _, l_w, acc_w = lax.fori_loop(0, n_back + 1, win_body, _flash_init((grp, tq)))

    gates = jax.nn.sigmoid(sm_ref[0])
    nz = nz_ref[0]
    for g in range(grp):
        head = kvh * grp + g
        w_cmp = _lane_pick(gates, head)
        w_sel = _lane_pick(gates, n_heads + head) / jnp.maximum(l_s[g], TINY)
        w_win = _lane_pick(gates, 2 * n_heads + head) / jnp.maximum(l_w[g], TINY)
        o = w_cmp * o_cmp[g] + (w_sel * acc_s[g] + w_win * acc_w[g])
        sl = slice(g * HEAD_DIM, (g + 1) * HEAD_DIM)
        o_ref[0, :, sl] = (o * _silu(nz[:, sl])).astype(o_ref.dtype)


def _nsa_prompt(proj3, small3, qr3, ksr3, kwr3, pe, wphi, mo, nsa_w):
    b, seq, _ = proj3.shape
    n_heads = nsa_w // HEAD_DIM
    kvh = n_heads // NSA_GROUP
    gw = NSA_GROUP * HEAD_DIM
    tq = _tile(seq, 256)
    tk = _tile(seq, 512)
    assert seq % (2 * CMP_BLOCK) == 0 and tk % SEL_BLOCK == 0
    assert N_SELECT >= 3, "the forced blocks (first, current, previous) must always fit in the selection"
    assert seq // SEL_BLOCK <= LANES
    hb = lambda name: mo[name] // HEAD_DIM

    def head_spec(off_blocks):
        return pl.BlockSpec((1, seq, HEAD_DIM), lambda bi, k, qi: (bi, 0, off_blocks + k))

    return pl.pallas_call(
        functools.partial(_nsa_prompt_kernel, seq=seq, tq=tq, tk=tk, n_heads=n_heads),
        grid=(b, kvh, seq // tq),
        in_specs=[pl.BlockSpec((1, tq, gw), lambda bi, k, qi: (bi, qi, mo['nsa_q'] // gw + k)),
                  pl.BlockSpec((1, tq, gw), lambda bi, k, qi: (bi, qi, k)),
                  head_spec(hb('kv_cmp')), head_spec(hb('kv_cmp') + kvh),
                  head_spec(0), head_spec(kvh), head_spec(0), head_spec(kvh),
                  pl.BlockSpec((1, tq, gw), lambda bi, k, qi: (bi, qi, mo['nsa_z'] // gw + k)),
                  pl.BlockSpec((1, tq, LANES), lambda bi, k, qi: (bi, qi, 0)),
                  pl.BlockSpec((2, CMP_BLOCK, HEAD_DIM), lambda bi, k, qi: (0, 0, 0)),
                  pl.BlockSpec((2, HEAD_DIM, HEAD_DIM), lambda bi, k, qi: (0, 0, 0))],
        out_specs=pl.BlockSpec((1, tq, gw), lambda bi, k, qi: (bi, qi, k)),
        out_shape=jax.ShapeDtypeStruct((b, seq, nsa_w), BF16),
        scratch_shapes=[pltpu.VMEM((seq // CMP_BLOCK, HEAD_DIM), F32),
                        pltpu.VMEM((seq // CMP_BLOCK, HEAD_DIM), F32)],
        compiler_params=_cparams(("arbitrary", "arbitrary", "arbitrary")),
        name="nsa_prompt",
    )(proj3, qr3, proj3, proj3, ksr3, ksr3, kwr3, kwr3, proj3, small3, pe, wphi)


def _summ_kernel(pt_ref, *refs, n_pg, n_ch):
    del pt_ref
    x_refs, pe_ref, o_ref = refs[:n_pg], refs[n_pg], refs[n_pg + 1]
    pe = pe_ref[...][None]
    outs = []
    for r in x_refs:
        page = r[0, 0]
        n_blk = page.shape[0] // (CMP_BLOCK * n_ch)
        x = page.reshape(n_blk, CMP_BLOCK, n_ch, HEAD_DIM)
        outs.append(jnp.mean(x + pe, axis=1).reshape(n_blk * n_ch, HEAD_DIM))
    o_ref[0] = jnp.concatenate(outs, axis=0)


def _summaries(cache4, layer, pt_flat, pe_ch, n_batch, n_pages, n_ch):
    page_rows = cache4.shape[2]
    blk_per_page = page_rows // (CMP_BLOCK * n_ch)
    n_pg = _tile(n_pages, 8)
    rows = n_pg * blk_per_page * n_ch

    def page_spec(k):
        return pl.BlockSpec((1, 1, page_rows, HEAD_DIM),
                            lambda bi, j, pt: (layer, pt[bi * n_pages + j * n_pg + k], 0, 0))

    return pl.pallas_call(
        functools.partial(_summ_kernel, n_pg=n_pg, n_ch=n_ch),
        grid_spec=pltpu.PrefetchScalarGridSpec(
            num_scalar_prefetch=1,
            grid=(n_batch, n_pages // n_pg),
            in_specs=[page_spec(k) for k in range(n_pg)]
            + [pl.BlockSpec((CMP_BLOCK, n_ch, HEAD_DIM), lambda bi, j, pt: (0, 0, 0))],
            out_specs=pl.BlockSpec((1, rows, HEAD_DIM), lambda bi, j, pt: (bi, j, 0))),
        out_shape=jax.ShapeDtypeStruct((n_batch, n_pages * blk_per_page * n_ch, HEAD_DIM), F32),
        compiler_params=_cparams(("arbitrary", "arbitrary")),
        name="cmp_summaries",
    )(pt_flat, *([cache4] * n_pg), pe_ch)


def _cmpsel_kernel(summ_ref, q_ref, wphi_ref, ocmp_ref, idx_ref, *, past, t_new, kvh_n, n_selp):
    grp = NSA_GROUP
    scale = HEAD_DIM ** -0.5
    n_ch = 2 * kvh_n
    n_half = summ_ref.shape[1] // (2 * n_ch)
    n_sel = n_half + 1

    def summaries(ch, odd):
        return summ_ref[0, pl.ds(odd * n_ch + ch, n_half, stride=2 * n_ch), :]
    tpos = lax.broadcasted_iota(jnp.int32, (t_new, 1), 0) + past
    pos = jnp.concatenate([tpos] * grp, axis=0)
    lane = lax.broadcasted_iota(jnp.int32, (1, n_half), 1)
    ok_e = (2 * lane + 1) * CMP_BLOCK - 1 <= pos
    ok_o = (2 * lane + 2) * CMP_BLOCK - 1 <= pos
    w_k, w_v = wphi_ref[0].astype(BF16), wphi_ref[1].astype(BF16)
    q = q_ref[0]
    scores = []
    for k in range(kvh_n):
        proj = lambda ch, odd, w: jnp.dot(summaries(ch, odd).astype(BF16), w,
                                          preferred_element_type=F32).astype(BF16)
        k_e, k_o = proj(k, 0, w_k), proj(k, 1, w_k)
        v_e, v_o = proj(kvh_n + k, 0, w_v), proj(kvh_n + k, 1, w_v)
        qk = jnp.concatenate(
            [q[:, (k * grp + g) * HEAD_DIM:(k * grp + g + 1) * HEAD_DIM] for g in range(grp)], axis=0).astype(BF16)
        s_e = jnp.where(ok_e, _nt(qk, k_e) * scale, NEG_BIG)
        s_o = jnp.where(ok_o, _nt(qk, k_o) * scale, NEG_BIG)
        m = jnp.maximum(jnp.max(s_e, axis=-1, keepdims=True), jnp.max(s_o, axis=-1, keepdims=True))
        p_e = jnp.where(ok_e, jnp.exp(s_e - m), 0.0)
        p_o = jnp.where(ok_o, jnp.exp(s_o - m), 0.0)
        den = jnp.maximum(jnp.sum(p_e, axis=-1, keepdims=True) + jnp.sum(p_o, axis=-1, keepdims=True), TINY)
        p_e, p_o = p_e / den, p_o / den
        o = (jnp.dot(p_e.astype(BF16), v_e, preferred_element_type=F32)
             + jnp.dot(p_o.astype(BF16), v_o, preferred_element_type=F32))
        imp = jnp.zeros((t_new, n_half), F32)
        for g in range(grp):
            hsl = slice((k * grp + g) * HEAD_DIM, (k * grp + g + 1) * HEAD_DIM)
            ocmp_ref[0, :, hsl] = o[g * t_new:(g + 1) * t_new]
            imp = imp + (p_e[g * t_new:(g + 1) * t_new] + p_o[g * t_new:(g + 1) * t_new])
        tail_lane = lax.broadcasted_iota(jnp.int32, (t_new, n_selp - n_half), 1)
        tail = jnp.where(tail_lane == 0, 0.0, -jnp.inf)
        scores.append(jnp.concatenate([imp, tail], axis=1))
    score = jnp.concatenate(scores, axis=0)
    pos_r = jnp.concatenate([tpos] * kvh_n, axis=0)
    blk = lax.broadcasted_iota(jnp.int32, (1, n_selp), 1)
    q_blk = pos_r // SEL_BLOCK
    forced = (blk == 0) | (blk == q_blk) | (blk == q_blk - 1)
    score = jnp.where(forced, score + FORCE_BONUS, score)
    score = jnp.where((blk * SEL_BLOCK > pos_r) | (blk >= n_sel), -jnp.inf, score)
    rank = _topk_rank(score, blk, n_sel)
    blk_f = blk.astype(F32)
    out_lane = lax.broadcasted_iota(jnp.int32, (kvh_n * t_new, LANES), 1)
    out = jnp.zeros((kvh_n * t_new, LANES), F32)
    for r in range(min(N_SELECT, n_sel)):
        pick = jnp.sum(jnp.where(rank == r, blk_f, 0.0), axis=1, keepdims=True)
        out = jnp.where(out_lane == r, pick, out)
    idx_ref[0] = out.astype(jnp.int32)


def _cmpsel(summ, proj3, wphi, mo, nsa_w, past):
    b, summ_rows, _ = summ.shape
    t_new = proj3.shape[1]
    kvh_n = nsa_w // HEAD_DIM // NSA_GROUP
    n_half = summ_rows // (4 * kvh_n)
    assert n_half == past // SEL_BLOCK
    n_selp = -(-(n_half + 1) // LANES) * LANES
    return pl.pallas_call(
        functools.partial(_cmpsel_kernel, past=past, t_new=t_new, kvh_n=kvh_n, n_selp=n_selp),
        grid=(b,),
        in_specs=[pl.BlockSpec((1, summ_rows, HEAD_DIM), lambda bi: (bi, 0, 0)),
                  pl.BlockSpec((1, t_new, nsa_w), lambda bi: (bi, 0, mo['nsa_q'] // nsa_w)),
                  pl.BlockSpec((2, HEAD_DIM, HEAD_DIM), lambda bi: (0, 0, 0))],
        out_specs=[pl.BlockSpec((1, t_new, nsa_w), lambda bi: (bi, 0, 0)),
                   pl.BlockSpec((1, kvh_n * t_new, LANES), lambda bi: (bi, 0, 0))],
        out_shape=[jax.ShapeDtypeStruct((b, t_new, nsa_w), F32),
                   jax.ShapeDtypeStruct((b, kvh_n * t_new, LANES), jnp.int32)],
        compiler_params=_cparams(("arbitrary",)),
        name="cmp_select",
    )(summ, proj3, wphi)


def _two_part_attention(parts):
    m = None
    for s, msk, _ in parts:
        mx = jnp.max(jnp.where(msk, s, NEG_BIG), axis=-1, keepdims=True)
        m = mx if m is None else jnp.maximum(m, mx)
    den, acc = 0.0, 0.0
    for s, msk, v in parts:
        p = jnp.where(msk, jnp.exp(jnp.where(msk, s, NEG_BIG) - m), 0.0)
        den = den + jnp.sum(p, axis=-1, keepdims=True)
        acc = acc + jnp.dot(p.astype(BF16), v, preferred_element_type=F32)
    return acc / jnp.maximum(den, TINY)


def _selwin_kernel(idx_ref, pt_ref, cache_ref, qr_ref, ksn_ref, vsn_ref, wbuf_ref, kwn_ref, vwn_ref,
                   sm_ref, ocmp_ref, nz_ref, o_ref, kbuf, vbuf, sem,
                   *, layer, past, t_new, kvh_n, n_pages, page, n_heads):
    bi, kvh = pl.program_id(0), pl.program_id(1)
    grp = NSA_GROUP
    n_ch = 2 * kvh_n
    n_top = kbuf.shape[0] // t_new
    n_past_blk = past // SEL_BLOCK
    per_page = page // SEL_BLOCK
    base = (bi * kvh_n + kvh) * t_new * n_top

    def copies(i, row0):
        return (pltpu.make_async_copy(cache_ref.at[layer, pl.ds(row0, SEL_BLOCK), kvh, :], kbuf.at[i], sem.at[0]),
                pltpu.make_async_copy(cache_ref.at[layer, pl.ds(row0, SEL_BLOCK), kvh_n + kvh, :], vbuf.at[i],
                                      sem.at[1]))

    def issue(i, carry):
        idc = jnp.minimum(idx_ref[base + i], n_past_blk - 1)
        phys = pt_ref[bi * n_pages + idc // per_page]
        row0 = pl.multiple_of(phys * page + (idc % per_page) * SEL_BLOCK, SEL_BLOCK)
        for cp in copies(i, row0):
            cp.start()
        return carry

    def wait(i, carry):
        for cp in copies(i, 0):
            cp.wait()
        return carry

    lax.fori_loop(0, t_new * n_top, issue, 0)
    lax.fori_loop(0, t_new * n_top, wait, 0)

    qr = qr_ref[0]
    q4 = jnp.concatenate([qr[:, g * HEAD_DIM:(g + 1) * HEAD_DIM] for g in range(grp)], axis=0)
    rows = grp * t_new
    row_t = jnp.concatenate([lax.broadcasted_iota(jnp.int32, (t_new, 1), 0)] * grp, axis=0)
    pos = row_t + past

    def padded(ref):
        x = ref[0]
        return jnp.concatenate([x, jnp.zeros((SEL_BLOCK - t_new, HEAD_DIM), x.dtype)], axis=0).astype(BF16)

    new_j = lax.broadcasted_iota(jnp.int32, (1, SEL_BLOCK), 1)
    new_pos = past + new_j
    new_real = new_j < t_new

    ksn, vsn = padded(ksn_ref), padded(vsn_ref)
    s_new = _nt(q4, ksn)
    lane = lax.broadcasted_iota(jnp.int32, (1, n_top * SEL_BLOCK), 1)
    o_sel = jnp.zeros((rows, HEAD_DIM), F32)
    for t in range(t_new):
        idx_vec = jnp.zeros((1, n_top * SEL_BLOCK), jnp.int32)
        has_new = jnp.int32(0)
        for n in range(n_top):
            idx = idx_ref[base + t * n_top + n]
            idx_vec = jnp.where(lane // SEL_BLOCK == n, idx, idx_vec)
            has_new = has_new | (idx == n_past_blk).astype(jnp.int32)
        kpos = idx_vec * SEL_BLOCK + lane % SEL_BLOCK
        msk = jnp.where(idx_vec < n_past_blk, kpos, past + t + 1) <= past + t
        k_t = kbuf[t * n_top:(t + 1) * n_top].reshape(n_top * SEL_BLOCK, HEAD_DIM).astype(BF16)
        v_t = vbuf[t * n_top:(t + 1) * n_top].reshape(n_top * SEL_BLOCK, HEAD_DIM).astype(BF16)
        lim_new = jnp.where(has_new > 0, past + t, -1)
        msk_new = jnp.where(new_real, new_pos, past + SEL_BLOCK) <= lim_new
        o_t = _two_part_attention([(_nt(q4, k_t), msk, v_t), (s_new, msk_new, vsn)])
        o_sel = jnp.where(row_t == t, o_t, o_sel)

    n_buf = wbuf_ref.shape[1] // n_ch
    kwb = wbuf_ref[0, pl.ds(kvh, n_buf, stride=n_ch), :].astype(BF16)
    vwb = wbuf_ref[0, pl.ds(kvh_n + kvh, n_buf, stride=n_ch), :].astype(BF16)
    buf_pos = past - n_buf + lax.broadcasted_iota(jnp.int32, (1, n_buf), 1)
    msk_buf = jnp.where(buf_pos <= pos, pos - buf_pos, WINDOW) < WINDOW
    msk_wn = jnp.where(new_real & (new_pos <= pos), pos - new_pos, WINDOW) < WINDOW
    o_win = _two_part_attention([(_nt(q4, kwb), msk_buf, vwb),
                                 (_nt(q4, padded(kwn_ref)), msk_wn, padded(vwn_ref))])

    gates = jax.nn.sigmoid(sm_ref[0])
    ocmp, nz = ocmp_ref[0], nz_ref[0]
    for g in range(grp):
        head = kvh * grp + g
        rs = slice(g * t_new, (g + 1) * t_new)
        sl = slice(g * HEAD_DIM, (g + 1) * HEAD_DIM)
        o = (_lane_pick(gates, head) * ocmp[:, sl]
             + (_lane_pick(gates, n_heads + head) * o_sel[rs] + _lane_pick(gates, 2 * n_heads + head) * o_win[rs]))
        o_ref[0, :, sl] = (o * _silu(nz[:, sl])).astype(o_ref.dtype)


def _selwin(sel_idx_flat, pt_flat, cache4, layer, qr3, ksr3, win_state3, kwr3, small3, ocmp3, proj3, mo, nsa_w,
            past, n_pages, page):
    b, t_new, _ = qr3.shape
    n_heads = nsa_w // HEAD_DIM
    kvh_n = n_heads // NSA_GROUP
    gw = NSA_GROUP * HEAD_DIM
    n_sel = past // SEL_BLOCK + 1
    n_top = min(N_SELECT, n_sel)
    win_rows = win_state3.shape[1]
    new_k = pl.BlockSpec((1, t_new, HEAD_DIM), lambda bi, k, *_: (bi, 0, k))
    new_v = pl.BlockSpec((1, t_new, HEAD_DIM), lambda bi, k, *_: (bi, 0, kvh_n + k))
    return pl.pallas_call(
        functools.partial(_selwin_kernel, layer=layer, past=past, t_new=t_new, kvh_n=kvh_n, n_pages=n_pages,
                          page=page, n_heads=n_heads),
        grid_spec=pltpu.PrefetchScalarGridSpec(
            num_scalar_prefetch=2,
            grid=(b, kvh_n),
            in_specs=[pl.BlockSpec(memory_space=pl.ANY),
                      pl.BlockSpec((1, t_new, gw), lambda bi, k, *_: (bi, 0, k)),
                      new_k, new_v,
                      pl.BlockSpec((1, win_rows, HEAD_DIM), lambda bi, k, *_: (layer * b + bi, 0, 0)),
                      new_k, new_v,
                      pl.BlockSpec((1, t_new, LANES), lambda bi, k, *_: (bi, 0, 0)),
                      pl.BlockSpec((1, t_new, gw), lambda bi, k, *_: (bi, 0, k)),
                      pl.BlockSpec((1, t_new, gw), lambda bi, k, *_: (bi, 0, mo['nsa_z'] // gw + k))],
            out_specs=pl.BlockSpec((1, t_new, gw), lambda bi, k, *_: (bi, 0, k)),
            scratch_shapes=[pltpu.VMEM((t_new * n_top, SEL_BLOCK, HEAD_DIM), F32),
                            pltpu.VMEM((t_new * n_top, SEL_BLOCK, HEAD_DIM), F32),
                            pltpu.SemaphoreType.DMA((2,))]),
        out_shape=jax.ShapeDtypeStruct((b, t_new, nsa_w), BF16),
        compiler_params=_cparams(("arbitrary", "arbitrary")),
        name="sel_win_sample",
    )(sel_idx_flat, pt_flat, cache4, qr3, ksr3, ksr3, win_state3, kwr3, kwr3, small3, ocmp3, proj3)


def _conv_kernel(x_ref, b_ref, c_ref, z_ref, buf_ref, w_ref, o_ref, st_ref, prev):
    tc = x_ref.shape[1]

    @pl.when(pl.program_id(1) == 0)
    def _():
        prev[0:CONV_K - 1, :] = buf_ref[0]

    u = c_ref[0] * x_ref[0]
    row = lax.broadcasted_iota(jnp.int32, (tc, 1), 0)
    w = w_ref[...]
    y = u * w[CONV_K - 1:CONV_K, :]
    for back in range(1, CONV_K):
        shifted = pltpu.roll(u, back, axis=0)
        for r in range(back):
            shifted = jnp.where(row == r, prev[CONV_K - 1 - back + r:CONV_K - back + r, :], shifted)
        y = y + shifted * w[CONV_K - 1 - back:CONV_K - back, :]
    o_ref[0] = (b_ref[0] * y * _silu(z_ref[0])).astype(o_ref.dtype)
    tail = u[tc - (CONV_K - 1):, :]
    prev[0:CONV_K - 1, :] = tail
    st_ref[0] = tail


def _conv(proj3, buf, w, mo, conv_w):
    b, seq, _ = proj3.shape
    tc = _tile(seq, 512)
    assert tc >= CONV_K - 1
    col = lambda name: pl.BlockSpec((1, tc, conv_w), lambda bi, ti: (bi, ti, mo[name] // conv_w))
    return pl.pallas_call(
        _conv_kernel,
        grid=(b, seq // tc),
        in_specs=[col('conv_x'), col('conv_b'), col('conv_c'), col('conv_z'),
                  pl.BlockSpec((1, CONV_K - 1, conv_w), lambda bi, ti: (bi, 0, 0)),
                  pl.BlockSpec((CONV_K, conv_w), lambda bi, ti: (0, 0))],
        out_specs=[pl.BlockSpec((1, tc, conv_w), lambda bi, ti: (bi, ti, 0)),
                   pl.BlockSpec((1, CONV_K - 1, conv_w), lambda bi, ti: (bi, 0, 0))],
        out_shape=[jax.ShapeDtypeStruct((b, seq, conv_w), BF16),
                   jax.ShapeDtypeStruct((b, CONV_K - 1, conv_w), F32)],
        scratch_shapes=[pltpu.VMEM((8, conv_w), F32)],
        compiler_params=_cparams(("arbitrary", "arbitrary")),
        name="short_conv",
    )(proj3, proj3, proj3, proj3, buf, w)


def _log_sigmoid(x):
    return jnp.minimum(x, 0.0) - jnp.log(1.0 + jnp.exp(-jnp.abs(x)))


def _mlstm_kernel(q_ref, k_ref, v_ref, og_ref, z_ref, sm_ref, gr_ref, c0_ref, n0_ref, m0_ref, nw_ref,
                  o_ref, c_out, n_out, m_out, c_s, n_s, m_s, *, n_chunks, gate_i, gate_f):
    ci = pl.program_id(1)

    @pl.when(ci == 0)
    def _():
        c_s[...] = c0_ref[0]
        n_s[...] = n0_ref[0]
        m_s[...] = m0_ref[0]

    small, gr = sm_ref[0], gr_ref[0]
    dk = c_s.shape[1]
    for h in range(MLSTM_HEADS):
        hs = slice(h * dk, (h + 1) * dk)
        c_new, n_new, m_new, out = _mlstm_head(
            q_ref[0, :, hs], k_ref[0, :, hs], v_ref[0, :, hs], og_ref[0, :, hs], z_ref[0, :, hs],
            small[:, gate_i + h:gate_i + h + 1], small[:, gate_f + h:gate_f + h + 1],
            gr[h:h + 1, :], gr[MLSTM_HEADS + h:MLSTM_HEADS + h + 1, :],
            c_s[h], n_s[h], m_s[h], nw_ref[h])
        c_s[h] = c_new
        n_s[h] = n_new
        m_s[h] = m_new
        o_ref[0, :, hs] = out.astype(o_ref.dtype)

    @pl.when(ci == n_chunks - 1)
    def _():
        c_out[0] = c_s[...]
        n_out[0] = n_s[...]
        m_out[0] = m_s[...]


def _mlstm_head(q, k, v, og, z, i_col, f_col, i_row, f_row, c_prev, n_prev, m_prev, norm_w):
    lc, dk = q.shape
    lf_col, lf_row = _log_sigmoid(f_col), _log_sigmoid(f_row)
    ti = lax.broadcasted_iota(jnp.int32, (lc, 1), 0)
    si = lax.broadcasted_iota(jnp.int32, (1, lc), 1)
    causal = si <= ti
    b_col = jnp.sum(jnp.where(causal, lf_row, 0.0), axis=1, keepdims=True)
    b_row = jnp.sum(jnp.where(ti <= si, lf_col, 0.0), axis=0, keepdims=True)
    log_d = jnp.where(causal, b_col - b_row + i_row, -jnp.inf)
    m_t = jnp.maximum(b_col + m_prev, jnp.max(log_d, axis=1, keepdims=True))
    inter = jnp.exp(b_col + m_prev - m_t)
    kf = k * (dk ** -0.5)
    qb, kb, vb = q.astype(BF16), kf.astype(BF16), v.astype(BF16)
    s = _nt(qb, kb) * jnp.exp(log_d - m_t)
    num = (inter * jnp.dot(qb, c_prev.astype(BF16), preferred_element_type=F32)
           + jnp.dot(s.astype(BF16), vb, preferred_element_type=F32))
    den = inter * jnp.sum(q * n_prev, axis=1, keepdims=True) + jnp.sum(s, axis=1, keepdims=True)
    hid = num / jnp.maximum(jnp.abs(den), jnp.exp(-m_t))

    m_new = m_t[lc - 1:lc, :]
    b_last = b_col[lc - 1:lc, :]
    wdec = jnp.exp(b_last - b_col + i_col - m_new)
    cdec = jnp.exp(b_last + m_prev - m_new)
    c_new = cdec * c_prev + lax.dot_general(kb, (wdec * v).astype(BF16), (((0,), (0,)), ((), ())),
                                            preferred_element_type=F32)
    n_new = cdec * n_prev + jnp.sum(wdec * kf, axis=0, keepdims=True)
    hn = hid * lax.rsqrt(jnp.mean(hid * hid, axis=-1, keepdims=True) + RMS_EPS) * norm_w
    return c_new, n_new, m_new, jax.nn.sigmoid(og) * hn * _silu(z)


def _mlstm(proj3, small3, gates_row, c0, n0, m0, norm_w, mo, ml_w, gate_i, gate_f):
    b, seq, _ = proj3.shape
    nh = MLSTM_HEADS
    dk = ml_w // nh
    lc = _tile(seq, 256)
    n_chunks = seq // lc
    col = lambda name: pl.BlockSpec((1, lc, ml_w), lambda bi, ci: (bi, ci, mo[name] // ml_w))
    state = lambda shape: pl.BlockSpec((1, nh) + shape, lambda bi, ci: (bi, 0, 0, 0))
    return pl.pallas_call(
        functools.partial(_mlstm_kernel, n_chunks=n_chunks, gate_i=gate_i, gate_f=gate_f),
        grid=(b, n_chunks),
        in_specs=[col('m_q'), col('m_k'), col('m_v'), col('m_o'), col('m_z'),
                  pl.BlockSpec((1, lc, LANES), lambda bi, ci: (bi, ci, 0)),
                  pl.BlockSpec((1, 2 * nh, lc), lambda bi, ci: (bi, 0, ci)),
                  state((dk, dk)), state((1, dk)), state((1, 1)),
                  pl.BlockSpec((nh, 1, dk), lambda bi, ci: (0, 0, 0))],
        out_specs=[pl.BlockSpec((1, lc, ml_w), lambda bi, ci: (bi, ci, 0)),
                   state((dk, dk)), state((1, dk)), state((1, 1))],
        out_shape=[jax.ShapeDtypeStruct((b, seq, ml_w), BF16),
                   jax.ShapeDtypeStruct((b, nh, dk, dk), F32),
                   jax.ShapeDtypeStruct((b, nh, 1, dk), F32),
                   jax.ShapeDtypeStruct((b, nh, 1, 1), F32)],
        scratch_shapes=[pltpu.VMEM((nh, dk, dk), F32), pltpu.VMEM((nh, 1, dk), F32), pltpu.VMEM((nh, 1, 1), F32)],
        compiler_params=_cparams(("arbitrary", "arbitrary")),
        name="mlstm",
    )(proj3, proj3, proj3, proj3, proj3, small3, gates_row, c0, n0, m0, norm_w)


def _outproj_kernel(x_ref, g_ref, a_ref, c_ref, m_ref, w_ref, o_ref):
    ka, kc = a_ref.shape[1], c_ref.shape[1]
    acc = jnp.dot(a_ref[...], w_ref[0:ka, :], preferred_element_type=F32)
    acc = acc + jnp.dot(c_ref[...], w_ref[ka:ka + kc, :], preferred_element_type=F32)
    acc = acc + jnp.dot(m_ref[...], w_ref[ka + kc:, :], preferred_element_type=F32)
    o_ref[...] = x_ref[...] + g_ref[0] * acc


def _outproj(x2, gate3, o_nsa, o_conv, o_ml, w_out, tm, rows_per_mod):
    m, d = x2.shape
    tn = _tile(d, 1024)
    r = gate3.shape[1]
    kdim = w_out.shape[0]
    lhs = lambda a: pl.BlockSpec((tm, a.shape[1]), lambda i, j: (i, 0))
    return pl.pallas_call(
        _outproj_kernel,
        grid=(m // tm, d // tn),
        in_specs=[pl.BlockSpec((tm, tn), lambda i, j: (i, j)),
                  pl.BlockSpec((1, r, tn), lambda i, j: (i // rows_per_mod, 0, j)),
                  lhs(o_nsa), lhs(o_conv), lhs(o_ml),
                  pl.BlockSpec((kdim, tn), lambda i, j: (0, j))],
        out_specs=pl.BlockSpec((tm, tn), lambda i, j: (i, j)),
        out_shape=jax.ShapeDtypeStruct((m, d), F32),
        compiler_params=_cparams(("arbitrary", "arbitrary")),
        name="outproj",
    )(x2, gate3, o_nsa, o_conv, o_ml, w_out)


def _rmsnorm_kernel(x_ref, w_ref, o_ref):
    x = x_ref[...]
    o_ref[...] = x * lax.rsqrt(jnp.mean(x * x, axis=-1, keepdims=True) + RMS_EPS) * w_ref[...]


def _rmsnorm(x2, w):
    m, d = x2.shape
    tm = _tile(m, 256)
    return pl.pallas_call(
        _rmsnorm_kernel,
        grid=(m // tm,),
        in_specs=[pl.BlockSpec((tm, d), lambda i: (i, 0)), pl.BlockSpec((1, d), lambda i: (0, 0))],
        out_specs=pl.BlockSpec((tm, d), lambda i: (i, 0)),
        out_shape=jax.ShapeDtypeStruct((m, d), F32),
        compiler_params=_cparams(("arbitrary",)),
        name="final_norm",
    )(x2, w.reshape(1, d))


def _layer(x3, mod, lw, cfg, past, states):
    b, seq, d = x3.shape
    m = b * seq
    mo, nsa_w, kv_w, conv_w, ml_w = cfg['mo'], cfg['nsa_w'], cfg['kv_w'], cfg['conv_w'], cfg['ml_w']
    shift, scale, gate = mod
    x2 = x3.reshape(m, d)
    if seq >= 512:
        tm, rows_per_mod = 512, seq // 512
        per = lambda a: a.reshape(b, 1, d)
    else:
        tm, rows_per_mod = m, 1
        per = lambda a: jnp.repeat(a, seq, axis=0).reshape(1, m, d)
    h2 = _modnorm(x2, per(scale), per(shift), lw['norm_w'], tm, rows_per_mod)
    proj, small = _inproj(h2, lw['w_main'], lw['b_main'], lw['w_small'], lw['b_small'], lw['layer'])
    nm = proj.shape[1]
    proj3, small3 = proj.reshape(b, seq, nm), small.reshape(b, seq, LANES)

    pos = past + jnp.arange(seq, dtype=jnp.int32)
    cos, sin = _rope_tables(pos)
    qr, ksr, kwr = _rope(proj, jnp.tile(cos, (b, 1)), jnp.tile(sin, (b, 1)), mo, nsa_w, kv_w)
    qr3, ksr3, kwr3 = qr.reshape(b, seq, nsa_w), ksr.reshape(b, seq, kv_w), kwr.reshape(b, seq, kv_w)

    if states is None:
        o_nsa = _nsa_prompt(proj3, small3, qr3, ksr3, kwr3, lw['cmp_pe'], lw['cmp_w'], mo, nsa_w)
        conv_buf = jnp.zeros((b, CONV_K - 1, conv_w), F32)
        dk = ml_w // MLSTM_HEADS
        c0 = jnp.zeros((b, MLSTM_HEADS, dk, dk), F32)
        n0 = jnp.zeros((b, MLSTM_HEADS, 1, dk), F32)
        m0 = jnp.zeros((b, MLSTM_HEADS, 1, 1), F32)
    else:
        n_pages, page = states['n_pages'], states['page']
        summ = _summaries(states['cache_cmp'], states['layer'], states['pt_flat'], lw['pe_ch'], b, n_pages,
                          kv_w // HEAD_DIM)
        o_cmp, sel_idx = _cmpsel(summ, proj3, lw['cmp_w'], mo, nsa_w, past)
        n_top = min(N_SELECT, past // SEL_BLOCK + 1)
        sel_flat = sel_idx[:, :, :n_top].reshape(-1)
        o_nsa = _selwin(sel_flat, states['pt_flat'], states['cache_sel'], states['layer'], qr3, ksr3,
                        states['win'], kwr3, small3, o_cmp, proj3, mo, nsa_w, past, n_pages, page)
        conv_buf = states['conv']
        dk = ml_w // MLSTM_HEADS
        c0 = states['C']
        n0 = states['n'].reshape(b, MLSTM_HEADS, 1, dk)
        m0 = states['m'].reshape(b, MLSTM_HEADS, 1, 1)

    o_conv, new_conv = _conv(proj3, conv_buf, lw['conv_w'], mo, conv_w)
    gates_row = jnp.swapaxes(small3[:, :, cfg['gate_i']:cfg['gate_i'] + 2 * MLSTM_HEADS], 1, 2)
    o_ml, c1, n1, m1 = _mlstm(proj3, small3, gates_row, c0, n0, m0, lw['mlstm_norm_w'], mo, ml_w,
                              cfg['gate_i'], cfg['gate_f'])

    y = _outproj(x2, per(gate), o_nsa.reshape(m, nsa_w), o_conv.reshape(m, conv_w), o_ml.reshape(m, ml_w),
                 lw['w_out'], tm, rows_per_mod)
    kv_shape = (b, seq, 2, kv_w // (2 * HEAD_DIM), HEAD_DIM)
    new_cmp = proj3[:, :, mo['kv_cmp']:mo['kv_cmp'] + kv_w].reshape(kv_shape)
    dk = ml_w // MLSTM_HEADS
    st = (new_cmp, ksr3.reshape(kv_shape), kwr3.reshape(kv_shape), new_conv, c1,
          n1.reshape(b, MLSTM_HEADS, dk), m1.reshape(b, MLSTM_HEADS))
    return y.reshape(b, seq, d), st


def kernel(x_prompt, x_sample, cache_cmp_kv, cache_sel_kv, state_win_kv, state_conv, state_mlstm_C, state_mlstm_n,
           state_mlstm_m, page_table, c_prompt, c_sample, norm_w, w_ada, b_ada, w_in, b_in, cmp_pe, cmp_w, conv_w,
           mlstm_norm_w, w_out, final_norm_w):
    bp, seq, d = x_prompt.shape
    bs, t_new, _ = x_sample.shape
    depth = w_in.shape[0]
    layout = _in_layout(d)
    sizes = dict(layout)
    in_off, _ = _offsets([n for n, _ in layout], sizes)
    mo, n_main = _offsets(MAIN_NAMES, sizes)
    so, n_small = _offsets(SMALL_NAMES, sizes)
    assert n_small <= LANES
    nsa_w, kv_w, conv_wd, ml_w = sizes['nsa_q'], sizes['kv_cmp'], sizes['conv_x'], sizes['m_q']
    cfg = dict(mo=mo, nsa_w=nsa_w, kv_w=kv_w, conv_w=conv_wd, ml_w=ml_w, gate_i=so['m_i'], gate_f=so['m_f'])
    n_kv = kv_w // (2 * HEAD_DIM)

    def gather_cols(a, names, pad_to):
        parts = [a[..., in_off[n]:in_off[n] + sizes[n]] for n in names]
        width = sum(sizes[n] for n in names)
        if pad_to > width:
            parts.append(jnp.zeros(a.shape[:-1] + (pad_to - width,), a.dtype))
        return jnp.concatenate(parts, axis=-1)

    segs_main = tuple((in_off[n], sizes[n], mo[n]) for n in MAIN_NAMES)
    segs_small = tuple((in_off[n], sizes[n], so[n]) for n in SMALL_NAMES)
    w_main_all, w_small_all = _wprep(w_in, segs_main, segs_small, n_main)

    layers = []
    for l in range(depth):
        pe_ch = jnp.repeat(jnp.transpose(cmp_pe[l], (1, 0, 2)), n_kv, axis=1)
        layers.append(dict(
            layer=l, norm_w=norm_w[l].reshape(1, d),
            w_main=w_main_all,
            b_main=gather_cols(b_in[l], MAIN_NAMES, n_main).reshape(1, n_main),
            w_small=w_small_all,
            b_small=gather_cols(b_in[l], SMALL_NAMES, LANES).reshape(1, LANES),
            cmp_pe=cmp_pe[l], cmp_w=cmp_w[l], pe_ch=pe_ch, conv_w=conv_w[l],
            mlstm_norm_w=mlstm_norm_w[l].reshape(MLSTM_HEADS, 1, ml_w // MLSTM_HEADS),
            w_out=w_out[l].astype(BF16)))

    n_rows = bp + bs
    pad_rows = -(-n_rows // 8) * 8
    c_all = jnp.concatenate([c_prompt, c_sample, jnp.zeros((pad_rows - n_rows, d), F32)], axis=0)
    mod = _ada(c_all, w_ada, b_ada)

    def mods(l, lo, hi):
        return tuple(mod[l, lo:hi, i * d:(i + 1) * d] for i in range(3))

    xp = x_prompt
    ps = []
    for l in range(depth):
        xp, st = _layer(xp, mods(l, 0, bp), layers[l], cfg, 0, None)
        ps.append(st)

    n_phys, page = cache_cmp_kv.shape[1], cache_cmp_kv.shape[2]
    n_pages = page_table.shape[1]
    past = n_pages * page
    assert past % SEL_BLOCK == 0 and t_new <= CMP_BLOCK and state_win_kv.shape[2] <= WINDOW
    pt_flat = page_table.reshape(-1).astype(jnp.int32)
    n_ch = 2 * n_kv
    cache_cmp4 = cache_cmp_kv.reshape(depth, n_phys, page * n_ch, HEAD_DIM)
    cache_sel3 = cache_sel_kv.reshape(depth, n_phys * page, n_ch, HEAD_DIM)
    win4 = state_win_kv.reshape(depth * bs, state_win_kv.shape[2] * n_ch, HEAD_DIM)
    xs = x_sample
    ss = []
    for l in range(depth):
        states = dict(layer=l, n_pages=n_pages, page=page, pt_flat=pt_flat, cache_cmp=cache_cmp4,
                      cache_sel=cache_sel3, win=win4, conv=state_conv[l], C=state_mlstm_C[l], n=state_mlstm_n[l],
                      m=state_mlstm_m[l])
        xs, st = _layer(xs, mods(l, bp, bp + bs), layers[l], cfg, past, states)
        ss.append(st)

    y_prompt = _rmsnorm(xp.reshape(bp * seq, d), final_norm_w).reshape(bp, seq, d)
    y_sample = _rmsnorm(xs.reshape(bs * t_new, d), final_norm_w).reshape(bs, t_new, d)
    n_keep = min(WINDOW, seq)
    stack = lambda sts, i: jnp.stack([s[i] for s in sts])
    p_win = jnp.stack([s[2][:, seq - n_keep:] for s in ps])
    return (y_prompt, y_sample, stack(ps, 0), stack(ps, 1), p_win, stack(ps, 3), stack(ps, 4), stack(ps, 5),
            stack(ps, 6), stack(ss, 0), stack(ss, 1), stack(ss, 2), stack(ss, 3), stack(ss, 4), stack(ss, 5),
            stack(ss, 6))
```

```python
import functools

import jax
import jax.numpy as jnp
from jax import lax
from jax.experimental import pallas as pl
from jax.experimental.pallas import tpu as pltpu

HEAD_DIM = 128
NSA_GROUP = 4
CMP_BLOCK = 32
SEL_BLOCK = 64
N_SELECT = 16
WINDOW = 512
FORCE_BONUS = 1.0e4
CONV_K = 3
MLSTM_HEADS = 4
ROPE_THETA = 10000.0
RMS_EPS = 1e-6
NEG_BIG = -1e30
TINY = 1e-30

LANES = 128
VMEM_LIMIT = 56 * 1024 * 1024

F32 = jnp.float32
BF16 = jnp.bfloat16

MAIN_NAMES = ('nsa_q', 'kv_cmp', 'kv_sel', 'kv_win', 'nsa_z', 'conv_x', 'conv_b', 'conv_c', 'conv_z',
              'm_q', 'm_k', 'm_v', 'm_o', 'm_z')
SMALL_NAMES = ('nsa_gate', 'm_i', 'm_f')


def _in_layout(d_model):
    nsa_w, kv_w, conv_w, ml_w = d_model // 2, d_model // 4, d_model // 4, d_model // 4
    nsa_heads = nsa_w // HEAD_DIM
    return (('nsa_q', nsa_w), ('kv_cmp', kv_w), ('kv_sel', kv_w), ('kv_win', kv_w),
            ('nsa_gate', 3 * nsa_heads), ('nsa_z', nsa_w),
            ('conv_x', conv_w), ('conv_b', conv_w), ('conv_c', conv_w), ('conv_z', conv_w),
            ('m_q', ml_w), ('m_k', ml_w), ('m_v', ml_w),
            ('m_i', MLSTM_HEADS), ('m_f', MLSTM_HEADS), ('m_o', ml_w), ('m_z', ml_w))


def _offsets(names, sizes):
    off, out = 0, {}
    for n in names:
        out[n] = off
        off += sizes[n]
    return out, off


def _cparams(sem):
    return pltpu.CompilerParams(dimension_semantics=sem, vmem_limit_bytes=VMEM_LIMIT)


def _tile(dim, pref):
    t = min(dim, pref)
    assert dim % t == 0, (dim, pref)
    return t


def _nt(a, b):
    return lax.dot_general(a, b, (((1,), (1,)), ((), ())), preferred_element_type=F32)


def _silu(x):
    return x * jax.nn.sigmoid(x)


def _lane_pick(x, col):
    lane = lax.broadcasted_iota(jnp.int32, x.shape, 1)
    return jnp.sum(jnp.where(lane == col, x, 0.0), axis=1, keepdims=True)


def _ada_kernel(c_ref, w_ref, b_ref, o_ref):
    a = _silu(c_ref[...]).astype(BF16)
    o_ref[0] = jnp.dot(a, w_ref[0].astype(BF16), preferred_element_type=F32) + b_ref[0]


def _ada(c_all, w_ada, b_ada):
    rows, d = c_all.shape
    depth, _, n = w_ada.shape
    tn = _tile(n, 512)
    return pl.pallas_call(
        _ada_kernel,
        grid=(depth, n // tn),
        in_specs=[pl.BlockSpec((rows, d), lambda l, j: (0, 0)),
                  pl.BlockSpec((1, d, tn), lambda l, j: (l, 0, j)),
                  pl.BlockSpec((1, 1, tn), lambda l, j: (l, 0, j))],
        out_specs=pl.BlockSpec((1, rows, tn), lambda l, j: (l, 0, j)),
        out_shape=jax.ShapeDtypeStruct((depth, rows, n), F32),
        compiler_params=_cparams(("arbitrary", "arbitrary")),
        name="ada",
    )(c_all, w_ada, b_ada.reshape(depth, 1, n))


SUBLANES = 8


def _wprep_kernel(tab_ref, wt_ref, wm_ref):
    del tab_ref
    wm_ref[0] = wt_ref[0].T.astype(wm_ref.dtype)


def _wprep(w_in, segs_main, n_main):
    depth, d, _ = w_in.shape
    tn = 512
    src_rows = []
    for src, size, dst in segs_main:
        assert size % tn == 0 and dst % tn == 0 and src % SUBLANES == 0
        src_rows += [(src + i) // SUBLANES for i in range(0, size, tn)]
    table = jnp.asarray(src_rows, jnp.int32)
    elem = lambda n: pl.Element(n)
    return pl.pallas_call(
        _wprep_kernel,
        grid_spec=pltpu.PrefetchScalarGridSpec(
            num_scalar_prefetch=1,
            grid=(depth, n_main // tn),
            in_specs=[pl.BlockSpec((elem(1), elem(tn), elem(d)), lambda l, j, tab: (l, tab[j] * SUBLANES, 0))],
            out_specs=pl.BlockSpec((1, d, tn), lambda l, j, tab: (l, 0, j))),
        out_shape=jax.ShapeDtypeStruct((depth, d, n_main), BF16),
        compiler_params=_cparams(("arbitrary", "arbitrary")),
        name="weight_prep",
    )(table, jnp.swapaxes(w_in, 1, 2))


def _modnorm_kernel(x_ref, sc_ref, sh_ref, nw_ref, h_ref):
    x = x_ref[...]
    y = x * lax.rsqrt(jnp.mean(x * x, axis=-1, keepdims=True) + RMS_EPS) * nw_ref[...]
    h_ref[...] = (y * (1.0 + sc_ref[0]) + sh_ref[0]).astype(h_ref.dtype)


def _modnorm(x2, scale3, shift3, nw, tm, rows_per_mod):
    m, d = x2.shape
    r = scale3.shape[1]
    mod_spec = pl.BlockSpec((1, r, d), lambda i: (i // rows_per_mod, 0, 0))
    return pl.pallas_call(
        _modnorm_kernel,
        grid=(m // tm,),
        in_specs=[pl.BlockSpec((tm, d), lambda i: (i, 0)), mod_spec, mod_spec,
                  pl.BlockSpec((1, d), lambda i: (0, 0))],
        out_specs=pl.BlockSpec((tm, d), lambda i: (i, 0)),
        out_shape=jax.ShapeDtypeStruct((m, d), BF16),
        compiler_params=_cparams(("arbitrary",)),
        name="modnorm",
    )(x2, scale3, shift3, nw)


def _inproj_kernel(h_ref, wm_ref, bm_ref, ws_ref, bs_ref, o_ref, os_ref):
    @pl.when(pl.program_id(1) == 0)
    def _():
        os_ref[...] = jnp.dot(h_ref[...], ws_ref[0], preferred_element_type=F32) + bs_ref[...]

    o_ref[...] = jnp.dot(h_ref[...], wm_ref[0], preferred_element_type=F32) + bm_ref[...]


def _inproj(h2, w_main, b_main, w_small, b_small, layer):
    m, d = h2.shape
    nm = w_main.shape[2]
    tm, tn = _tile(m, 1024), _tile(nm, 1024)
    return pl.pallas_call(
        _inproj_kernel,
        grid=(m // tm, nm // tn),
        in_specs=[pl.BlockSpec((tm, d), lambda i, j: (i, 0)),
                  pl.BlockSpec((1, d, tn), lambda i, j: (layer, 0, j)),
                  pl.BlockSpec((1, tn), lambda i, j: (0, j)),
                  pl.BlockSpec((1, d, LANES), lambda i, j: (layer, 0, 0)),
                  pl.BlockSpec((1, LANES), lambda i, j: (0, 0))],
        out_specs=[pl.BlockSpec((tm, tn), lambda i, j: (i, j)),
                   pl.BlockSpec((tm, LANES), lambda i, j: (i, 0))],
        out_shape=[jax.ShapeDtypeStruct((m, nm), F32), jax.ShapeDtypeStruct((m, LANES), F32)],
        compiler_params=_cparams(("arbitrary", "arbitrary")),
        name="inproj",
    )(h2, w_main, b_main, w_small, b_small)


def _rope_kernel(q_ref, ks_ref, kw_ref, cos_ref, sin_ref, qr_ref, ksr_ref, kwr_ref, *, n_q, n_kv):
    cos, sin = cos_ref[...], sin_ref[...]

    def rot(x):
        return x * cos + pltpu.roll(x, HEAD_DIM // 2, axis=1) * sin

    for h in range(n_q):
        sl = slice(h * HEAD_DIM, (h + 1) * HEAD_DIM)
        qr_ref[:, sl] = (rot(q_ref[:, sl]) * HEAD_DIM ** -0.5).astype(qr_ref.dtype)
    for src, dst in ((ks_ref, ksr_ref), (kw_ref, kwr_ref)):
        for h in range(n_kv):
            sl = slice(h * HEAD_DIM, (h + 1) * HEAD_DIM)
            dst[:, sl] = rot(src[:, sl])
        dst[:, n_kv * HEAD_DIM:] = src[:, n_kv * HEAD_DIM:]


def _rope(proj, cos, sin, mo, nsa_w, kv_w):
    m = proj.shape[0]
    tr = _tile(m, 256)
    n_q, n_kv = nsa_w // HEAD_DIM, kv_w // (2 * HEAD_DIM)
    return pl.pallas_call(
        functools.partial(_rope_kernel, n_q=n_q, n_kv=n_kv),
        grid=(m // tr,),
        in_specs=[pl.BlockSpec((tr, nsa_w), lambda i: (i, mo['nsa_q'] // nsa_w)),
                  pl.BlockSpec((tr, kv_w), lambda i: (i, mo['kv_sel'] // kv_w)),
                  pl.BlockSpec((tr, kv_w), lambda i: (i, mo['kv_win'] // kv_w)),
                  pl.BlockSpec((tr, HEAD_DIM), lambda i: (i, 0)),
                  pl.BlockSpec((tr, HEAD_DIM), lambda i: (i, 0))],
        out_specs=[pl.BlockSpec((tr, nsa_w), lambda i: (i, 0)),
                   pl.BlockSpec((tr, kv_w), lambda i: (i, 0)),
                   pl.BlockSpec((tr, kv_w), lambda i: (i, 0))],
        out_shape=[jax.ShapeDtypeStruct((m, nsa_w), BF16),
                   jax.ShapeDtypeStruct((m, kv_w), F32),
                   jax.ShapeDtypeStruct((m, kv_w), F32)],
        compiler_params=_cparams(("arbitrary",)),
        name="rope",
    )(proj, proj, proj, cos, sin)


def _rope_tables(pos):
    half = HEAD_DIM // 2
    inv = ROPE_THETA ** (-jnp.arange(half, dtype=F32) / half)
    ang = pos.astype(F32)[:, None] * inv[None, :]
    cos, sin = jnp.cos(ang), jnp.sin(ang)
    return jnp.concatenate([cos, cos], axis=1), jnp.concatenate([-sin, sin], axis=1)


def _flash_step(carry, s, bias, v_bf):
    m, l, acc = carry
    if bias is not None:
        s = s + bias[None]
    m_new = jnp.maximum(m, jnp.max(s, axis=-1, keepdims=True))
    p = jnp.exp(s - m_new)
    alpha = jnp.exp(m - m_new)
    l = alpha * l + jnp.sum(p, axis=-1, keepdims=True)
    pv = jnp.dot(p.reshape(-1, p.shape[-1]).astype(BF16), v_bf, preferred_element_type=F32)
    acc = alpha * acc + pv.reshape(acc.shape)
    return m_new, l, acc


def _flash_init(shape_rows):
    return (jnp.full(shape_rows + (1,), NEG_BIG, F32), jnp.zeros(shape_rows + (1,), F32),
            jnp.zeros(shape_rows + (HEAD_DIM,), F32))


def _topk_rank(score, blk, n_blocks, axis=1):
    rank = jnp.zeros(score.shape, F32)
    for i in range(n_blocks):
        si = score[:, i:i + 1] if axis == 1 else score[i:i + 1, :]
        ge = jnp.where(si >= score, 1.0, 0.0)
        gt = jnp.where(si > score, 1.0, 0.0)
        rank = rank + jnp.where(blk > i, ge, gt)
    return rank


def _nsa_prompt_kernel(qc_ref, qr_ref, kc_ref, vc_ref, ks_ref, vs_ref, kw_ref, vw_ref, nz_ref, sm_ref,
                       pe_ref, wphi_ref, o_ref, kcs, vcs, *, seq, tq, tk, n_heads):
    kvh, qi = pl.program_id(1), pl.program_id(2)
    n_sel = seq // SEL_BLOCK
    n_cmp = 2 * n_sel
    grp = NSA_GROUP
    scale = HEAD_DIM ** -0.5

    @pl.when(qi == 0)
    def _():
        for c, (src, dst) in enumerate(((kc_ref, kcs), (vc_ref, vcs))):
            x = src[0].reshape(n_sel, 2 * CMP_BLOCK, HEAD_DIM)
            pe = pe_ref[c][None]
            ev = jnp.mean(x[:, :CMP_BLOCK, :] + pe, axis=1)
            od = jnp.mean(x[:, CMP_BLOCK:, :] + pe, axis=1)
            summ = jnp.concatenate([ev, od], axis=0).astype(BF16)
            dst[...] = jnp.dot(summ, wphi_ref[c].astype(BF16), preferred_element_type=F32)

    t0 = qi * tq
    pos = t0 + lax.broadcasted_iota(jnp.int32, (tq, 1), 0)
    pos_l = t0 + lax.broadcasted_iota(jnp.int32, (1, tq), 1)

    row_c = lax.broadcasted_iota(jnp.int32, (n_cmp, 1), 0)
    cblk = jnp.where(row_c < n_sel, 2 * row_c, 2 * (row_c - n_sel) + 1)
    cmp_ok = (cblk + 1) * CMP_BLOCK - 1 <= pos_l
    kcb, vcb = kcs[...].astype(BF16), vcs[...].astype(BF16)
    qc = qc_ref[0]
    imp = jnp.zeros((n_cmp, tq), F32)
    o_cmp = []
    for g in range(grp):
        s = _nt(kcb, qc[:, g * HEAD_DIM:(g + 1) * HEAD_DIM].astype(BF16)) * scale
        s = jnp.where(cmp_ok, s, NEG_BIG)
        s = s - jnp.max(s, axis=0, keepdims=True)
        p = jnp.where(cmp_ok, jnp.exp(s), 0.0)
        p = p / jnp.maximum(jnp.sum(p, axis=0, keepdims=True), TINY)
        imp = imp + p
        o_cmp.append(lax.dot_general(p.astype(BF16), vcb, (((0,), (0,)), ((), ())), preferred_element_type=F32))

    imp_sel = imp[:n_sel] + imp[n_sel:]
    blk = lax.broadcasted_iota(jnp.int32, (n_sel, 1), 0)
    q_blk = pos_l // SEL_BLOCK
    forced = (blk == 0) | (blk == q_blk) | (blk == q_blk - 1)
    score = jnp.where(forced, imp_sel + FORCE_BONUS, imp_sel)
    score = jnp.where(blk * SEL_BLOCK > pos_l, -jnp.inf, score)
    rank = _topk_rank(score, blk, n_sel, axis=0)
    sel_neg = jnp.concatenate([jnp.where(rank < N_SELECT, 0.0, NEG_BIG), jnp.zeros((LANES - n_sel, tq), F32)], axis=0)
    sel_neg = sel_neg.T.astype(BF16)

    qr = qr_ref[0]
    q4 = jnp.concatenate([qr[:, g * HEAD_DIM:(g + 1) * HEAD_DIM] for g in range(grp)], axis=0)
    q4_sel = jnp.concatenate([q4, jnp.concatenate([sel_neg] * grp, axis=0)], axis=1)

    def sel_keys(k0):
        kch = ks_ref[0, pl.ds(k0, tk), :].astype(BF16)
        kblk = (k0 + lax.broadcasted_iota(jnp.int32, (tk, 1), 0)) // SEL_BLOCK
        one_hot = jnp.where(kblk == lax.broadcasted_iota(jnp.int32, (1, LANES), 1), 1.0, 0.0).astype(BF16)
        return jnp.concatenate([kch, one_hot], axis=1), vs_ref[0, pl.ds(k0, tk), :].astype(BF16)

    def sel_step(k0, carry, bias):
        k_aug, vch = sel_keys(k0)
        return _flash_step(carry, _nt(q4_sel, k_aug).reshape(grp, tq, tk), bias, vch)

    n_chunks = (t0 + tq + tk - 1) // tk
    carry = lax.fori_loop(0, n_chunks - 1, lambda c, cr: sel_step(pl.multiple_of(c * tk, tk), cr, None),
                          _flash_init((grp, tq)))
    k_last = pl.multiple_of((n_chunks - 1) * tk, tk)
    kpos = k_last + lax.broadcasted_iota(jnp.int32, (1, tk), 1)
    _, l_s, acc_s = sel_step(k_last, carry, jnp.where(kpos <= pos, 0.0, NEG_BIG))

    def win_body(i, carry):
        k0 = pl.multiple_of((qi - i) * tq, tq)
        kpos = k0 + lax.broadcasted_iota(jnp.int32, (1, tq), 1)
        bias = jnp.where(jnp.where(kpos <= pos, pos - kpos, WINDOW) < WINDOW, 0.0, NEG_BIG)
        s = _nt(q4, kw_ref[0, pl.ds(k0, tq), :].astype(BF16)).reshape(grp, tq, tq)
        return _flash_step(carry, s, bias, vw_ref[0, pl.ds(k0, tq), :].astype(BF16))

    n_back = jnp.minimum(qi, (WINDOW + tq - 1) // tq)
    _, l_w, acc_w = lax.fori_loop(0, n_back + 1, win_body, _flash_init((grp, tq)))

    gates = jax.nn.sigmoid(sm_ref[0])
    nz = nz_ref[0]
    for g in range(grp):
        head = kvh * grp + g
        w_cmp = _lane_pick(gates, head)
        w_sel = _lane_pick(gates, n_heads + head) / jnp.maximum(l_s[g], TINY)
        w_win = _lane_pick(gates, 2 * n_heads + head) / jnp.maximum(l_w[g], TINY)
        o = w_cmp * o_cmp[g] + (w_sel * acc_s[g] + w_win * acc_w[g])
        sl = slice(g * HEAD_DIM, (g + 1) * HEAD_DIM)
        o_ref[0, :, sl] = (o * _silu(nz[:, sl])).astype(o_ref.dtype)


def _nsa_prompt(proj3, small3, qr3, ksr3, kwr3, pe, wphi, mo, nsa_w):
    b, seq, _ = proj3.shape
    n_heads = nsa_w // HEAD_DIM
    kvh = n_heads // NSA_GROUP
    gw = NSA_GROUP * HEAD_DIM
    tq = _tile(seq, 256)
    tk = _tile(seq, 512)
    assert seq % (2 * CMP_BLOCK) == 0 and tk % SEL_BLOCK == 0
    assert N_SELECT >= 3, "the forced blocks (first, current, previous) must always fit in the selection"
    assert seq // SEL_BLOCK <= LANES
    hb = lambda name: mo[name] // HEAD_DIM

    def head_spec(off_blocks):
        return pl.BlockSpec((1, seq, HEAD_DIM), lambda bi, k, qi: (bi, 0, off_blocks + k))

    return pl.pallas_call(
        functools.partial(_nsa_prompt_kernel, seq=seq, tq=tq, tk=tk, n_heads=n_heads),
        grid=(b, kvh, seq // tq),
        in_specs=[pl.BlockSpec((1, tq, gw), lambda bi, k, qi: (bi, qi, mo['nsa_q'] // gw + k)),
                  pl.BlockSpec((1, tq, gw), lambda bi, k, qi: (bi, qi, k)),
                  head_spec(hb('kv_cmp')), head_spec(hb('kv_cmp') + kvh),
                  head_spec(0), head_spec(kvh), head_spec(0), head_spec(kvh),
                  pl.BlockSpec((1, tq, gw), lambda bi, k, qi: (bi, qi, mo['nsa_z'] // gw + k)),
                  pl.BlockSpec((1, tq, LANES), lambda bi, k, qi: (bi, qi, 0)),
                  pl.BlockSpec((2, CMP_BLOCK, HEAD_DIM), lambda bi, k, qi: (0, 0, 0)),
                  pl.BlockSpec((2, HEAD_DIM, HEAD_DIM), lambda bi, k, qi: (0, 0, 0))],
        out_specs=pl.BlockSpec((1, tq, gw), lambda bi, k, qi: (bi, qi, k)),
        out_shape=jax.ShapeDtypeStruct((b, seq, nsa_w), BF16),
        scratch_shapes=[pltpu.VMEM((seq // CMP_BLOCK, HEAD_DIM), F32),
                        pltpu.VMEM((seq // CMP_BLOCK, HEAD_DIM), F32)],
        compiler_params=_cparams(("arbitrary", "arbitrary", "arbitrary")),
        name="nsa_prompt",
    )(proj3, qr3, proj3, proj3, ksr3, ksr3, kwr3, kwr3, proj3, small3, pe, wphi)


def _summ_kernel(pt_ref, *refs, n_pg, n_ch):
    del pt_ref
    x_refs, pe_ref, o_ref = refs[:n_pg], refs[n_pg], refs[n_pg + 1]
    pe = pe_ref[...][None]
    outs = []
    for r in x_refs:
        page = r[0, 0]
        n_blk = page.shape[0] // (CMP_BLOCK * n_ch)
        x = page.reshape(n_blk, CMP_BLOCK, n_ch, HEAD_DIM)
        outs.append(jnp.mean(x + pe, axis=1).reshape(n_blk * n_ch, HEAD_DIM))
    o_ref[0] = jnp.concatenate(outs, axis=0)


def _summaries(cache4, layer, pt_flat, pe_ch, n_batch, n_pages, n_ch):
    page_rows = cache4.shape[2]
    blk_per_page = page_rows // (CMP_BLOCK * n_ch)
    n_pg = _tile(n_pages, 8)
    rows = n_pg * blk_per_page * n_ch

    def page_spec(k):
        return pl.BlockSpec((1, 1, page_rows, HEAD_DIM),
                            lambda bi, j, pt: (layer, pt[bi * n_pages + j * n_pg + k], 0, 0))

    return pl.pallas_call(
        functools.partial(_summ_kernel, n_pg=n_pg, n_ch=n_ch),
        grid_spec=pltpu.PrefetchScalarGridSpec(
            num_scalar_prefetch=1,
            grid=(n_batch, n_pages // n_pg),
            in_specs=[page_spec(k) for k in range(n_pg)]
            + [pl.BlockSpec((CMP_BLOCK, n_ch, HEAD_DIM), lambda bi, j, pt: (0, 0, 0))],
            out_specs=pl.BlockSpec((1, rows, HEAD_DIM), lambda bi, j, pt: (bi, j, 0))),
        out_shape=jax.ShapeDtypeStruct((n_batch, n_pages * blk_per_page * n_ch, HEAD_DIM), F32),
        compiler_params=_cparams(("arbitrary", "arbitrary")),
        name="cmp_summaries",
    )(pt_flat, *([cache4] * n_pg), pe_ch)


def _cmpsel_kernel(summ_ref, q_ref, wphi_ref, ocmp_ref, sel_ref, *, past, t_new, kvh_n, n_selp):
    grp = NSA_GROUP
    scale = HEAD_DIM ** -0.5
    n_ch = 2 * kvh_n
    n_half = summ_ref.shape[1] // (2 * n_ch)
    n_sel = n_half + 1

    def summaries(ch, odd):
        return summ_ref[0, pl.ds(odd * n_ch + ch, n_half, stride=2 * n_ch), :]
    tpos = lax.broadcasted_iota(jnp.int32, (t_new, 1), 0) + past
    pos = jnp.concatenate([tpos] * grp, axis=0)
    lane = lax.broadcasted_iota(jnp.int32, (1, n_half), 1)
    ok_e = (2 * lane + 1) * CMP_BLOCK - 1 <= pos
    ok_o = (2 * lane + 2) * CMP_BLOCK - 1 <= pos
    w_k, w_v = wphi_ref[0].astype(BF16), wphi_ref[1].astype(BF16)
    q = q_ref[0]
    scores = []
    for k in range(kvh_n):
        proj = lambda ch, odd, w: jnp.dot(summaries(ch, odd).astype(BF16), w,
                                          preferred_element_type=F32).astype(BF16)
        k_e, k_o = proj(k, 0, w_k), proj(k, 1, w_k)
        v_e, v_o = proj(kvh_n + k, 0, w_v), proj(kvh_n + k, 1, w_v)
        qk = jnp.concatenate(
            [q[:, (k * grp + g) * HEAD_DIM:(k * grp + g + 1) * HEAD_DIM] for g in range(grp)], axis=0).astype(BF16)
        s_e = jnp.where(ok_e, _nt(qk, k_e) * scale, NEG_BIG)
        s_o = jnp.where(ok_o, _nt(qk, k_o) * scale, NEG_BIG)
        m = jnp.maximum(jnp.max(s_e, axis=-1, keepdims=True), jnp.max(s_o, axis=-1, keepdims=True))
        p_e = jnp.where(ok_e, jnp.exp(s_e - m), 0.0)
        p_o = jnp.where(ok_o, jnp.exp(s_o - m), 0.0)
        den = jnp.maximum(jnp.sum(p_e, axis=-1, keepdims=True) + jnp.sum(p_o, axis=-1, keepdims=True), TINY)
        p_e, p_o = p_e / den, p_o / den
        o = (jnp.dot(p_e.astype(BF16), v_e, preferred_element_type=F32)
             + jnp.dot(p_o.astype(BF16), v_o, preferred_element_type=F32))
        imp = jnp.zeros((t_new, n_half), F32)
        for g in range(grp):
            hsl = slice((k * grp + g) * HEAD_DIM, (k * grp + g + 1) * HEAD_DIM)
            ocmp_ref[0, :, hsl] = o[g * t_new:(g + 1) * t_new]
            imp = imp + (p_e[g * t_new:(g + 1) * t_new] + p_o[g * t_new:(g + 1) * t_new])
        tail_lane = lax.broadcasted_iota(jnp.int32, (t_new, n_selp - n_half), 1)
        tail = jnp.where(tail_lane == 0, 0.0, -jnp.inf)
        scores.append(jnp.concatenate([imp, tail], axis=1))
    score = jnp.concatenate(scores, axis=0)
    pos_r = jnp.concatenate([tpos] * kvh_n, axis=0)
    blk = lax.broadcasted_iota(jnp.int32, (1, n_selp), 1)
    q_blk = pos_r // SEL_BLOCK
    forced = (blk == 0) | (blk == q_blk) | (blk == q_blk - 1)
    score = jnp.where(forced, score + FORCE_BONUS, score)
    score = jnp.where((blk * SEL_BLOCK > pos_r) | (blk >= n_sel), -jnp.inf, score)
    rank = _topk_rank(score, blk, n_sel)

    n_slot = sel_ref.shape[3]
    n_top = min(N_SELECT, n_sel)
    blk_f = blk.astype(F32)
    blk_hi = jnp.floor(blk_f * (1.0 / 16.0))
    blk_lo = blk_f - 16.0 * blk_hi
    before = jnp.where(lax.broadcasted_iota(jnp.int32, (n_selp, 1), 0) < blk, 1.0, 0.0).astype(BF16)
    slot = lax.broadcasted_iota(jnp.int32, (n_slot, 1), 0).astype(F32)
    for k in range(kvh_n):
        member = jnp.where(rank[k * t_new:(k + 1) * t_new] < n_top, 1.0, 0.0)
        member_past = jnp.where(blk < n_half, member, 0.0)
        used = jnp.max(member_past, axis=0, keepdims=True)
        n_before = jnp.dot(jnp.broadcast_to(used, (SUBLANES, n_selp)).astype(BF16), before,
                           preferred_element_type=F32)[0:1]
        place = jnp.where(used > 0.5, jnp.where(n_before == slot, 1.0, 0.0), 0.0).astype(BF16)
        rows = jnp.concatenate([blk_hi, blk_lo, used, jnp.zeros((SUBLANES - 3, n_selp), F32), member_past], axis=0)
        packed = _nt(rows.astype(BF16), place)
        has_new = jnp.broadcast_to(member[:, n_half:n_half + 1], (t_new, n_slot))
        sel_ref[0, k] = jnp.concatenate([packed, has_new], axis=0)


def _cmpsel(summ, proj3, wphi, mo, nsa_w, past):
    b, summ_rows, _ = summ.shape
    t_new = proj3.shape[1]
    kvh_n = nsa_w // HEAD_DIM // NSA_GROUP
    n_half = summ_rows // (4 * kvh_n)
    assert n_half == past // SEL_BLOCK
    n_selp = -(-(n_half + 1) // LANES) * LANES
    n_slot = t_new * min(N_SELECT, n_half + 1)
    sel_rows = SUBLANES + 2 * t_new
    assert t_new == SUBLANES
    return pl.pallas_call(
        functools.partial(_cmpsel_kernel, past=past, t_new=t_new, kvh_n=kvh_n, n_selp=n_selp),
        grid=(b,),
        in_specs=[pl.BlockSpec((1, summ_rows, HEAD_DIM), lambda bi: (bi, 0, 0)),
                  pl.BlockSpec((1, t_new, nsa_w), lambda bi: (bi, 0, mo['nsa_q'] // nsa_w)),
                  pl.BlockSpec((2, HEAD_DIM, HEAD_DIM), lambda bi: (0, 0, 0))],
        out_specs=[pl.BlockSpec((1, t_new, nsa_w), lambda bi: (bi, 0, 0)),
                   pl.BlockSpec((1, kvh_n, sel_rows, n_slot), lambda bi: (bi, 0, 0, 0))],
        out_shape=[jax.ShapeDtypeStruct((b, t_new, nsa_w), F32),
                   jax.ShapeDtypeStruct((b, kvh_n, sel_rows, n_slot), F32)],
        compiler_params=_cparams(("arbitrary",)),
        name="cmp_select",
    )(summ, proj3, wphi)


def _two_part_attention(parts):
    m = None
    for s, msk, _ in parts:
        mx = jnp.max(jnp.where(msk, s, NEG_BIG), axis=-1, keepdims=True)
        m = mx if m is None else jnp.maximum(m, mx)
    den, acc = 0.0, 0.0
    for s, msk, v in parts:
        p = jnp.where(msk, jnp.exp(jnp.where(msk, s, NEG_BIG) - m), 0.0)
        den = den + jnp.sum(p, axis=-1, keepdims=True)
        acc = acc + jnp.dot(p.astype(BF16), v, preferred_element_type=F32)
    return acc / jnp.maximum(den, TINY)


SLOT_CHUNK = 16


def _selwin_kernel(idx_ref, cnt_ref, pt_ref, cache_ref, qr_ref, ksn_ref, vsn_ref, wbuf_ref, kwn_ref, vwn_ref,
                   sm_ref, ocmp_ref, nz_ref, sel_ref, o_ref, kbuf, vbuf, sem,
                   *, layer, past, t_new, kvh_n, n_pages, page, n_heads):
    bi, kvh = pl.program_id(0), pl.program_id(1)
    grp = NSA_GROUP
    n_ch = 2 * kvh_n
    n_slot = kbuf.shape[0]
    n_past_blk = past // SEL_BLOCK
    per_page = page // SEL_BLOCK
    base = (bi * kvh_n + kvh) * n_slot
    n_chunk = (cnt_ref[bi * kvh_n + kvh] + SLOT_CHUNK - 1) // SLOT_CHUNK
    n_issue = n_chunk * SLOT_CHUNK

    def copies(i, row0):
        return (pltpu.make_async_copy(cache_ref.at[layer, pl.ds(row0, SEL_BLOCK), kvh, :], kbuf.at[i], sem.at[0]),
                pltpu.make_async_copy(cache_ref.at[layer, pl.ds(row0, SEL_BLOCK), kvh_n + kvh, :], vbuf.at[i],
                                      sem.at[1]))

    def issue(i, carry):
        idc = jnp.minimum(idx_ref[base + i], n_past_blk - 1)
        phys = pt_ref[bi * n_pages + idc // per_page]
        row0 = pl.multiple_of(phys * page + (idc % per_page) * SEL_BLOCK, SEL_BLOCK)
        for cp in copies(i, row0):
            cp.start()
        return carry

    def wait(i, carry):
        for cp in copies(i, 0):
            cp.wait()
        return carry

    lax.fori_loop(0, n_issue, issue, 0)
    lax.fori_loop(0, n_issue, wait, 0)

    qr = qr_ref[0]
    q4 = jnp.concatenate([qr[:, g * HEAD_DIM:(g + 1) * HEAD_DIM] for g in range(grp)], axis=0)
    rows = grp * t_new
    row_t = jnp.concatenate([lax.broadcasted_iota(jnp.int32, (t_new, 1), 0)] * grp, axis=0)
    pos = row_t + past

    def padded(ref):
        x = ref[0]
        return jnp.concatenate([x, jnp.zeros((SEL_BLOCK - t_new, HEAD_DIM), x.dtype)], axis=0).astype(BF16)

    new_j = lax.broadcasted_iota(jnp.int32, (1, SEL_BLOCK), 1)
    new_pos = past + new_j
    new_real = new_j < t_new

    flags = sel_ref[0, 0]
    uses_slot = jnp.concatenate([flags[SUBLANES:SUBLANES + t_new]] * grp, axis=0).astype(BF16)
    has_new = jnp.concatenate([flags[SUBLANES + t_new:SUBLANES + 2 * t_new, 0:1]] * grp, axis=0)
    width = SLOT_CHUNK * SEL_BLOCK

    def sel_body(c, carry):
        s0 = pl.multiple_of(c * SLOT_CHUNK, SLOT_CHUNK)
        k_c = kbuf[pl.ds(s0, SLOT_CHUNK)].reshape(width, HEAD_DIM).astype(BF16)
        v_c = vbuf[pl.ds(s0, SLOT_CHUNK)].reshape(width, HEAD_DIM).astype(BF16)
        slot_of_key = s0 + lax.broadcasted_iota(jnp.int32, (1, width), 1) // SEL_BLOCK
        expand = jnp.where(slot_of_key == lax.broadcasted_iota(jnp.int32, (n_slot, 1), 0), 1.0, 0.0).astype(BF16)
        bias = jnp.where(jnp.dot(uses_slot, expand, preferred_element_type=F32) > 0.5, 0.0, NEG_BIG)
        return _flash_step(carry, _nt(q4, k_c)[None], bias, v_c)

    carry = lax.fori_loop(0, n_chunk, sel_body, _flash_init((1, rows)))
    bias_new = jnp.where(jnp.where(new_real & (new_pos <= pos), has_new, 0.0) > 0.5, 0.0, NEG_BIG)
    _, l_sel, acc_sel = _flash_step(carry, _nt(q4, padded(ksn_ref))[None], bias_new, padded(vsn_ref))
    o_sel = acc_sel[0] / jnp.maximum(l_sel[0], TINY)

    n_buf = wbuf_ref.shape[1] // n_ch
    kwb = wbuf_ref[0, pl.ds(kvh, n_buf, stride=n_ch), :].astype(BF16)
    vwb = wbuf_ref[0, pl.ds(kvh_n + kvh, n_buf, stride=n_ch), :].astype(BF16)
    buf_pos = past - n_buf + lax.broadcasted_iota(jnp.int32, (1, n_buf), 1)
    msk_buf = jnp.where(buf_pos <= pos, pos - buf_pos, WINDOW) < WINDOW
    msk_wn = jnp.where(new_real & (new_pos <= pos), pos - new_pos, WINDOW) < WINDOW
    o_win = _two_part_attention([(_nt(q4, kwb), msk_buf, vwb),
                                 (_nt(q4, padded(kwn_ref)), msk_wn, padded(vwn_ref))])

    gates = jax.nn.sigmoid(sm_ref[0])
    ocmp, nz = ocmp_ref[0], nz_ref[0]
    for g in range(grp):
        head = kvh * grp + g
        rs = slice(g * t_new, (g + 1) * t_new)
        sl = slice(g * HEAD_DIM, (g + 1) * HEAD_DIM)
        o = (_lane_pick(gates, head) * ocmp[:, sl]
             + (_lane_pick(gates, n_heads + head) * o_sel[rs] + _lane_pick(gates, 2 * n_heads + head) * o_win[rs]))
        o_ref[0, :, sl] = (o * _silu(nz[:, sl])).astype(o_ref.dtype)


def _selwin(sel, pt_flat, cache4, layer, qr3, ksr3, win_state3, kwr3, small3, ocmp3, proj3, mo, nsa_w,
            past, n_pages, page):
    b, t_new, _ = qr3.shape
    n_heads = nsa_w // HEAD_DIM
    kvh_n = n_heads // NSA_GROUP
    gw = NSA_GROUP * HEAD_DIM
    sel_rows, n_slot = sel.shape[2], sel.shape[3]
    assert n_slot % SLOT_CHUNK == 0
    slot_blocks = (16.0 * sel[:, :, 0] + sel[:, :, 1]).astype(jnp.int32).reshape(-1)
    slot_count = jnp.sum(sel[:, :, 2], axis=-1).astype(jnp.int32).reshape(-1)
    win_rows = win_state3.shape[1]
    new_k = pl.BlockSpec((1, t_new, HEAD_DIM), lambda bi, k, *_: (bi, 0, k))
    new_v = pl.BlockSpec((1, t_new, HEAD_DIM), lambda bi, k, *_: (bi, 0, kvh_n + k))
    return pl.pallas_call(
        functools.partial(_selwin_kernel, layer=layer, past=past, t_new=t_new, kvh_n=kvh_n, n_pages=n_pages,
                          page=page, n_heads=n_heads),
        grid_spec=pltpu.PrefetchScalarGridSpec(
            num_scalar_prefetch=3,
            grid=(b, kvh_n),
            in_specs=[pl.BlockSpec(memory_space=pl.ANY),
                      pl.BlockSpec((1, t_new, gw), lambda bi, k, *_: (bi, 0, k)),
                      new_k, new_v,
                      pl.BlockSpec((1, win_rows, HEAD_DIM), lambda bi, k, *_: (layer * b + bi, 0, 0)),
                      new_k, new_v,
                      pl.BlockSpec((1, t_new, LANES), lambda bi, k, *_: (bi, 0, 0)),
                      pl.BlockSpec((1, t_new, gw), lambda bi, k, *_: (bi, 0, k)),
                      pl.BlockSpec((1, t_new, gw), lambda bi, k, *_: (bi, 0, mo['nsa_z'] // gw + k)),
                      pl.BlockSpec((1, 1, sel_rows, n_slot), lambda bi, k, *_: (bi, k, 0, 0))],
            out_specs=pl.BlockSpec((1, t_new, gw), lambda bi, k, *_: (bi, 0, k)),
            scratch_shapes=[pltpu.VMEM((n_slot, SEL_BLOCK, HEAD_DIM), F32),
                            pltpu.VMEM((n_slot, SEL_BLOCK, HEAD_DIM), F32),
                            pltpu.SemaphoreType.DMA((2,))]),
        out_shape=jax.ShapeDtypeStruct((b, t_new, nsa_w), BF16),
        compiler_params=_cparams(("arbitrary", "arbitrary")),
        name="sel_win_sample",
    )(slot_blocks, slot_count, pt_flat, cache4, qr3, ksr3, ksr3, win_state3, kwr3, kwr3, small3, ocmp3, proj3, sel)


def _conv_kernel(x_ref, b_ref, c_ref, z_ref, buf_ref, w_ref, o_ref, st_ref, prev):
    tc = x_ref.shape[1]

    @pl.when(pl.program_id(1) == 0)
    def _():
        prev[0:CONV_K - 1, :] = buf_ref[0]

    u = c_ref[0] * x_ref[0]
    row = lax.broadcasted_iota(jnp.int32, (tc, 1), 0)
    w = w_ref[...]
    y = u * w[CONV_K - 1:CONV_K, :]
    for back in range(1, CONV_K):
        shifted = pltpu.roll(u, back, axis=0)
        for r in range(back):
            shifted = jnp.where(row == r, prev[CONV_K - 1 - back + r:CONV_K - back + r, :], shifted)
        y = y + shifted * w[CONV_K - 1 - back:CONV_K - back, :]
    o_ref[0] = (b_ref[0] * y * _silu(z_ref[0])).astype(o_ref.dtype)
    tail = u[tc - (CONV_K - 1):, :]
    prev[0:CONV_K - 1, :] = tail
    st_ref[0] = tail


def _conv(proj3, buf, w, mo, conv_w):
    b, seq, _ = proj3.shape
    tc = _tile(seq, 512)
    assert tc >= CONV_K - 1
    col = lambda name: pl.BlockSpec((1, tc, conv_w), lambda bi, ti: (bi, ti, mo[name] // conv_w))
    return pl.pallas_call(
        _conv_kernel,
        grid=(b, seq // tc),
        in_specs=[col('conv_x'), col('conv_b'), col('conv_c'), col('conv_z'),
                  pl.BlockSpec((1, CONV_K - 1, conv_w), lambda bi, ti: (bi, 0, 0)),
                  pl.BlockSpec((CONV_K, conv_w), lambda bi, ti: (0, 0))],
        out_specs=[pl.BlockSpec((1, tc, conv_w), lambda bi, ti: (bi, ti, 0)),
                   pl.BlockSpec((1, CONV_K - 1, conv_w), lambda bi, ti: (bi, 0, 0))],
        out_shape=[jax.ShapeDtypeStruct((b, seq, conv_w), BF16),
                   jax.ShapeDtypeStruct((b, CONV_K - 1, conv_w), F32)],
        scratch_shapes=[pltpu.VMEM((8, conv_w), F32)],
        compiler_params=_cparams(("arbitrary", "arbitrary")),
        name="short_conv",
    )(proj3, proj3, proj3, proj3, buf, w)


def _log_sigmoid(x):
    return jnp.minimum(x, 0.0) - jnp.log(1.0 + jnp.exp(-jnp.abs(x)))


def _mlstm_kernel(q_ref, k_ref, v_ref, og_ref, z_ref, sm_ref, gr_ref, c0_ref, n0_ref, m0_ref, nw_ref,
                  o_ref, c_out, n_out, m_out, c_s, n_s, m_s, *, n_chunks, gate_i, gate_f):
    ci = pl.program_id(1)

    @pl.when(ci == 0)
    def _():
        c_s[...] = c0_ref[0]
        n_s[...] = n0_ref[0]
        m_s[...] = m0_ref[0]

    small, gr = sm_ref[0], gr_ref[0]
    dk = c_s.shape[1]
    for h in range(MLSTM_HEADS):
        hs = slice(h * dk, (h + 1) * dk)
        c_new, n_new, m_new, out = _mlstm_head(
            q_ref[0, :, hs], k_ref[0, :, hs], v_ref[0, :, hs], og_ref[0, :, hs], z_ref[0, :, hs],
            small[:, gate_i + h:gate_i + h + 1], small[:, gate_f + h:gate_f + h + 1],
            gr[h:h + 1, :], gr[MLSTM_HEADS + h:MLSTM_HEADS + h + 1, :],
            c_s[h], n_s[h], m_s[h], nw_ref[h])
        c_s[h] = c_new
        n_s[h] = n_new
        m_s[h] = m_new
        o_ref[0, :, hs] = out.astype(o_ref.dtype)

    @pl.when(ci == n_chunks - 1)
    def _():
        c_out[0] = c_s[...]
        n_out[0] = n_s[...]
        m_out[0] = m_s[...]


def _mlstm_head(q, k, v, og, z, i_col, f_col, i_row, f_row, c_prev, n_prev, m_prev, norm_w):
    lc, dk = q.shape
    lf_col, lf_row = _log_sigmoid(f_col), _log_sigmoid(f_row)
    ti = lax.broadcasted_iota(jnp.int32, (lc, 1), 0)
    si = lax.broadcasted_iota(jnp.int32, (1, lc), 1)
    causal = si <= ti
    b_col = jnp.sum(jnp.where(causal, lf_row, 0.0), axis=1, keepdims=True)
    b_row = jnp.sum(jnp.where(ti <= si, lf_col, 0.0), axis=0, keepdims=True)
    log_d = jnp.where(causal, b_col - b_row + i_row, -jnp.inf)
    m_t = jnp.maximum(b_col + m_prev, jnp.max(log_d, axis=1, keepdims=True))
    inter = jnp.exp(b_col + m_prev - m_t)
    kf = k * (dk ** -0.5)
    qb, kb, vb = q.astype(BF16), kf.astype(BF16), v.astype(BF16)
    s = _nt(qb, kb) * jnp.exp(log_d - m_t)
    num = (inter * jnp.dot(qb, c_prev.astype(BF16), preferred_element_type=F32)
           + jnp.dot(s.astype(BF16), vb, preferred_element_type=F32))
    den = inter * jnp.sum(q * n_prev, axis=1, keepdims=True) + jnp.sum(s, axis=1, keepdims=True)
    hid = num / jnp.maximum(jnp.abs(den), jnp.exp(-m_t))

    m_new = m_t[lc - 1:lc, :]
    b_last = b_col[lc - 1:lc, :]
    wdec = jnp.exp(b_last - b_col + i_col - m_new)
    cdec = jnp.exp(b_last + m_prev - m_new)
    c_new = cdec * c_prev + lax.dot_general(kb, (wdec * v).astype(BF16), (((0,), (0,)), ((), ())),
                                            preferred_element_type=F32)
    n_new = cdec * n_prev + jnp.sum(wdec * kf, axis=0, keepdims=True)
    hn = hid * lax.rsqrt(jnp.mean(hid * hid, axis=-1, keepdims=True) + RMS_EPS) * norm_w
    return c_new, n_new, m_new, jax.nn.sigmoid(og) * hn * _silu(z)


def _mlstm(proj3, small3, gates_row, c0, n0, m0, norm_w, mo, ml_w, gate_i, gate_f):
    b, seq, _ = proj3.shape
    nh = MLSTM_HEADS
    dk = ml_w // nh
    lc = _tile(seq, 256)
    n_chunks = seq // lc
    col = lambda name: pl.BlockSpec((1, lc, ml_w), lambda bi, ci: (bi, ci, mo[name] // ml_w))
    state = lambda shape: pl.BlockSpec((1, nh) + shape, lambda bi, ci: (bi, 0, 0, 0))
    return pl.pallas_call(
        functools.partial(_mlstm_kernel, n_chunks=n_chunks, gate_i=gate_i, gate_f=gate_f),
        grid=(b, n_chunks),
        in_specs=[col('m_q'), col('m_k'), col('m_v'), col('m_o'), col('m_z'),
                  pl.BlockSpec((1, lc, LANES), lambda bi, ci: (bi, ci, 0)),
                  pl.BlockSpec((1, 2 * nh, lc), lambda bi, ci: (bi, 0, ci)),
                  state((dk, dk)), state((1, dk)), state((1, 1)),
                  pl.BlockSpec((nh, 1, dk), lambda bi, ci: (0, 0, 0))],
        out_specs=[pl.BlockSpec((1, lc, ml_w), lambda bi, ci: (bi, ci, 0)),
                   state((dk, dk)), state((1, dk)), state((1, 1))],
        out_shape=[jax.ShapeDtypeStruct((b, seq, ml_w), BF16),
                   jax.ShapeDtypeStruct((b, nh, dk, dk), F32),
                   jax.ShapeDtypeStruct((b, nh, 1, dk), F32),
                   jax.ShapeDtypeStruct((b, nh, 1, 1), F32)],
        scratch_shapes=[pltpu.VMEM((nh, dk, dk), F32), pltpu.VMEM((nh, 1, dk), F32), pltpu.VMEM((nh, 1, 1), F32)],
        compiler_params=_cparams(("arbitrary", "arbitrary")),
        name="mlstm",
    )(proj3, proj3, proj3, proj3, proj3, small3, gates_row, c0, n0, m0, norm_w)


def _outproj_kernel(x_ref, g_ref, a_ref, c_ref, m_ref, w_ref, o_ref):
    ka, kc = a_ref.shape[1], c_ref.shape[1]
    acc = jnp.dot(a_ref[...], w_ref[0, 0:ka, :], preferred_element_type=F32)
    acc = acc + jnp.dot(c_ref[...], w_ref[0, ka:ka + kc, :], preferred_element_type=F32)
    acc = acc + jnp.dot(m_ref[...], w_ref[0, ka + kc:, :], preferred_element_type=F32)
    o_ref[...] = x_ref[...] + g_ref[0] * acc


def _outproj(x2, gate3, o_nsa, o_conv, o_ml, w_out, layer, tm, rows_per_mod):
    m, d = x2.shape
    tn = _tile(d, 1024)
    r = gate3.shape[1]
    kdim = w_out.shape[1]
    lhs = lambda a: pl.BlockSpec((tm, a.shape[1]), lambda i, j: (i, 0))
    return pl.pallas_call(
        _outproj_kernel,
        grid=(m // tm, d // tn),
        in_specs=[pl.BlockSpec((tm, tn), lambda i, j: (i, j)),
                  pl.BlockSpec((1, r, tn), lambda i, j: (i // rows_per_mod, 0, j)),
                  lhs(o_nsa), lhs(o_conv), lhs(o_ml),
                  pl.BlockSpec((1, kdim, tn), lambda i, j: (layer, 0, j))],
        out_specs=pl.BlockSpec((tm, tn), lambda i, j: (i, j)),
        out_shape=jax.ShapeDtypeStruct((m, d), F32),
        compiler_params=_cparams(("arbitrary", "arbitrary")),
        name="outproj",
    )(x2, gate3, o_nsa, o_conv, o_ml, w_out)


def _rmsnorm_kernel(x_ref, w_ref, o_ref):
    x = x_ref[...]
    o_ref[...] = x * lax.rsqrt(jnp.mean(x * x, axis=-1, keepdims=True) + RMS_EPS) * w_ref[...]


def _rmsnorm(x2, w):
    m, d = x2.shape
    tm = _tile(m, 256)
    return pl.pallas_call(
        _rmsnorm_kernel,
        grid=(m // tm,),
        in_specs=[pl.BlockSpec((tm, d), lambda i: (i, 0)), pl.BlockSpec((1, d), lambda i: (0, 0))],
        out_specs=pl.BlockSpec((tm, d), lambda i: (i, 0)),
        out_shape=jax.ShapeDtypeStruct((m, d), F32),
        compiler_params=_cparams(("arbitrary",)),
        name="final_norm",
    )(x2, w.reshape(1, d))


def _layer(x3, mod, lw, cfg, past, states):
    b, seq, d = x3.shape
    m = b * seq
    mo, nsa_w, kv_w, conv_w, ml_w = cfg['mo'], cfg['nsa_w'], cfg['kv_w'], cfg['conv_w'], cfg['ml_w']
    shift, scale, gate = mod
    x2 = x3.reshape(m, d)
    if seq >= 512:
        tm, rows_per_mod = 512, seq // 512
        per = lambda a: a.reshape(b, 1, d)
    else:
        tm, rows_per_mod = m, 1
        per = lambda a: jnp.repeat(a, seq, axis=0).reshape(1, m, d)
    h2 = _modnorm(x2, per(scale), per(shift), lw['norm_w'], tm, rows_per_mod)
    proj, small = _inproj(h2, lw['w_main'], lw['b_main'], lw['w_small'], lw['b_small'], lw['layer'])
    nm = proj.shape[1]
    proj3, small3 = proj.reshape(b, seq, nm), small.reshape(b, seq, LANES)

    pos = past + jnp.arange(seq, dtype=jnp.int32)
    cos, sin = _rope_tables(pos)
    qr, ksr, kwr = _rope(proj, jnp.tile(cos, (b, 1)), jnp.tile(sin, (b, 1)), mo, nsa_w, kv_w)
    qr3, ksr3, kwr3 = qr.reshape(b, seq, nsa_w), ksr.reshape(b, seq, kv_w), kwr.reshape(b, seq, kv_w)

    if states is None:
        o_nsa = _nsa_prompt(proj3, small3, qr3, ksr3, kwr3, lw['cmp_pe'], lw['cmp_w'], mo, nsa_w)
        conv_buf = jnp.zeros((b, CONV_K - 1, conv_w), F32)
        dk = ml_w // MLSTM_HEADS
        c0 = jnp.zeros((b, MLSTM_HEADS, dk, dk), F32)
        n0 = jnp.zeros((b, MLSTM_HEADS, 1, dk), F32)
        m0 = jnp.zeros((b, MLSTM_HEADS, 1, 1), F32)
    else:
        n_pages, page = states['n_pages'], states['page']
        summ = _summaries(states['cache_cmp'], states['layer'], states['pt_flat'], lw['pe_ch'], b, n_pages,
                          kv_w // HEAD_DIM)
        o_cmp, sel = _cmpsel(summ, proj3, lw['cmp_w'], mo, nsa_w, past)
        o_nsa = _selwin(sel, states['pt_flat'], states['cache_sel'], states['layer'], qr3, ksr3,
                        states['win'], kwr3, small3, o_cmp, proj3, mo, nsa_w, past, n_pages, page)
        conv_buf = states['conv']
        dk = ml_w // MLSTM_HEADS
        c0 = states['C']
        n0 = states['n'].reshape(b, MLSTM_HEADS, 1, dk)
        m0 = states['m'].reshape(b, MLSTM_HEADS, 1, 1)

    o_conv, new_conv = _conv(proj3, conv_buf, lw['conv_w'], mo, conv_w)
    gates_row = jnp.swapaxes(small3[:, :, cfg['gate_i']:cfg['gate_i'] + 2 * MLSTM_HEADS], 1, 2)
    o_ml, c1, n1, m1 = _mlstm(proj3, small3, gates_row, c0, n0, m0, lw['mlstm_norm_w'], mo, ml_w,
                              cfg['gate_i'], cfg['gate_f'])

    y = _outproj(x2, per(gate), o_nsa.reshape(m, nsa_w), o_conv.reshape(m, conv_w), o_ml.reshape(m, ml_w),
                 lw['w_out'], lw['layer'], tm, rows_per_mod)
    kv_shape = (b, seq, 2, kv_w // (2 * HEAD_DIM), HEAD_DIM)
    new_cmp = proj3[:, :, mo['kv_cmp']:mo['kv_cmp'] + kv_w].reshape(kv_shape)
    dk = ml_w // MLSTM_HEADS
    st = (new_cmp, ksr3.reshape(kv_shape), kwr3.reshape(kv_shape), new_conv, c1,
          n1.reshape(b, MLSTM_HEADS, dk), m1.reshape(b, MLSTM_HEADS))
    return y.reshape(b, seq, d), st


def kernel(x_prompt, x_sample, cache_cmp_kv, cache_sel_kv, state_win_kv, state_conv, state_mlstm_C, state_mlstm_n,
           state_mlstm_m, page_table, c_prompt, c_sample, norm_w, w_ada, b_ada, w_in, b_in, cmp_pe, cmp_w, conv_w,
           mlstm_norm_w, w_out, final_norm_w):
    bp, seq, d = x_prompt.shape
    bs, t_new, _ = x_sample.shape
    depth = w_in.shape[0]
    layout = _in_layout(d)
    sizes = dict(layout)
    in_off, _ = _offsets([n for n, _ in layout], sizes)
    mo, n_main = _offsets(MAIN_NAMES, sizes)
    so, n_small = _offsets(SMALL_NAMES, sizes)
    assert n_small <= LANES
    nsa_w, kv_w, conv_wd, ml_w = sizes['nsa_q'], sizes['kv_cmp'], sizes['conv_x'], sizes['m_q']
    cfg = dict(mo=mo, nsa_w=nsa_w, kv_w=kv_w, conv_w=conv_wd, ml_w=ml_w, gate_i=so['m_i'], gate_f=so['m_f'])
    n_kv = kv_w // (2 * HEAD_DIM)

    def gather_cols(a, names, pad_to):
        parts = [a[..., in_off[n]:in_off[n] + sizes[n]] for n in names]
        width = sum(sizes[n] for n in names)
        if pad_to > width:
            parts.append(jnp.zeros(a.shape[:-1] + (pad_to - width,), a.dtype))
        return jnp.concatenate(parts, axis=-1)

    segs_main = tuple((in_off[n], sizes[n], mo[n]) for n in MAIN_NAMES)
    w_main_all = _wprep(w_in, segs_main, n_main)
    w_small_all = gather_cols(w_in, SMALL_NAMES, LANES).astype(BF16)
    w_out_bf = w_out.astype(BF16)

    layers = []
    for l in range(depth):
        pe_ch = jnp.repeat(jnp.transpose(cmp_pe[l], (1, 0, 2)), n_kv, axis=1)
        layers.append(dict(
            layer=l, norm_w=norm_w[l].reshape(1, d),
            w_main=w_main_all,
            b_main=gather_cols(b_in[l], MAIN_NAMES, n_main).reshape(1, n_main),
            w_small=w_small_all,
            b_small=gather_cols(b_in[l], SMALL_NAMES, LANES).reshape(1, LANES),
            cmp_pe=cmp_pe[l], cmp_w=cmp_w[l], pe_ch=pe_ch, conv_w=conv_w[l],
            mlstm_norm_w=mlstm_norm_w[l].reshape(MLSTM_HEADS, 1, ml_w // MLSTM_HEADS),
            w_out=w_out_bf))

    n_rows = bp + bs
    pad_rows = -(-n_rows // 8) * 8
    c_all = jnp.concatenate([c_prompt, c_sample, jnp.zeros((pad_rows - n_rows, d), F32)], axis=0)
    mod = _ada(c_all, w_ada, b_ada)

    def mods(l, lo, hi):
        return tuple(mod[l, lo:hi, i * d:(i + 1) * d] for i in range(3))

    xp = x_prompt
    ps = []
    for l in range(depth):
        xp, st = _layer(xp, mods(l, 0, bp), layers[l], cfg, 0, None)
        ps.append(st)

    n_phys, page = cache_cmp_kv.shape[1], cache_cmp_kv.shape[2]
    n_pages = page_table.shape[1]
    past = n_pages * page
    assert past % SEL_BLOCK == 0 and t_new <= CMP_BLOCK and state_win_kv.shape[2] <= WINDOW
    pt_flat = page_table.reshape(-1).astype(jnp.int32)
    n_ch = 2 * n_kv
    cache_cmp4 = cache_cmp_kv.reshape(depth, n_phys, page * n_ch, HEAD_DIM)
    cache_sel3 = cache_sel_kv.reshape(depth, n_phys * page, n_ch, HEAD_DIM)
    win4 = state_win_kv.reshape(depth * bs, state_win_kv.shape[2] * n_ch, HEAD_DIM)
    xs = x_sample
    ss = []
    for l in range(depth):
        states = dict(layer=l, n_pages=n_pages, page=page, pt_flat=pt_flat, cache_cmp=cache_cmp4,
                      cache_sel=cache_sel3, win=win4, conv=state_conv[l], C=state_mlstm_C[l], n=state_mlstm_n[l],
                      m=state_mlstm_m[l])
        xs, st = _layer(xs, mods(l, bp, bp + bs), layers[l], cfg, past, states)
        ss.append(st)

    y_prompt = _rmsnorm(xp.reshape(bp * seq, d), final_norm_w).reshape(bp, seq, d)
    y_sample = _rmsnorm(xs.reshape(bs * t_new, d), final_norm_w).reshape(bs, t_new, d)
    n_keep = min(WINDOW, seq)
    stack = lambda sts, i: jnp.stack([s[i] for s in sts])
    p_win = jnp.stack([s[2][:, seq - n_keep:] for s in ps])
    return (y_prompt, y_sample, stack(ps, 0), stack(ps, 1), p_win, stack(ps, 3), stack(ps, 4), stack(ps, 5),
            stack(ps, 6), stack(ss, 0), stack(ss, 1), stack(ss, 2), stack(ss, 3), stack(ss, 4), stack(ss, 5),
            stack(ss, 6))
```

```python
import functools

import jax
import jax.numpy as jnp
from jax import lax
from jax.experimental import pallas as pl
from jax.experimental.pallas import tpu as pltpu

HEAD_DIM = 128
NSA_GROUP = 4
CMP_BLOCK = 32
SEL_BLOCK = 64
N_SELECT = 16
WINDOW = 512
FORCE_BONUS = 1.0e4
CONV_K = 3
MLSTM_HEADS = 4
ROPE_THETA = 10000.0
RMS_EPS = 1e-6
NEG_BIG = -1e30
TINY = 1e-30

LANES = 128
VMEM_LIMIT = 56 * 1024 * 1024

F32 = jnp.float32
BF16 = jnp.bfloat16

MAIN_NAMES = ('nsa_q', 'kv_cmp', 'kv_sel', 'kv_win', 'nsa_z', 'conv_x', 'conv_b', 'conv_c', 'conv_z',
              'm_q', 'm_k', 'm_v', 'm_o', 'm_z')
SMALL_NAMES = ('nsa_gate', 'm_i', 'm_f')


def _in_layout(d_model):
    nsa_w, kv_w, conv_w, ml_w = d_model // 2, d_model // 4, d_model // 4, d_model // 4
    nsa_heads = nsa_w // HEAD_DIM
    return (('nsa_q', nsa_w), ('kv_cmp', kv_w), ('kv_sel', kv_w), ('kv_win', kv_w),
            ('nsa_gate', 3 * nsa_heads), ('nsa_z', nsa_w),
            ('conv_x', conv_w), ('conv_b', conv_w), ('conv_c', conv_w), ('conv_z', conv_w),
            ('m_q', ml_w), ('m_k', ml_w), ('m_v', ml_w),
            ('m_i', MLSTM_HEADS), ('m_f', MLSTM_HEADS), ('m_o', ml_w), ('m_z', ml_w))


def _offsets(names, sizes):
    off, out = 0, {}
    for n in names:
        out[n] = off
        off += sizes[n]
    return out, off


def _cparams(sem):
    return pltpu.CompilerParams(dimension_semantics=sem, vmem_limit_bytes=VMEM_LIMIT)


def _tile(dim, pref):
    t = min(dim, pref)
    assert dim % t == 0, (dim, pref)
    return t


def _nt(a, b):
    return lax.dot_general(a, b, (((1,), (1,)), ((), ())), preferred_element_type=F32)


def _silu(x):
    return x * jax.nn.sigmoid(x)


def _lane_pick(x, col):
    lane = lax.broadcasted_iota(jnp.int32, x.shape, 1)
    return jnp.sum(jnp.where(lane == col, x, 0.0), axis=1, keepdims=True)


def _ada_kernel(c_ref, w_ref, b_ref, o_ref):
    a = _silu(c_ref[...]).astype(BF16)
    o_ref[0] = jnp.dot(a, w_ref[0].astype(BF16), preferred_element_type=F32) + b_ref[0]


def _ada(c_all, w_ada, b_ada):
    rows, d = c_all.shape
    depth, _, n = w_ada.shape
    tn = _tile(n, 512)
    return pl.pallas_call(
        _ada_kernel,
        grid=(depth, n // tn),
        in_specs=[pl.BlockSpec((rows, d), lambda l, j: (0, 0)),
                  pl.BlockSpec((1, d, tn), lambda l, j: (l, 0, j)),
                  pl.BlockSpec((1, 1, tn), lambda l, j: (l, 0, j))],
        out_specs=pl.BlockSpec((1, rows, tn), lambda l, j: (l, 0, j)),
        out_shape=jax.ShapeDtypeStruct((depth, rows, n), F32),
        compiler_params=_cparams(("arbitrary", "arbitrary")),
        name="ada",
    )(c_all, w_ada, b_ada.reshape(depth, 1, n))


SUBLANES = 8


def _wprep_kernel(tab_ref, wt_ref, wm_ref):
    del tab_ref
    wm_ref[0] = wt_ref[0].T.astype(wm_ref.dtype)


def _wprep(w_in, segs_main, n_main):
    depth, d, _ = w_in.shape
    tn = 512
    src_rows = []
    for src, size, dst in segs_main:
        assert size % tn == 0 and dst % tn == 0 and src % SUBLANES == 0
        src_rows += [(src + i) // SUBLANES for i in range(0, size, tn)]
    table = jnp.asarray(src_rows, jnp.int32)
    elem = lambda n: pl.Element(n)
    return pl.pallas_call(
        _wprep_kernel,
        grid_spec=pltpu.PrefetchScalarGridSpec(
            num_scalar_prefetch=1,
            grid=(depth, n_main // tn),
            in_specs=[pl.BlockSpec((elem(1), elem(tn), elem(d)), lambda l, j, tab: (l, tab[j] * SUBLANES, 0))],
            out_specs=pl.BlockSpec((1, d, tn), lambda l, j, tab: (l, 0, j))),
        out_shape=jax.ShapeDtypeStruct((depth, d, n_main), BF16),
        compiler_params=_cparams(("arbitrary", "arbitrary")),
        name="weight_prep",
    )(table, jnp.swapaxes(w_in, 1, 2))


def _modnorm_kernel(x_ref, sc_ref, sh_ref, nw_ref, h_ref):
    x = x_ref[...]
    y = x * lax.rsqrt(jnp.mean(x * x, axis=-1, keepdims=True) + RMS_EPS) * nw_ref[...]
    h_ref[...] = (y * (1.0 + sc_ref[0]) + sh_ref[0]).astype(h_ref.dtype)


def _modnorm(x2, scale3, shift3, nw, tm, rows_per_mod):
    m, d = x2.shape
    r = scale3.shape[1]
    mod_spec = pl.BlockSpec((1, r, d), lambda i: (i // rows_per_mod, 0, 0))
    return pl.pallas_call(
        _modnorm_kernel,
        grid=(m // tm,),
        in_specs=[pl.BlockSpec((tm, d), lambda i: (i, 0)), mod_spec, mod_spec,
                  pl.BlockSpec((1, d), lambda i: (0, 0))],
        out_specs=pl.BlockSpec((tm, d), lambda i: (i, 0)),
        out_shape=jax.ShapeDtypeStruct((m, d), BF16),
        compiler_params=_cparams(("arbitrary",)),
        name="modnorm",
    )(x2, scale3, shift3, nw)


def _inproj_kernel(h_ref, wm_ref, bm_ref, ws_ref, bs_ref, o_ref, os_ref):
    @pl.when(pl.program_id(1) == 0)
    def _():
        os_ref[...] = jnp.dot(h_ref[...], ws_ref[0], preferred_element_type=F32) + bs_ref[...]

    o_ref[...] = jnp.dot(h_ref[...], wm_ref[0], preferred_element_type=F32) + bm_ref[...]


def _inproj(h2, w_main, b_main, w_small, b_small, layer):
    m, d = h2.shape
    nm = w_main.shape[2]
    tm, tn = _tile(m, 1024), _tile(nm, 1024)
    return pl.pallas_call(
        _inproj_kernel,
        grid=(m // tm, nm // tn),
        in_specs=[pl.BlockSpec((tm, d), lambda i, j: (i, 0)),
                  pl.BlockSpec((1, d, tn), lambda i, j: (layer, 0, j)),
                  pl.BlockSpec((1, tn), lambda i, j: (0, j)),
                  pl.BlockSpec((1, d, LANES), lambda i, j: (layer, 0, 0)),
                  pl.BlockSpec((1, LANES), lambda i, j: (0, 0))],
        out_specs=[pl.BlockSpec((tm, tn), lambda i, j: (i, j)),
                   pl.BlockSpec((tm, LANES), lambda i, j: (i, 0))],
        out_shape=[jax.ShapeDtypeStruct((m, nm), F32), jax.ShapeDtypeStruct((m, LANES), F32)],
        compiler_params=_cparams(("arbitrary", "arbitrary")),
        name="inproj",
    )(h2, w_main, b_main, w_small, b_small)


def _rope_kernel(q_ref, ks_ref, kw_ref, cos_ref, sin_ref, qr_ref, ksr_ref, kwr_ref, *, n_q, n_kv):
    cos, sin = cos_ref[...], sin_ref[...]

    def rot(x):
        return x * cos + pltpu.roll(x, HEAD_DIM // 2, axis=1) * sin

    for h in range(n_q):
        sl = slice(h * HEAD_DIM, (h + 1) * HEAD_DIM)
        qr_ref[:, sl] = (rot(q_ref[:, sl]) * HEAD_DIM ** -0.5).astype(qr_ref.dtype)
    for src, dst in ((ks_ref, ksr_ref), (kw_ref, kwr_ref)):
        for h in range(n_kv):
            sl = slice(h * HEAD_DIM, (h + 1) * HEAD_DIM)
            dst[:, sl] = rot(src[:, sl])
        dst[:, n_kv * HEAD_DIM:] = src[:, n_kv * HEAD_DIM:]


def _rope(proj, cos, sin, mo, nsa_w, kv_w):
    m = proj.shape[0]
    tr = _tile(m, 256)
    n_q, n_kv = nsa_w // HEAD_DIM, kv_w // (2 * HEAD_DIM)
    return pl.pallas_call(
        functools.partial(_rope_kernel, n_q=n_q, n_kv=n_kv),
        grid=(m // tr,),
        in_specs=[pl.BlockSpec((tr, nsa_w), lambda i: (i, mo['nsa_q'] // nsa_w)),
                  pl.BlockSpec((tr, kv_w), lambda i: (i, mo['kv_sel'] // kv_w)),
                  pl.BlockSpec((tr, kv_w), lambda i: (i, mo['kv_win'] // kv_w)),
                  pl.BlockSpec((tr, HEAD_DIM), lambda i: (i, 0)),
                  pl.BlockSpec((tr, HEAD_DIM), lambda i: (i, 0))],
        out_specs=[pl.BlockSpec((tr, nsa_w), lambda i: (i, 0)),
                   pl.BlockSpec((tr, kv_w), lambda i: (i, 0)),
                   pl.BlockSpec((tr, kv_w), lambda i: (i, 0))],
        out_shape=[jax.ShapeDtypeStruct((m, nsa_w), BF16),
                   jax.ShapeDtypeStruct((m, kv_w), F32),
                   jax.ShapeDtypeStruct((m, kv_w), F32)],
        compiler_params=_cparams(("arbitrary",)),
        name="rope",
    )(proj, proj, proj, cos, sin)


def _rope_tables(pos):
    half = HEAD_DIM // 2
    inv = ROPE_THETA ** (-jnp.arange(half, dtype=F32) / half)
    ang = pos.astype(F32)[:, None] * inv[None, :]
    cos, sin = jnp.cos(ang), jnp.sin(ang)
    return jnp.concatenate([cos, cos], axis=1), jnp.concatenate([-sin, sin], axis=1)


def _flash_step(carry, s, bias, v_bf):
    m, l, acc = carry
    if bias is not None:
        s = s + bias[None]
    m_new = jnp.maximum(m, jnp.max(s, axis=-1, keepdims=True))
    p = jnp.exp(s - m_new)
    alpha = jnp.exp(m - m_new)
    l = alpha * l + jnp.sum(p, axis=-1, keepdims=True)
    pv = jnp.dot(p.reshape(-1, p.shape[-1]).astype(BF16), v_bf, preferred_element_type=F32)
    acc = alpha * acc + pv.reshape(acc.shape)
    return m_new, l, acc


def _flash_init(shape_rows):
    return (jnp.full(shape_rows + (1,), NEG_BIG, F32), jnp.zeros(shape_rows + (1,), F32),
            jnp.zeros(shape_rows + (HEAD_DIM,), F32))


def _topk_rank(score, blk, n_blocks, axis=1):
    rank = jnp.zeros(score.shape, F32)
    for i in range(n_blocks):
        si = score[:, i:i + 1] if axis == 1 else score[i:i + 1, :]
        ge = jnp.where(si >= score, 1.0, 0.0)
        gt = jnp.where(si > score, 1.0, 0.0)
        rank = rank + jnp.where(blk > i, ge, gt)
    return rank


def _nsa_prompt_kernel(qc_ref, qr_ref, kc_ref, vc_ref, ks_ref, vs_ref, kw_ref, vw_ref, nz_ref, sm_ref,
                       pe_ref, wphi_ref, o_ref, kcs, vcs, *, seq, tq, tk, n_heads):
    kvh, qi = pl.program_id(1), pl.program_id(2)
    n_sel = seq // SEL_BLOCK
    n_cmp = 2 * n_sel
    grp = NSA_GROUP
    scale = HEAD_DIM ** -0.5

    @pl.when(qi == 0)
    def _():
        for c, (src, dst) in enumerate(((kc_ref, kcs), (vc_ref, vcs))):
            x = src[0].reshape(n_sel, 2 * CMP_BLOCK, HEAD_DIM)
            pe = pe_ref[c][None]
            ev = jnp.mean(x[:, :CMP_BLOCK, :] + pe, axis=1)
            od = jnp.mean(x[:, CMP_BLOCK:, :] + pe, axis=1)
            summ = jnp.concatenate([ev, od], axis=0).astype(BF16)
            dst[...] = jnp.dot(summ, wphi_ref[c].astype(BF16), preferred_element_type=F32)

    t0 = qi * tq
    pos = t0 + lax.broadcasted_iota(jnp.int32, (tq, 1), 0)
    pos_l = t0 + lax.broadcasted_iota(jnp.int32, (1, tq), 1)

    row_c = lax.broadcasted_iota(jnp.int32, (n_cmp, 1), 0)
    cblk = jnp.where(row_c < n_sel, 2 * row_c, 2 * (row_c - n_sel) + 1)
    cmp_ok = (cblk + 1) * CMP_BLOCK - 1 <= pos_l
    kcb, vcb = kcs[...].astype(BF16), vcs[...].astype(BF16)
    qc = qc_ref[0]
    imp = jnp.zeros((n_cmp, tq), F32)
    o_cmp = []
    for g in range(grp):
        s = _nt(kcb, qc[:, g * HEAD_DIM:(g + 1) * HEAD_DIM].astype(BF16)) * scale
        s = jnp.where(cmp_ok, s, NEG_BIG)
        s = s - jnp.max(s, axis=0, keepdims=True)
        p = jnp.where(cmp_ok, jnp.exp(s), 0.0)
        p = p / jnp.maximum(jnp.sum(p, axis=0, keepdims=True), TINY)
        imp = imp + p
        o_cmp.append(lax.dot_general(p.astype(BF16), vcb, (((0,), (0,)), ((), ())), preferred_element_type=F32))

    imp_sel = imp[:n_sel] + imp[n_sel:]
    blk = lax.broadcasted_iota(jnp.int32, (n_sel, 1), 0)
    q_blk = pos_l // SEL_BLOCK
    forced = (blk == 0) | (blk == q_blk) | (blk == q_blk - 1)
    score = jnp.where(forced, imp_sel + FORCE_BONUS, imp_sel)
    score = jnp.where(blk * SEL_BLOCK > pos_l, -jnp.inf, score)
    rank = _topk_rank(score, blk, n_sel, axis=0)
    sel_neg = jnp.concatenate([jnp.where(rank < N_SELECT, 0.0, NEG_BIG), jnp.zeros((LANES - n_sel, tq), F32)], axis=0)
    sel_neg = sel_neg.T.astype(BF16)

    qr = qr_ref[0]
    q4 = jnp.concatenate([qr[:, g * HEAD_DIM:(g + 1) * HEAD_DIM] for g in range(grp)], axis=0)
    q4_sel = jnp.concatenate([q4, jnp.concatenate([sel_neg] * grp, axis=0)], axis=1)

    def sel_keys(k0):
        kch = ks_ref[0, pl.ds(k0, tk), :].astype(BF16)
        kblk = (k0 + lax.broadcasted_iota(jnp.int32, (tk, 1), 0)) // SEL_BLOCK
        one_hot = jnp.where(kblk == lax.broadcasted_iota(jnp.int32, (1, LANES), 1), 1.0, 0.0).astype(BF16)
        return jnp.concatenate([kch, one_hot], axis=1), vs_ref[0, pl.ds(k0, tk), :].astype(BF16)

    def sel_step(k0, carry, bias):
        k_aug, vch = sel_keys(k0)
        return _flash_step(carry, _nt(q4_sel, k_aug).reshape(grp, tq, tk), bias, vch)

    n_chunks = (t0 + tq + tk - 1) // tk
    carry = lax.fori_loop(0, n_chunks - 1, lambda c, cr: sel_step(pl.multiple_of(c * tk, tk), cr, None),
                          _flash_init((grp, tq)))
    k_last = pl.multiple_of((n_chunks - 1) * tk, tk)
    kpos = k_last + lax.broadcasted_iota(jnp.int32, (1, tk), 1)
    _, l_s, acc_s = sel_step(k_last, carry, jnp.where(kpos <= pos, 0.0, NEG_BIG))

    def win_body(i, carry):
        k0 = pl.multiple_of((qi - i) * tq, tq)
        kpos = k0 + lax.broadcasted_iota(jnp.int32, (1, tq), 1)
        bias = jnp.where(jnp.where(kpos <= pos, pos - kpos, WINDOW) < WINDOW, 0.0, NEG_BIG)
        s = _nt(q4, kw_ref[0, pl.ds(k0, tq), :].astype(BF16)).reshape(grp, tq, tq)
        return _flash_step(carry, s, bias, vw_ref[0, pl.ds(k0, tq), :].astype(BF16))

    n_back = jnp.minimum(qi, (WINDOW + tq - 1) // tq)
    _, l_w, acc_w = lax.fori_loop(0, n_back + 1, win_body, _flash_init((grp, tq)))

    gates = jax.nn.sigmoid(sm_ref[0])
    nz = nz_ref[0]
    for g in range(grp):
        head = kvh * grp + g
        w_cmp = _lane_pick(gates, head)
        w_sel = _lane_pick(gates, n_heads + head) / jnp.maximum(l_s[g], TINY)
        w_win = _lane_pick(gates, 2 * n_heads + head) / jnp.maximum(l_w[g], TINY)
        o = w_cmp * o_cmp[g] + (w_sel * acc_s[g] + w_win * acc_w[g])
        sl = slice(g * HEAD_DIM, (g + 1) * HEAD_DIM)
        o_ref[0, :, sl] = (o * _silu(nz[:, sl])).astype(o_ref.dtype)


def _nsa_prompt(proj3, small3, qr3, ksr3, kwr3, pe, wphi, mo, nsa_w):
    b, seq, _ = proj3.shape
    n_heads = nsa_w // HEAD_DIM
    kvh = n_heads // NSA_GROUP
    gw = NSA_GROUP * HEAD_DIM
    tq = _tile(seq, 512)
    tk = _tile(seq, 512)
    assert seq % (2 * CMP_BLOCK) == 0 and tk % SEL_BLOCK == 0
    assert N_SELECT >= 3, "the forced blocks (first, current, previous) must always fit in the selection"
    assert seq // SEL_BLOCK <= LANES
    hb = lambda name: mo[name] // HEAD_DIM

    def head_spec(off_blocks):
        return pl.BlockSpec((1, seq, HEAD_DIM), lambda bi, k, qi: (bi, 0, off_blocks + k))

    return pl.pallas_call(
        functools.partial(_nsa_prompt_kernel, seq=seq, tq=tq, tk=tk, n_heads=n_heads),
        grid=(b, kvh, seq // tq),
        in_specs=[pl.BlockSpec((1, tq, gw), lambda bi, k, qi: (bi, qi, mo['nsa_q'] // gw + k)),
                  pl.BlockSpec((1, tq, gw), lambda bi, k, qi: (bi, qi, k)),
                  head_spec(hb('kv_cmp')), head_spec(hb('kv_cmp') + kvh),
                  head_spec(0), head_spec(kvh), head_spec(0), head_spec(kvh),
                  pl.BlockSpec((1, tq, gw), lambda bi, k, qi: (bi, qi, mo['nsa_z'] // gw + k)),
                  pl.BlockSpec((1, tq, LANES), lambda bi, k, qi: (bi, qi, 0)),
                  pl.BlockSpec((2, CMP_BLOCK, HEAD_DIM), lambda bi, k, qi: (0, 0, 0)),
                  pl.BlockSpec((2, HEAD_DIM, HEAD_DIM), lambda bi, k, qi: (0, 0, 0))],
        out_specs=pl.BlockSpec((1, tq, gw), lambda bi, k, qi: (bi, qi, k)),
        out_shape=jax.ShapeDtypeStruct((b, seq, nsa_w), BF16),
        scratch_shapes=[pltpu.VMEM((seq // CMP_BLOCK, HEAD_DIM), F32),
                        pltpu.VMEM((seq // CMP_BLOCK, HEAD_DIM), F32)],
        compiler_params=_cparams(("arbitrary", "arbitrary", "arbitrary")),
        name="nsa_prompt",
    )(proj3, qr3, proj3, proj3, ksr3, ksr3, kwr3, kwr3, proj3, small3, pe, wphi)


def _summ_kernel(pt_ref, *refs, n_pg, n_ch):
    del pt_ref
    x_refs, pe_ref, o_ref = refs[:n_pg], refs[n_pg], refs[n_pg + 1]
    pe = pe_ref[...][None]
    outs = []
    for r in x_refs:
        page = r[0, 0]
        n_blk = page.shape[0] // (CMP_BLOCK * n_ch)
        x = page.reshape(n_blk, CMP_BLOCK, n_ch, HEAD_DIM)
        outs.append(jnp.mean(x + pe, axis=1).reshape(n_blk * n_ch, HEAD_DIM))
    o_ref[0] = jnp.concatenate(outs, axis=0)


def _summaries(cache4, layer, pt_flat, pe_ch, n_batch, n_pages, n_ch):
    page_rows = cache4.shape[2]
    blk_per_page = page_rows // (CMP_BLOCK * n_ch)
    n_pg = _tile(n_pages, 8)
    rows = n_pg * blk_per_page * n_ch

    def page_spec(k):
        return pl.BlockSpec((1, 1, page_rows, HEAD_DIM),
                            lambda bi, j, pt: (layer, pt[bi * n_pages + j * n_pg + k], 0, 0))

    return pl.pallas_call(
        functools.partial(_summ_kernel, n_pg=n_pg, n_ch=n_ch),
        grid_spec=pltpu.PrefetchScalarGridSpec(
            num_scalar_prefetch=1,
            grid=(n_batch, n_pages // n_pg),
            in_specs=[page_spec(k) for k in range(n_pg)]
            + [pl.BlockSpec((CMP_BLOCK, n_ch, HEAD_DIM), lambda bi, j, pt: (0, 0, 0))],
            out_specs=pl.BlockSpec((1, rows, HEAD_DIM), lambda bi, j, pt: (bi, j, 0))),
        out_shape=jax.ShapeDtypeStruct((n_batch, n_pages * blk_per_page * n_ch, HEAD_DIM), F32),
        compiler_params=_cparams(("arbitrary", "arbitrary")),
        name="cmp_summaries",
    )(pt_flat, *([cache4] * n_pg), pe_ch)


def _cmpsel_kernel(summ_ref, q_ref, wphi_ref, ocmp_ref, sel_ref, *, past, t_new, kvh_n, n_selp):
    grp = NSA_GROUP
    scale = HEAD_DIM ** -0.5
    n_ch = 2 * kvh_n
    n_half = summ_ref.shape[1] // (2 * n_ch)
    n_sel = n_half + 1

    def summaries(ch, odd):
        return summ_ref[0, pl.ds(odd * n_ch + ch, n_half, stride=2 * n_ch), :]
    tpos = lax.broadcasted_iota(jnp.int32, (t_new, 1), 0) + past
    pos = jnp.concatenate([tpos] * grp, axis=0)
    lane = lax.broadcasted_iota(jnp.int32, (1, n_half), 1)
    ok_e = (2 * lane + 1) * CMP_BLOCK - 1 <= pos
    ok_o = (2 * lane + 2) * CMP_BLOCK - 1 <= pos
    w_k, w_v = wphi_ref[0].astype(BF16), wphi_ref[1].astype(BF16)
    q = q_ref[0]
    scores = []
    for k in range(kvh_n):
        proj = lambda ch, odd, w: jnp.dot(summaries(ch, odd).astype(BF16), w,
                                          preferred_element_type=F32).astype(BF16)
        k_e, k_o = proj(k, 0, w_k), proj(k, 1, w_k)
        v_e, v_o = proj(kvh_n + k, 0, w_v), proj(kvh_n + k, 1, w_v)
        qk = jnp.concatenate(
            [q[:, (k * grp + g) * HEAD_DIM:(k * grp + g + 1) * HEAD_DIM] for g in range(grp)], axis=0).astype(BF16)
        s_e = jnp.where(ok_e, _nt(qk, k_e) * scale, NEG_BIG)
        s_o = jnp.where(ok_o, _nt(qk, k_o) * scale, NEG_BIG)
        m = jnp.maximum(jnp.max(s_e, axis=-1, keepdims=True), jnp.max(s_o, axis=-1, keepdims=True))
        p_e = jnp.where(ok_e, jnp.exp(s_e - m), 0.0)
        p_o = jnp.where(ok_o, jnp.exp(s_o - m), 0.0)
        den = jnp.maximum(jnp.sum(p_e, axis=-1, keepdims=True) + jnp.sum(p_o, axis=-1, keepdims=True), TINY)
        p_e, p_o = p_e / den, p_o / den
        o = (jnp.dot(p_e.astype(BF16), v_e, preferred_element_type=F32)
             + jnp.dot(p_o.astype(BF16), v_o, preferred_element_type=F32))
        imp = jnp.zeros((t_new, n_half), F32)
        for g in range(grp):
            hsl = slice((k * grp + g) * HEAD_DIM, (k * grp + g + 1) * HEAD_DIM)
            ocmp_ref[0, :, hsl] = o[g * t_new:(g + 1) * t_new]
            imp = imp + (p_e[g * t_new:(g + 1) * t_new] + p_o[g * t_new:(g + 1) * t_new])
        tail_lane = lax.broadcasted_iota(jnp.int32, (t_new, n_selp - n_half), 1)
        tail = jnp.where(tail_lane == 0, 0.0, -jnp.inf)
        scores.append(jnp.concatenate([imp, tail], axis=1))
    score = jnp.concatenate(scores, axis=0)
    pos_r = jnp.concatenate([tpos] * kvh_n, axis=0)
    blk = lax.broadcasted_iota(jnp.int32, (1, n_selp), 1)
    q_blk = pos_r // SEL_BLOCK
    forced = (blk == 0) | (blk == q_blk) | (blk == q_blk - 1)
    score = jnp.where(forced, score + FORCE_BONUS, score)
    score = jnp.where((blk * SEL_BLOCK > pos_r) | (blk >= n_sel), -jnp.inf, score)
    rank = _topk_rank(score, blk, n_sel)

    n_slot = sel_ref.shape[3]
    n_top = min(N_SELECT, n_sel)
    blk_f = blk.astype(F32)
    blk_hi = jnp.floor(blk_f * (1.0 / 16.0))
    blk_lo = blk_f - 16.0 * blk_hi
    before = jnp.where(lax.broadcasted_iota(jnp.int32, (n_selp, 1), 0) < blk, 1.0, 0.0).astype(BF16)
    slot = lax.broadcasted_iota(jnp.int32, (n_slot, 1), 0).astype(F32)
    for k in range(kvh_n):
        member = jnp.where(rank[k * t_new:(k + 1) * t_new] < n_top, 1.0, 0.0)
        member_past = jnp.where(blk < n_half, member, 0.0)
        used = jnp.max(member_past, axis=0, keepdims=True)
        n_before = jnp.dot(jnp.broadcast_to(used, (SUBLANES, n_selp)).astype(BF16), before,
                           preferred_element_type=F32)[0:1]
        place = jnp.where(used > 0.5, jnp.where(n_before == slot, 1.0, 0.0), 0.0).astype(BF16)
        rows = jnp.concatenate([blk_hi, blk_lo, used, jnp.zeros((SUBLANES - 3, n_selp), F32), member_past], axis=0)
        packed = _nt(rows.astype(BF16), place)
        has_new = jnp.broadcast_to(member[:, n_half:n_half + 1], (t_new, n_slot))
        sel_ref[0, k] = jnp.concatenate([packed, has_new], axis=0)


def _cmpsel(summ, proj3, wphi, mo, nsa_w, past):
    b, summ_rows, _ = summ.shape
    t_new = proj3.shape[1]
    kvh_n = nsa_w // HEAD_DIM // NSA_GROUP
    n_half = summ_rows // (4 * kvh_n)
    assert n_half == past // SEL_BLOCK
    n_selp = -(-(n_half + 1) // LANES) * LANES
    n_slot = t_new * min(N_SELECT, n_half + 1)
    sel_rows = SUBLANES + 2 * t_new
    assert t_new == SUBLANES
    return pl.pallas_call(
        functools.partial(_cmpsel_kernel, past=past, t_new=t_new, kvh_n=kvh_n, n_selp=n_selp),
        grid=(b,),
        in_specs=[pl.BlockSpec((1, summ_rows, HEAD_DIM), lambda bi: (bi, 0, 0)),
                  pl.BlockSpec((1, t_new, nsa_w), lambda bi: (bi, 0, mo['nsa_q'] // nsa_w)),
                  pl.BlockSpec((2, HEAD_DIM, HEAD_DIM), lambda bi: (0, 0, 0))],
        out_specs=[pl.BlockSpec((1, t_new, nsa_w), lambda bi: (bi, 0, 0)),
                   pl.BlockSpec((1, kvh_n, sel_rows, n_slot), lambda bi: (bi, 0, 0, 0))],
        out_shape=[jax.ShapeDtypeStruct((b, t_new, nsa_w), F32),
                   jax.ShapeDtypeStruct((b, kvh_n, sel_rows, n_slot), F32)],
        compiler_params=_cparams(("arbitrary",)),
        name="cmp_select",
    )(summ, proj3, wphi)


def _two_part_attention(parts):
    m = None
    for s, msk, _ in parts:
        mx = jnp.max(jnp.where(msk, s, NEG_BIG), axis=-1, keepdims=True)
        m = mx if m is None else jnp.maximum(m, mx)
    den, acc = 0.0, 0.0
    for s, msk, v in parts:
        p = jnp.where(msk, jnp.exp(jnp.where(msk, s, NEG_BIG) - m), 0.0)
        den = den + jnp.sum(p, axis=-1, keepdims=True)
        acc = acc + jnp.dot(p.astype(BF16), v, preferred_element_type=F32)
    return acc / jnp.maximum(den, TINY)


SLOT_CHUNK = 16


def _selwin_kernel(idx_ref, cnt_ref, pt_ref, cache_ref, qr_ref, ksn_ref, vsn_ref, wbuf_ref, kwn_ref, vwn_ref,
                   sm_ref, ocmp_ref, nz_ref, sel_ref, o_ref, kbuf, vbuf, sem,
                   *, layer, past, t_new, kvh_n, n_pages, page, n_heads):
    bi, kvh = pl.program_id(0), pl.program_id(1)
    grp = NSA_GROUP
    n_ch = 2 * kvh_n
    n_slot = kbuf.shape[0]
    n_past_blk = past // SEL_BLOCK
    per_page = page // SEL_BLOCK
    base = (bi * kvh_n + kvh) * n_slot
    n_chunk = (cnt_ref[bi * kvh_n + kvh] + SLOT_CHUNK - 1) // SLOT_CHUNK
    n_issue = n_chunk * SLOT_CHUNK

    def copies(i, row0):
        return (pltpu.make_async_copy(cache_ref.at[layer, pl.ds(row0, SEL_BLOCK), kvh, :], kbuf.at[i], sem.at[0]),
                pltpu.make_async_copy(cache_ref.at[layer, pl.ds(row0, SEL_BLOCK), kvh_n + kvh, :], vbuf.at[i],
                                      sem.at[1]))

    def issue(i, carry):
        idc = jnp.minimum(idx_ref[base + i], n_past_blk - 1)
        phys = pt_ref[bi * n_pages + idc // per_page]
        row0 = pl.multiple_of(phys * page + (idc % per_page) * SEL_BLOCK, SEL_BLOCK)
        for cp in copies(i, row0):
            cp.start()
        return carry

    def wait(i, carry):
        for cp in copies(i, 0):
            cp.wait()
        return carry

    lax.fori_loop(0, n_issue, issue, 0)
    lax.fori_loop(0, n_issue, wait, 0)

    qr = qr_ref[0]
    q4 = jnp.concatenate([qr[:, g * HEAD_DIM:(g + 1) * HEAD_DIM] for g in range(grp)], axis=0)
    rows = grp * t_new
    row_t = jnp.concatenate([lax.broadcasted_iota(jnp.int32, (t_new, 1), 0)] * grp, axis=0)
    pos = row_t + past

    def padded(ref):
        x = ref[0]
        return jnp.concatenate([x, jnp.zeros((SEL_BLOCK - t_new, HEAD_DIM), x.dtype)], axis=0).astype(BF16)

    new_j = lax.broadcasted_iota(jnp.int32, (1, SEL_BLOCK), 1)
    new_pos = past + new_j
    new_real = new_j < t_new

    flags = sel_ref[0, 0]
    uses_slot = jnp.concatenate([flags[SUBLANES:SUBLANES + t_new]] * grp, axis=0).astype(BF16)
    has_new = jnp.concatenate([flags[SUBLANES + t_new:SUBLANES + 2 * t_new, 0:1]] * grp, axis=0)
    width = SLOT_CHUNK * SEL_BLOCK

    def sel_body(c, carry):
        s0 = pl.multiple_of(c * SLOT_CHUNK, SLOT_CHUNK)
        k_c = kbuf[pl.ds(s0, SLOT_CHUNK)].reshape(width, HEAD_DIM).astype(BF16)
        v_c = vbuf[pl.ds(s0, SLOT_CHUNK)].reshape(width, HEAD_DIM).astype(BF16)
        slot_of_key = s0 + lax.broadcasted_iota(jnp.int32, (1, width), 1) // SEL_BLOCK
        expand = jnp.where(slot_of_key == lax.broadcasted_iota(jnp.int32, (n_slot, 1), 0), 1.0, 0.0).astype(BF16)
        bias = jnp.where(jnp.dot(uses_slot, expand, preferred_element_type=F32) > 0.5, 0.0, NEG_BIG)
        return _flash_step(carry, _nt(q4, k_c)[None], bias, v_c)

    carry = lax.fori_loop(0, n_chunk, sel_body, _flash_init((1, rows)))
    bias_new = jnp.where(jnp.where(new_real & (new_pos <= pos), has_new, 0.0) > 0.5, 0.0, NEG_BIG)
    _, l_sel, acc_sel = _flash_step(carry, _nt(q4, padded(ksn_ref))[None], bias_new, padded(vsn_ref))
    o_sel = acc_sel[0] / jnp.maximum(l_sel[0], TINY)

    n_buf = wbuf_ref.shape[1] // n_ch
    kwb = wbuf_ref[0, pl.ds(kvh, n_buf, stride=n_ch), :].astype(BF16)
    vwb = wbuf_ref[0, pl.ds(kvh_n + kvh, n_buf, stride=n_ch), :].astype(BF16)
    buf_pos = past - n_buf + lax.broadcasted_iota(jnp.int32, (1, n_buf), 1)
    msk_buf = jnp.where(buf_pos <= pos, pos - buf_pos, WINDOW) < WINDOW
    msk_wn = jnp.where(new_real & (new_pos <= pos), pos - new_pos, WINDOW) < WINDOW
    o_win = _two_part_attention([(_nt(q4, kwb), msk_buf, vwb),
                                 (_nt(q4, padded(kwn_ref)), msk_wn, padded(vwn_ref))])

    gates = jax.nn.sigmoid(sm_ref[0])
    ocmp, nz = ocmp_ref[0], nz_ref[0]
    for g in range(grp):
        head = kvh * grp + g
        rs = slice(g * t_new, (g + 1) * t_new)
        sl = slice(g * HEAD_DIM, (g + 1) * HEAD_DIM)
        o = (_lane_pick(gates, head) * ocmp[:, sl]
             + (_lane_pick(gates, n_heads + head) * o_sel[rs] + _lane_pick(gates, 2 * n_heads + head) * o_win[rs]))
        o_ref[0, :, sl] = (o * _silu(nz[:, sl])).astype(o_ref.dtype)


def _selwin(sel, pt_flat, cache4, layer, qr3, ksr3, win_state3, kwr3, small3, ocmp3, proj3, mo, nsa_w,
            past, n_pages, page):
    b, t_new, _ = qr3.shape
    n_heads = nsa_w // HEAD_DIM
    kvh_n = n_heads // NSA_GROUP
    gw = NSA_GROUP * HEAD_DIM
    sel_rows, n_slot = sel.shape[2], sel.shape[3]
    assert n_slot % SLOT_CHUNK == 0
    slot_blocks = (16.0 * sel[:, :, 0] + sel[:, :, 1]).astype(jnp.int32).reshape(-1)
    slot_count = jnp.sum(sel[:, :, 2], axis=-1).astype(jnp.int32).reshape(-1)
    win_rows = win_state3.shape[1]
    new_k = pl.BlockSpec((1, t_new, HEAD_DIM), lambda bi, k, *_: (bi, 0, k))
    new_v = pl.BlockSpec((1, t_new, HEAD_DIM), lambda bi, k, *_: (bi, 0, kvh_n + k))
    return pl.pallas_call(
        functools.partial(_selwin_kernel, layer=layer, past=past, t_new=t_new, kvh_n=kvh_n, n_pages=n_pages,
                          page=page, n_heads=n_heads),
        grid_spec=pltpu.PrefetchScalarGridSpec(
            num_scalar_prefetch=3,
            grid=(b, kvh_n),
            in_specs=[pl.BlockSpec(memory_space=pl.ANY),
                      pl.BlockSpec((1, t_new, gw), lambda bi, k, *_: (bi, 0, k)),
                      new_k, new_v,
                      pl.BlockSpec((1, win_rows, HEAD_DIM), lambda bi, k, *_: (layer * b + bi, 0, 0)),
                      new_k, new_v,
                      pl.BlockSpec((1, t_new, LANES), lambda bi, k, *_: (bi, 0, 0)),
                      pl.BlockSpec((1, t_new, gw), lambda bi, k, *_: (bi, 0, k)),
                      pl.BlockSpec((1, t_new, gw), lambda bi, k, *_: (bi, 0, mo['nsa_z'] // gw + k)),
                      pl.BlockSpec((1, 1, sel_rows, n_slot), lambda bi, k, *_: (bi, k, 0, 0))],
            out_specs=pl.BlockSpec((1, t_new, gw), lambda bi, k, *_: (bi, 0, k)),
            scratch_shapes=[pltpu.VMEM((n_slot, SEL_BLOCK, HEAD_DIM), F32),
                            pltpu.VMEM((n_slot, SEL_BLOCK, HEAD_DIM), F32),
                            pltpu.SemaphoreType.DMA((2,))]),
        out_shape=jax.ShapeDtypeStruct((b, t_new, nsa_w), BF16),
        compiler_params=_cparams(("arbitrary", "arbitrary")),
        name="sel_win_sample",
    )(slot_blocks, slot_count, pt_flat, cache4, qr3, ksr3, ksr3, win_state3, kwr3, kwr3, small3, ocmp3, proj3, sel)


def _conv_kernel(x_ref, b_ref, c_ref, z_ref, buf_ref, w_ref, o_ref, st_ref, prev):
    tc = x_ref.shape[1]

    @pl.when(pl.program_id(1) == 0)
    def _():
        prev[0:CONV_K - 1, :] = buf_ref[0]

    u = c_ref[0] * x_ref[0]
    row = lax.broadcasted_iota(jnp.int32, (tc, 1), 0)
    w = w_ref[...]
    y = u * w[CONV_K - 1:CONV_K, :]
    for back in range(1, CONV_K):
        shifted = pltpu.roll(u, back, axis=0)
        for r in range(back):
            shifted = jnp.where(row == r, prev[CONV_K - 1 - back + r:CONV_K - back + r, :], shifted)
        y = y + shifted * w[CONV_K - 1 - back:CONV_K - back, :]
    o_ref[0] = (b_ref[0] * y * _silu(z_ref[0])).astype(o_ref.dtype)
    tail = u[tc - (CONV_K - 1):, :]
    prev[0:CONV_K - 1, :] = tail
    st_ref[0] = tail


def _conv(proj3, buf, w, mo, conv_w):
    b, seq, _ = proj3.shape
    tc = _tile(seq, 512)
    assert tc >= CONV_K - 1
    col = lambda name: pl.BlockSpec((1, tc, conv_w), lambda bi, ti: (bi, ti, mo[name] // conv_w))
    return pl.pallas_call(
        _conv_kernel,
        grid=(b, seq // tc),
        in_specs=[col('conv_x'), col('conv_b'), col('conv_c'), col('conv_z'),
                  pl.BlockSpec((1, CONV_K - 1, conv_w), lambda bi, ti: (bi, 0, 0)),
                  pl.BlockSpec((CONV_K, conv_w), lambda bi, ti: (0, 0))],
        out_specs=[pl.BlockSpec((1, tc, conv_w), lambda bi, ti: (bi, ti, 0)),
                   pl.BlockSpec((1, CONV_K - 1, conv_w), lambda bi, ti: (bi, 0, 0))],
        out_shape=[jax.ShapeDtypeStruct((b, seq, conv_w), BF16),
                   jax.ShapeDtypeStruct((b, CONV_K - 1, conv_w), F32)],
        scratch_shapes=[pltpu.VMEM((8, conv_w), F32)],
        compiler_params=_cparams(("arbitrary", "arbitrary")),
        name="short_conv",
    )(proj3, proj3, proj3, proj3, buf, w)


def _log_sigmoid(x):
    return jnp.minimum(x, 0.0) - jnp.log(1.0 + jnp.exp(-jnp.abs(x)))


def _mlstm_kernel(q_ref, k_ref, v_ref, og_ref, z_ref, sm_ref, gr_ref, c0_ref, n0_ref, m0_ref, nw_ref,
                  o_ref, c_out, n_out, m_out, c_s, n_s, m_s, *, n_chunks, gate_i, gate_f):
    ci = pl.program_id(1)

    @pl.when(ci == 0)
    def _():
        c_s[...] = c0_ref[0]
        n_s[...] = n0_ref[0]
        m_s[...] = m0_ref[0]

    small, gr = sm_ref[0], gr_ref[0]
    dk = c_s.shape[1]
    for h in range(MLSTM_HEADS):
        hs = slice(h * dk, (h + 1) * dk)
        c_new, n_new, m_new, out = _mlstm_head(
            q_ref[0, :, hs], k_ref[0, :, hs], v_ref[0, :, hs], og_ref[0, :, hs], z_ref[0, :, hs],
            small[:, gate_i + h:gate_i + h + 1], small[:, gate_f + h:gate_f + h + 1],
            gr[h:h + 1, :], gr[MLSTM_HEADS + h:MLSTM_HEADS + h + 1, :],
            c_s[h], n_s[h], m_s[h], nw_ref[h])
        c_s[h] = c_new
        n_s[h] = n_new
        m_s[h] = m_new
        o_ref[0, :, hs] = out.astype(o_ref.dtype)

    @pl.when(ci == n_chunks - 1)
    def _():
        c_out[0] = c_s[...]
        n_out[0] = n_s[...]
        m_out[0] = m_s[...]


def _mlstm_head(q, k, v, og, z, i_col, f_col, i_row, f_row, c_prev, n_prev, m_prev, norm_w):
    lc, dk = q.shape
    lf_col, lf_row = _log_sigmoid(f_col), _log_sigmoid(f_row)
    ti = lax.broadcasted_iota(jnp.int32, (lc, 1), 0)
    si = lax.broadcasted_iota(jnp.int32, (1, lc), 1)
    causal = si <= ti
    b_col = jnp.sum(jnp.where(causal, lf_row, 0.0), axis=1, keepdims=True)
    b_row = jnp.sum(jnp.where(ti <= si, lf_col, 0.0), axis=0, keepdims=True)
    log_d = jnp.where(causal, b_col - b_row + i_row, -jnp.inf)
    m_t = jnp.maximum(b_col + m_prev, jnp.max(log_d, axis=1, keepdims=True))
    inter = jnp.exp(b_col + m_prev - m_t)
    kf = k * (dk ** -0.5)
    qb, kb, vb = q.astype(BF16), kf.astype(BF16), v.astype(BF16)
    s = _nt(qb, kb) * jnp.exp(log_d - m_t)
    num = (inter * jnp.dot(qb, c_prev.astype(BF16), preferred_element_type=F32)
           + jnp.dot(s.astype(BF16), vb, preferred_element_type=F32))
    den = inter * jnp.sum(q * n_prev, axis=1, keepdims=True) + jnp.sum(s, axis=1, keepdims=True)
    hid = num / jnp.maximum(jnp.abs(den), jnp.exp(-m_t))

    m_new = m_t[lc - 1:lc, :]
    b_last = b_col[lc - 1:lc, :]
    wdec = jnp.exp(b_last - b_col + i_col - m_new)
    cdec = jnp.exp(b_last + m_prev - m_new)
    c_new = cdec * c_prev + lax.dot_general(kb, (wdec * v).astype(BF16), (((0,), (0,)), ((), ())),
                                            preferred_element_type=F32)
    n_new = cdec * n_prev + jnp.sum(wdec * kf, axis=0, keepdims=True)
    hn = hid * lax.rsqrt(jnp.mean(hid * hid, axis=-1, keepdims=True) + RMS_EPS) * norm_w
    return c_new, n_new, m_new, jax.nn.sigmoid(og) * hn * _silu(z)


def _mlstm(proj3, small3, gates_row, c0, n0, m0, norm_w, mo, ml_w, gate_i, gate_f):
    b, seq, _ = proj3.shape
    nh = MLSTM_HEADS
    dk = ml_w // nh
    lc = _tile(seq, 256)
    n_chunks = seq // lc
    col = lambda name: pl.BlockSpec((1, lc, ml_w), lambda bi, ci: (bi, ci, mo[name] // ml_w))
    state = lambda shape: pl.BlockSpec((1, nh) + shape, lambda bi, ci: (bi, 0, 0, 0))
    return pl.pallas_call(
        functools.partial(_mlstm_kernel, n_chunks=n_chunks, gate_i=gate_i, gate_f=gate_f),
        grid=(b, n_chunks),
        in_specs=[col('m_q'), col('m_k'), col('m_v'), col('m_o'), col('m_z'),
                  pl.BlockSpec((1, lc, LANES), lambda bi, ci: (bi, ci, 0)),
                  pl.BlockSpec((1, 2 * nh, lc), lambda bi, ci: (bi, 0, ci)),
                  state((dk, dk)), state((1, dk)), state((1, 1)),
                  pl.BlockSpec((nh, 1, dk), lambda bi, ci: (0, 0, 0))],
        out_specs=[pl.BlockSpec((1, lc, ml_w), lambda bi, ci: (bi, ci, 0)),
                   state((dk, dk)), state((1, dk)), state((1, 1))],
        out_shape=[jax.ShapeDtypeStruct((b, seq, ml_w), BF16),
                   jax.ShapeDtypeStruct((b, nh, dk, dk), F32),
                   jax.ShapeDtypeStruct((b, nh, 1, dk), F32),
                   jax.ShapeDtypeStruct((b, nh, 1, 1), F32)],
        scratch_shapes=[pltpu.VMEM((nh, dk, dk), F32), pltpu.VMEM((nh, 1, dk), F32), pltpu.VMEM((nh, 1, 1), F32)],
        compiler_params=_cparams(("arbitrary", "arbitrary")),
        name="mlstm",
    )(proj3, proj3, proj3, proj3, proj3, small3, gates_row, c0, n0, m0, norm_w)


def _outproj_kernel(x_ref, g_ref, a_ref, c_ref, m_ref, w_ref, o_ref):
    ka, kc = a_ref.shape[1], c_ref.shape[1]
    acc = jnp.dot(a_ref[...], w_ref[0, 0:ka, :], preferred_element_type=F32)
    acc = acc + jnp.dot(c_ref[...], w_ref[0, ka:ka + kc, :], preferred_element_type=F32)
    acc = acc + jnp.dot(m_ref[...], w_ref[0, ka + kc:, :], preferred_element_type=F32)
    o_ref[...] = x_ref[...] + g_ref[0] * acc


def _outproj(x2, gate3, o_nsa, o_conv, o_ml, w_out, layer, tm, rows_per_mod):
    m, d = x2.shape
    tn = _tile(d, 1024)
    r = gate3.shape[1]
    kdim = w_out.shape[1]
    lhs = lambda a: pl.BlockSpec((tm, a.shape[1]), lambda i, j: (i, 0))
    return pl.pallas_call(
        _outproj_kernel,
        grid=(m // tm, d // tn),
        in_specs=[pl.BlockSpec((tm, tn), lambda i, j: (i, j)),
                  pl.BlockSpec((1, r, tn), lambda i, j: (i // rows_per_mod, 0, j)),
                  lhs(o_nsa), lhs(o_conv), lhs(o_ml),
                  pl.BlockSpec((1, kdim, tn), lambda i, j: (layer, 0, j))],
        out_specs=pl.BlockSpec((tm, tn), lambda i, j: (i, j)),
        out_shape=jax.ShapeDtypeStruct((m, d), F32),
        compiler_params=_cparams(("arbitrary", "arbitrary")),
        name="outproj",
    )(x2, gate3, o_nsa, o_conv, o_ml, w_out)


def _rmsnorm_kernel(x_ref, w_ref, o_ref):
    x = x_ref[...]
    o_ref[...] = x * lax.rsqrt(jnp.mean(x * x, axis=-1, keepdims=True) + RMS_EPS) * w_ref[...]


def _rmsnorm(x2, w):
    m, d = x2.shape
    tm = _tile(m, 256)
    return pl.pallas_call(
        _rmsnorm_kernel,
        grid=(m // tm,),
        in_specs=[pl.BlockSpec((tm, d), lambda i: (i, 0)), pl.BlockSpec((1, d), lambda i: (0, 0))],
        out_specs=pl.BlockSpec((tm, d), lambda i: (i, 0)),
        out_shape=jax.ShapeDtypeStruct((m, d), F32),
        compiler_params=_cparams(("arbitrary",)),
        name="final_norm",
    )(x2, w.reshape(1, d))


def _layer(x3, mod, lw, cfg, past, states):
    b, seq, d = x3.shape
    m = b * seq
    mo, nsa_w, kv_w, conv_w, ml_w = cfg['mo'], cfg['nsa_w'], cfg['kv_w'], cfg['conv_w'], cfg['ml_w']
    shift, scale, gate = mod
    x2 = x3.reshape(m, d)
    if seq >= 512:
        tm, rows_per_mod = 512, seq // 512
        tm_out = _tile(seq, 1024)
        per = lambda a: a.reshape(b, 1, d)
    else:
        tm, rows_per_mod = m, 1
        tm_out = m
        per = lambda a: jnp.repeat(a, seq, axis=0).reshape(1, m, d)
    h2 = _modnorm(x2, per(scale), per(shift), lw['norm_w'], tm, rows_per_mod)
    proj, small = _inproj(h2, lw['w_main'], lw['b_main'], lw['w_small'], lw['b_small'], lw['layer'])
    nm = proj.shape[1]
    proj3, small3 = proj.reshape(b, seq, nm), small.reshape(b, seq, LANES)

    pos = past + jnp.arange(seq, dtype=jnp.int32)
    cos, sin = _rope_tables(pos)
    qr, ksr, kwr = _rope(proj, jnp.tile(cos, (b, 1)), jnp.tile(sin, (b, 1)), mo, nsa_w, kv_w)
    qr3, ksr3, kwr3 = qr.reshape(b, seq, nsa_w), ksr.reshape(b, seq, kv_w), kwr.reshape(b, seq, kv_w)

    if states is None:
        o_nsa = _nsa_prompt(proj3, small3, qr3, ksr3, kwr3, lw['cmp_pe'], lw['cmp_w'], mo, nsa_w)
        conv_buf = jnp.zeros((b, CONV_K - 1, conv_w), F32)
        dk = ml_w // MLSTM_HEADS
        c0 = jnp.zeros((b, MLSTM_HEADS, dk, dk), F32)
        n0 = jnp.zeros((b, MLSTM_HEADS, 1, dk), F32)
        m0 = jnp.zeros((b, MLSTM_HEADS, 1, 1), F32)
    else:
        n_pages, page = states['n_pages'], states['page']
        summ = _summaries(states['cache_cmp'], states['layer'], states['pt_flat'], lw['pe_ch'], b, n_pages,
                          kv_w // HEAD_DIM)
        o_cmp, sel = _cmpsel(summ, proj3, lw['cmp_w'], mo, nsa_w, past)
        o_nsa = _selwin(sel, states['pt_flat'], states['cache_sel'], states['layer'], qr3, ksr3,
                        states['win'], kwr3, small3, o_cmp, proj3, mo, nsa_w, past, n_pages, page)
        conv_buf = states['conv']
        dk = ml_w // MLSTM_HEADS
        c0 = states['C']
        n0 = states['n'].reshape(b, MLSTM_HEADS, 1, dk)
        m0 = states['m'].reshape(b, MLSTM_HEADS, 1, 1)

    o_conv, new_conv = _conv(proj3, conv_buf, lw['conv_w'], mo, conv_w)
    gates_row = jnp.swapaxes(small3[:, :, cfg['gate_i']:cfg['gate_i'] + 2 * MLSTM_HEADS], 1, 2)
    o_ml, c1, n1, m1 = _mlstm(proj3, small3, gates_row, c0, n0, m0, lw['mlstm_norm_w'], mo, ml_w,
                              cfg['gate_i'], cfg['gate_f'])

    y = _outproj(x2, per(gate), o_nsa.reshape(m, nsa_w), o_conv.reshape(m, conv_w), o_ml.reshape(m, ml_w),
                 lw['w_out'], lw['layer'], tm_out, max(seq // tm_out, 1))
    kv_shape = (b, seq, 2, kv_w // (2 * HEAD_DIM), HEAD_DIM)
    new_cmp = proj3[:, :, mo['kv_cmp']:mo['kv_cmp'] + kv_w].reshape(kv_shape)
    dk = ml_w // MLSTM_HEADS
    st = (new_cmp, ksr3.reshape(kv_shape), kwr3.reshape(kv_shape), new_conv, c1,
          n1.reshape(b, MLSTM_HEADS, dk), m1.reshape(b, MLSTM_HEADS))
    return y.reshape(b, seq, d), st


def kernel(x_prompt, x_sample, cache_cmp_kv, cache_sel_kv, state_win_kv, state_conv, state_mlstm_C, state_mlstm_n,
           state_mlstm_m, page_table, c_prompt, c_sample, norm_w, w_ada, b_ada, w_in, b_in, cmp_pe, cmp_w, conv_w,
           mlstm_norm_w, w_out, final_norm_w):
    bp, seq, d = x_prompt.shape
    bs, t_new, _ = x_sample.shape
    depth = w_in.shape[0]
    layout = _in_layout(d)
    sizes = dict(layout)
    in_off, _ = _offsets([n for n, _ in layout], sizes)
    mo, n_main = _offsets(MAIN_NAMES, sizes)
    so, n_small = _offsets(SMALL_NAMES, sizes)
    assert n_small <= LANES
    nsa_w, kv_w, conv_wd, ml_w = sizes['nsa_q'], sizes['kv_cmp'], sizes['conv_x'], sizes['m_q']
    cfg = dict(mo=mo, nsa_w=nsa_w, kv_w=kv_w, conv_w=conv_wd, ml_w=ml_w, gate_i=so['m_i'], gate_f=so['m_f'])
    n_kv = kv_w // (2 * HEAD_DIM)

    def gather_cols(a, names, pad_to):
        parts = [a[..., in_off[n]:in_off[n] + sizes[n]] for n in names]
        width = sum(sizes[n] for n in names)
        if pad_to > width:
            parts.append(jnp.zeros(a.shape[:-1] + (pad_to - width,), a.dtype))
        return jnp.concatenate(parts, axis=-1)

    segs_main = tuple((in_off[n], sizes[n], mo[n]) for n in MAIN_NAMES)
    w_main_all = _wprep(w_in, segs_main, n_main)
    w_small_all = gather_cols(w_in, SMALL_NAMES, LANES).astype(BF16)
    w_out_bf = w_out.astype(BF16)

    layers = []
    for l in range(depth):
        pe_ch = jnp.repeat(jnp.transpose(cmp_pe[l], (1, 0, 2)), n_kv, axis=1)
        layers.append(dict(
            layer=l, norm_w=norm_w[l].reshape(1, d),
            w_main=w_main_all,
            b_main=gather_cols(b_in[l], MAIN_NAMES, n_main).reshape(1, n_main),
            w_small=w_small_all,
            b_small=gather_cols(b_in[l], SMALL_NAMES, LANES).reshape(1, LANES),
            cmp_pe=cmp_pe[l], cmp_w=cmp_w[l], pe_ch=pe_ch, conv_w=conv_w[l],
            mlstm_norm_w=mlstm_norm_w[l].reshape(MLSTM_HEADS, 1, ml_w // MLSTM_HEADS),
            w_out=w_out_bf))

    n_rows = bp + bs
    pad_rows = -(-n_rows // 8) * 8
    c_all = jnp.concatenate([c_prompt, c_sample, jnp.zeros((pad_rows - n_rows, d), F32)], axis=0)
    mod = _ada(c_all, w_ada, b_ada)

    def mods(l, lo, hi):
        return tuple(mod[l, lo:hi, i * d:(i + 1) * d] for i in range(3))

    xp = x_prompt
    ps = []
    for l in range(depth):
        xp, st = _layer(xp, mods(l, 0, bp), layers[l], cfg, 0, None)
        ps.append(st)

    n_phys, page = cache_cmp_kv.shape[1], cache_cmp_kv.shape[2]
    n_pages = page_table.shape[1]
    past = n_pages * page
    assert past % SEL_BLOCK == 0 and t_new <= CMP_BLOCK and state_win_kv.shape[2] <= WINDOW
    pt_flat = page_table.reshape(-1).astype(jnp.int32)
    n_ch = 2 * n_kv
    cache_cmp4 = cache_cmp_kv.reshape(depth, n_phys, page * n_ch, HEAD_DIM)
    cache_sel3 = cache_sel_kv.reshape(depth, n_phys * page, n_ch, HEAD_DIM)
    win4 = state_win_kv.reshape(depth * bs, state_win_kv.shape[2] * n_ch, HEAD_DIM)
    xs = x_sample
    ss = []
    for l in range(depth):
        states = dict(layer=l, n_pages=n_pages, page=page, pt_flat=pt_flat, cache_cmp=cache_cmp4,
                      cache_sel=cache_sel3, win=win4, conv=state_conv[l], C=state_mlstm_C[l], n=state_mlstm_n[l],
                      m=state_mlstm_m[l])
        xs, st = _layer(xs, mods(l, bp, bp + bs), layers[l], cfg, past, states)
        ss.append(st)

    y_prompt = _rmsnorm(xp.reshape(bp * seq, d), final_norm_w).reshape(bp, seq, d)
    y_sample = _rmsnorm(xs.reshape(bs * t_new, d), final_norm_w).reshape(bs, t_new, d)
    n_keep = min(WINDOW, seq)
    stack = lambda sts, i: jnp.stack([s[i] for s in sts])
    p_win = jnp.stack([s[2][:, seq - n_keep:] for s in ps])
    return (y_prompt, y_sample, stack(ps, 0), stack(ps, 1), p_win, stack(ps, 3), stack(ps, 4), stack(ps, 5),
            stack(ps, 6), stack(ss, 0), stack(ss, 1), stack(ss, 2), stack(ss, 3), stack(ss, 4), stack(ss, 5),
            stack(ss, 6))
```

```python
import functools

import jax
import jax.numpy as jnp
from jax import lax
from jax.experimental import pallas as pl
from jax.experimental.pallas import tpu as pltpu

HEAD_DIM = 128
NSA_GROUP = 4
CMP_BLOCK = 32
SEL_BLOCK = 64
N_SELECT = 16
WINDOW = 512
FORCE_BONUS = 1.0e4
CONV_K = 3
MLSTM_HEADS = 4
ROPE_THETA = 10000.0
RMS_EPS = 1e-6
NEG_BIG = -1e30
TINY = 1e-30

LANES = 128
VMEM_LIMIT = 56 * 1024 * 1024

F32 = jnp.float32
BF16 = jnp.bfloat16

MAIN_NAMES = ('nsa_q', 'kv_cmp', 'kv_sel', 'kv_win', 'nsa_z', 'conv_x', 'conv_b', 'conv_c', 'conv_z',
              'm_q', 'm_k', 'm_v', 'm_o', 'm_z')
SMALL_NAMES = ('nsa_gate', 'm_i', 'm_f')


def _in_layout(d_model):
    nsa_w, kv_w, conv_w, ml_w = d_model // 2, d_model // 4, d_model // 4, d_model // 4
    nsa_heads = nsa_w // HEAD_DIM
    return (('nsa_q', nsa_w), ('kv_cmp', kv_w), ('kv_sel', kv_w), ('kv_win', kv_w),
            ('nsa_gate', 3 * nsa_heads), ('nsa_z', nsa_w),
            ('conv_x', conv_w), ('conv_b', conv_w), ('conv_c', conv_w), ('conv_z', conv_w),
            ('m_q', ml_w), ('m_k', ml_w), ('m_v', ml_w),
            ('m_i', MLSTM_HEADS), ('m_f', MLSTM_HEADS), ('m_o', ml_w), ('m_z', ml_w))


def _offsets(names, sizes):
    off, out = 0, {}
    for n in names:
        out[n] = off
        off += sizes[n]
    return out, off


def _cparams(sem):
    return pltpu.CompilerParams(dimension_semantics=sem, vmem_limit_bytes=VMEM_LIMIT)


def _tile(dim, pref):
    t = min(dim, pref)
    assert dim % t == 0, (dim, pref)
    return t


def _nt(a, b):
    return lax.dot_general(a, b, (((1,), (1,)), ((), ())), preferred_element_type=F32)


def _silu(x):
    return x * jax.nn.sigmoid(x)


def _lane_pick(x, col):
    lane = lax.broadcasted_iota(jnp.int32, x.shape, 1)
    return jnp.sum(jnp.where(lane == col, x, 0.0), axis=1, keepdims=True)


def _ada_kernel(c_ref, w_ref, b_ref, o_ref):
    a = _silu(c_ref[...]).astype(BF16)
    o_ref[0] = jnp.dot(a, w_ref[0].astype(BF16), preferred_element_type=F32) + b_ref[0]


def _ada(c_all, w_ada, b_ada):
    rows, d = c_all.shape
    depth, _, n = w_ada.shape
    tn = _tile(n, 512)
    return pl.pallas_call(
        _ada_kernel,
        grid=(depth, n // tn),
        in_specs=[pl.BlockSpec((rows, d), lambda l, j: (0, 0)),
                  pl.BlockSpec((1, d, tn), lambda l, j: (l, 0, j)),
                  pl.BlockSpec((1, 1, tn), lambda l, j: (l, 0, j))],
        out_specs=pl.BlockSpec((1, rows, tn), lambda l, j: (l, 0, j)),
        out_shape=jax.ShapeDtypeStruct((depth, rows, n), F32),
        compiler_params=_cparams(("arbitrary", "arbitrary")),
        name="ada",
    )(c_all, w_ada, b_ada.reshape(depth, 1, n))


SUBLANES = 8


def _wprep_kernel(tab_ref, wt_ref, wm_ref):
    del tab_ref
    wm_ref[0] = wt_ref[0].T.astype(wm_ref.dtype)


def _wprep(w_in, segs_main, n_main):
    depth, d, _ = w_in.shape
    tn = 512
    src_rows = []
    for src, size, dst in segs_main:
        assert size % tn == 0 and dst % tn == 0 and src % SUBLANES == 0
        src_rows += [(src + i) // SUBLANES for i in range(0, size, tn)]
    table = jnp.asarray(src_rows, jnp.int32)
    elem = lambda n: pl.Element(n)
    return pl.pallas_call(
        _wprep_kernel,
        grid_spec=pltpu.PrefetchScalarGridSpec(
            num_scalar_prefetch=1,
            grid=(depth, n_main // tn),
            in_specs=[pl.BlockSpec((elem(1), elem(tn), elem(d)), lambda l, j, tab: (l, tab[j] * SUBLANES, 0))],
            out_specs=pl.BlockSpec((1, d, tn), lambda l, j, tab: (l, 0, j))),
        out_shape=jax.ShapeDtypeStruct((depth, d, n_main), BF16),
        compiler_params=_cparams(("arbitrary", "arbitrary")),
        name="weight_prep",
    )(table, jnp.swapaxes(w_in, 1, 2))


def _modnorm_kernel(x_ref, sc_ref, sh_ref, nw_ref, h_ref):
    x = x_ref[...]
    y = x * lax.rsqrt(jnp.mean(x * x, axis=-1, keepdims=True) + RMS_EPS) * nw_ref[...]
    h_ref[...] = (y * (1.0 + sc_ref[0]) + sh_ref[0]).astype(h_ref.dtype)


def _modnorm(x2, scale3, shift3, nw, tm, rows_per_mod):
    m, d = x2.shape
    r = scale3.shape[1]
    mod_spec = pl.BlockSpec((1, r, d), lambda i: (i // rows_per_mod, 0, 0))
    return pl.pallas_call(
        _modnorm_kernel,
        grid=(m // tm,),
        in_specs=[pl.BlockSpec((tm, d), lambda i: (i, 0)), mod_spec, mod_spec,
                  pl.BlockSpec((1, d), lambda i: (0, 0))],
        out_specs=pl.BlockSpec((tm, d), lambda i: (i, 0)),
        out_shape=jax.ShapeDtypeStruct((m, d), BF16),
        compiler_params=_cparams(("arbitrary",)),
        name="modnorm",
    )(x2, scale3, shift3, nw)


def _inproj_kernel(h_ref, wm_ref, bm_ref, ws_ref, bs_ref, o_ref, os_ref):
    @pl.when(pl.program_id(1) == 0)
    def _():
        os_ref[...] = jnp.dot(h_ref[...], ws_ref[0], preferred_element_type=F32) + bs_ref[...]

    o_ref[...] = jnp.dot(h_ref[...], wm_ref[0], preferred_element_type=F32) + bm_ref[...]


def _inproj(h2, w_main, b_main, w_small, b_small, layer):
    m, d = h2.shape
    nm = w_main.shape[2]
    tm, tn = _tile(m, 1024), _tile(nm, 1024)
    return pl.pallas_call(
        _inproj_kernel,
        grid=(m // tm, nm // tn),
        in_specs=[pl.BlockSpec((tm, d), lambda i, j: (i, 0)),
                  pl.BlockSpec((1, d, tn), lambda i, j: (layer, 0, j)),
                  pl.BlockSpec((1, tn), lambda i, j: (0, j)),
                  pl.BlockSpec((1, d, LANES), lambda i, j: (layer, 0, 0)),
                  pl.BlockSpec((1, LANES), lambda i, j: (0, 0))],
        out_specs=[pl.BlockSpec((tm, tn), lambda i, j: (i, j)),
                   pl.BlockSpec((tm, LANES), lambda i, j: (i, 0))],
        out_shape=[jax.ShapeDtypeStruct((m, nm), F32), jax.ShapeDtypeStruct((m, LANES), F32)],
        compiler_params=_cparams(("arbitrary", "arbitrary")),
        name="inproj",
    )(h2, w_main, b_main, w_small, b_small)


def _rope_kernel(q_ref, kc_ref, ks_ref, kw_ref, cos_ref, sin_ref, qr_ref, ksr_ref, kwr_ref, co_ref, so_ref, wo_ref,
                 *, n_q, n_kv):
    cos, sin = cos_ref[...], sin_ref[...]
    tr, n_ch = q_ref.shape[0], 2 * n_kv

    def rot(x):
        return x * cos + pltpu.roll(x, HEAD_DIM // 2, axis=1) * sin

    for h in range(n_q):
        sl = slice(h * HEAD_DIM, (h + 1) * HEAD_DIM)
        qr_ref[:, sl] = (rot(q_ref[:, sl]) * HEAD_DIM ** -0.5).astype(qr_ref.dtype)
    for src, dense, state in ((kc_ref, None, co_ref), (ks_ref, ksr_ref, so_ref), (kw_ref, kwr_ref, wo_ref)):
        for ch in range(n_ch):
            sl = slice(ch * HEAD_DIM, (ch + 1) * HEAD_DIM)
            x = src[:, sl]
            if dense is not None and ch < n_kv:
                x = rot(x)
            if dense is not None:
                dense[:, sl] = x.astype(dense.dtype)
            state[pl.ds(ch, tr, stride=n_ch), :] = x


def _rope(proj, cos, sin, mo, nsa_w, kv_w):
    m = proj.shape[0]
    tr = _tile(m, 256)
    n_q, n_kv = nsa_w // HEAD_DIM, kv_w // (2 * HEAD_DIM)
    n_ch = 2 * n_kv
    kv_in = lambda name: pl.BlockSpec((tr, kv_w), lambda i: (i, mo[name] // kv_w))
    state_spec = pl.BlockSpec((tr * n_ch, HEAD_DIM), lambda i: (i, 0))
    state_sds = jax.ShapeDtypeStruct((m * n_ch, HEAD_DIM), F32)
    return pl.pallas_call(
        functools.partial(_rope_kernel, n_q=n_q, n_kv=n_kv),
        grid=(m // tr,),
        in_specs=[pl.BlockSpec((tr, nsa_w), lambda i: (i, mo['nsa_q'] // nsa_w)),
                  kv_in('kv_cmp'), kv_in('kv_sel'), kv_in('kv_win'),
                  pl.BlockSpec((tr, HEAD_DIM), lambda i: (i, 0)),
                  pl.BlockSpec((tr, HEAD_DIM), lambda i: (i, 0))],
        out_specs=[pl.BlockSpec((tr, nsa_w), lambda i: (i, 0)),
                   pl.BlockSpec((tr, kv_w), lambda i: (i, 0)),
                   pl.BlockSpec((tr, kv_w), lambda i: (i, 0)),
                   state_spec, state_spec, state_spec],
        out_shape=[jax.ShapeDtypeStruct((m, nsa_w), BF16),
                   jax.ShapeDtypeStruct((m, kv_w), BF16),
                   jax.ShapeDtypeStruct((m, kv_w), BF16),
                   state_sds, state_sds, state_sds],
        compiler_params=_cparams(("arbitrary",)),
        name="rope",
    )(proj, proj, proj, proj, cos, sin)


def _rope_tables(pos):
    half = HEAD_DIM // 2
    inv = ROPE_THETA ** (-jnp.arange(half, dtype=F32) / half)
    ang = pos.astype(F32)[:, None] * inv[None, :]
    cos, sin = jnp.cos(ang), jnp.sin(ang)
    return jnp.concatenate([cos, cos], axis=1), jnp.concatenate([-sin, sin], axis=1)


def _flash_step(carry, s, bias, v_bf):
    m, l, acc = carry
    if bias is not None:
        s = s + bias[None]
    m_new = jnp.maximum(m, jnp.max(s, axis=-1, keepdims=True))
    p = jnp.exp(s - m_new)
    alpha = jnp.exp(m - m_new)
    l = alpha * l + jnp.sum(p, axis=-1, keepdims=True)
    pv = jnp.dot(p.reshape(-1, p.shape[-1]).astype(BF16), v_bf, preferred_element_type=F32)
    acc = alpha * acc + pv.reshape(acc.shape)
    return m_new, l, acc


def _flash_init(shape_rows):
    return (jnp.full(shape_rows + (1,), NEG_BIG, F32), jnp.zeros(shape_rows + (1,), F32),
            jnp.zeros(shape_rows + (HEAD_DIM,), F32))


def _topk_rank(score, blk, n_blocks, axis=1):
    rank = jnp.zeros(score.shape, F32)
    for i in range(n_blocks):
        si = score[:, i:i + 1] if axis == 1 else score[i:i + 1, :]
        ge = jnp.where(si >= score, 1.0, 0.0)
        gt = jnp.where(si > score, 1.0, 0.0)
        rank = rank + jnp.where(blk > i, ge, gt)
    return rank


def _nsa_prompt_kernel(qc_ref, qr_ref, kc_ref, vc_ref, ks_ref, vs_ref, kw_ref, vw_ref, nz_ref, sm_ref,
                       pe_ref, wphi_ref, o_ref, kcs, vcs, *, seq, tq, tk, n_heads):
    kvh, qi = pl.program_id(1), pl.program_id(2)
    n_sel = seq // SEL_BLOCK
    n_cmp = 2 * n_sel
    grp = NSA_GROUP
    scale = HEAD_DIM ** -0.5

    @pl.when(qi == 0)
    def _():
        for c, (src, dst) in enumerate(((kc_ref, kcs), (vc_ref, vcs))):
            x = src[0].reshape(n_sel, 2 * CMP_BLOCK, HEAD_DIM)
            pe = pe_ref[c][None]
            ev = jnp.mean(x[:, :CMP_BLOCK, :] + pe, axis=1)
            od = jnp.mean(x[:, CMP_BLOCK:, :] + pe, axis=1)
            summ = jnp.concatenate([ev, od], axis=0).astype(BF16)
            dst[...] = jnp.dot(summ, wphi_ref[c].astype(BF16), preferred_element_type=F32)

    t0 = qi * tq
    pos = t0 + lax.broadcasted_iota(jnp.int32, (tq, 1), 0)
    pos_l = t0 + lax.broadcasted_iota(jnp.int32, (1, tq), 1)

    row_c = lax.broadcasted_iota(jnp.int32, (n_cmp, 1), 0)
    cblk = jnp.where(row_c < n_sel, 2 * row_c, 2 * (row_c - n_sel) + 1)
    cmp_ok = (cblk + 1) * CMP_BLOCK - 1 <= pos_l
    kcb, vcb = kcs[...].astype(BF16), vcs[...].astype(BF16)
    qc = qc_ref[0]
    imp = jnp.zeros((n_cmp, tq), F32)
    o_cmp = []
    for g in range(grp):
        s = _nt(kcb, qc[:, g * HEAD_DIM:(g + 1) * HEAD_DIM].astype(BF16)) * scale
        s = jnp.where(cmp_ok, s, NEG_BIG)
        s = s - jnp.max(s, axis=0, keepdims=True)
        p = jnp.where(cmp_ok, jnp.exp(s), 0.0)
        p = p / jnp.maximum(jnp.sum(p, axis=0, keepdims=True), TINY)
        imp = imp + p
        o_cmp.append(lax.dot_general(p.astype(BF16), vcb, (((0,), (0,)), ((), ())), preferred_element_type=F32))

    imp_sel = imp[:n_sel] + imp[n_sel:]
    blk = lax.broadcasted_iota(jnp.int32, (n_sel, 1), 0)
    q_blk = pos_l // SEL_BLOCK
    forced = (blk == 0) | (blk == q_blk) | (blk == q_blk - 1)
    score = jnp.where(forced, imp_sel + FORCE_BONUS, imp_sel)
    score = jnp.where(blk * SEL_BLOCK > pos_l, -jnp.inf, score)
    rank = _topk_rank(score, blk, n_sel, axis=0)
    sel_neg = jnp.concatenate([jnp.where(rank < N_SELECT, 0.0, NEG_BIG), jnp.zeros((LANES - n_sel, tq), F32)], axis=0)
    sel_neg = sel_neg.T.astype(BF16)

    qr = qr_ref[0]
    q4 = jnp.concatenate([qr[:, g * HEAD_DIM:(g + 1) * HEAD_DIM] for g in range(grp)], axis=0)
    q4_sel = jnp.concatenate([q4, jnp.concatenate([sel_neg] * grp, axis=0)], axis=1)

    def sel_keys(k0):
        kch = ks_ref[0, pl.ds(k0, tk), :].astype(BF16)
        kblk = (k0 + lax.broadcasted_iota(jnp.int32, (tk, 1), 0)) // SEL_BLOCK
        one_hot = jnp.where(kblk == lax.broadcasted_iota(jnp.int32, (1, LANES), 1), 1.0, 0.0).astype(BF16)
        return jnp.concatenate([kch, one_hot], axis=1), vs_ref[0, pl.ds(k0, tk), :].astype(BF16)

    def sel_step(k0, carry, bias):
        k_aug, vch = sel_keys(k0)
        return _flash_step(carry, _nt(q4_sel, k_aug).reshape(grp, tq, tk), bias, vch)

    n_chunks = (t0 + tq + tk - 1) // tk
    carry = lax.fori_loop(0, n_chunks - 1, lambda c, cr: sel_step(pl.multiple_of(c * tk, tk), cr, None),
                          _flash_init((grp, tq)))
    k_last = pl.multiple_of((n_chunks - 1) * tk, tk)
    kpos = k_last + lax.broadcasted_iota(jnp.int32, (1, tk), 1)
    _, l_s, acc_s = sel_step(k_last, carry, jnp.where(kpos <= pos, 0.0, NEG_BIG))

    def win_body(i, carry):
        k0 = pl.multiple_of((qi - i) * tq, tq)
        kpos = k0 + lax.broadcasted_iota(jnp.int32, (1, tq), 1)
        bias = jnp.where(jnp.where(kpos <= pos, pos - kpos, WINDOW) < WINDOW, 0.0, NEG_BIG)
        s = _nt(q4, kw_ref[0, pl.ds(k0, tq), :].astype(BF16)).reshape(grp, tq, tq)
        return _flash_step(carry, s, bias, vw_ref[0, pl.ds(k0, tq), :].astype(BF16))

    n_back = jnp.minimum(qi, (WINDOW + tq - 1) // tq)
    _, l_w, acc_w = lax.fori_loop(0, n_back + 1, win_body, _flash_init((grp, tq)))

    gates = jax.nn.sigmoid(sm_ref[0])
    nz = nz_ref[0]
    for g in range(grp):
        head = kvh * grp + g
        w_cmp = _lane_pick(gates, head)
        w_sel = _lane_pick(gates, n_heads + head) / jnp.maximum(l_s[g], TINY)
        w_win = _lane_pick(gates, 2 * n_heads + head) / jnp.maximum(l_w[g], TINY)
        o = w_cmp * o_cmp[g] + (w_sel * acc_s[g] + w_win * acc_w[g])
        sl = slice(g * HEAD_DIM, (g + 1) * HEAD_DIM)
        o_ref[0, :, sl] = (o * _silu(nz[:, sl])).astype(o_ref.dtype)


def _nsa_prompt(proj3, small3, qr3, ksr3, kwr3, pe, wphi, mo, nsa_w):
    b, seq, _ = proj3.shape
    n_heads = nsa_w // HEAD_DIM
    kvh = n_heads // NSA_GROUP
    gw = NSA_GROUP * HEAD_DIM
    tq = _tile(seq, 512)
    tk = _tile(seq, 512)
    assert seq % (2 * CMP_BLOCK) == 0 and tk % SEL_BLOCK == 0
    assert N_SELECT >= 3, "the forced blocks (first, current, previous) must always fit in the selection"
    assert seq // SEL_BLOCK <= LANES
    hb = lambda name: mo[name] // HEAD_DIM

    def head_spec(off_blocks):
        return pl.BlockSpec((1, seq, HEAD_DIM), lambda bi, k, qi: (bi, 0, off_blocks + k))

    return pl.pallas_call(
        functools.partial(_nsa_prompt_kernel, seq=seq, tq=tq, tk=tk, n_heads=n_heads),
        grid=(b, kvh, seq // tq),
        in_specs=[pl.BlockSpec((1, tq, gw), lambda bi, k, qi: (bi, qi, mo['nsa_q'] // gw + k)),
                  pl.BlockSpec((1, tq, gw), lambda bi, k, qi: (bi, qi, k)),
                  head_spec(hb('kv_cmp')), head_spec(hb('kv_cmp') + kvh),
                  head_spec(0), head_spec(kvh), head_spec(0), head_spec(kvh),
                  pl.BlockSpec((1, tq, gw), lambda bi, k, qi: (bi, qi, mo['nsa_z'] // gw + k)),
                  pl.BlockSpec((1, tq, LANES), lambda bi, k, qi: (bi, qi, 0)),
                  pl.BlockSpec((2, CMP_BLOCK, HEAD_DIM), lambda bi, k, qi: (0, 0, 0)),
                  pl.BlockSpec((2, HEAD_DIM, HEAD_DIM), lambda bi, k, qi: (0, 0, 0))],
        out_specs=pl.BlockSpec((1, tq, gw), lambda bi, k, qi: (bi, qi, k)),
        out_shape=jax.ShapeDtypeStruct((b, seq, nsa_w), BF16),
        scratch_shapes=[pltpu.VMEM((seq // CMP_BLOCK, HEAD_DIM), F32),
                        pltpu.VMEM((seq // CMP_BLOCK, HEAD_DIM), F32)],
        compiler_params=_cparams(("arbitrary", "arbitrary", "arbitrary")),
        name="nsa_prompt",
    )(proj3, qr3, proj3, proj3, ksr3, ksr3, kwr3, kwr3, proj3, small3, pe, wphi)


def _summ_kernel(pt_ref, *refs, n_pg, n_ch):
    del pt_ref
    x_refs, pe_ref, o_ref = refs[:n_pg], refs[n_pg], refs[n_pg + 1]
    pe = pe_ref[...][None]
    outs = []
    for r in x_refs:
        page = r[0, 0]
        n_blk = page.shape[0] // (CMP_BLOCK * n_ch)
        x = page.reshape(n_blk, CMP_BLOCK, n_ch, HEAD_DIM)
        outs.append(jnp.mean(x + pe, axis=1).reshape(n_blk * n_ch, HEAD_DIM))
    o_ref[0] = jnp.concatenate(outs, axis=0)


def _summaries(cache4, layer, pt_flat, pe_ch, n_batch, n_pages, n_ch):
    page_rows = cache4.shape[2]
    blk_per_page = page_rows // (CMP_BLOCK * n_ch)
    n_pg = _tile(n_pages, 8)
    rows = n_pg * blk_per_page * n_ch

    def page_spec(k):
        return pl.BlockSpec((1, 1, page_rows, HEAD_DIM),
                            lambda bi, j, pt: (layer, pt[bi * n_pages + j * n_pg + k], 0, 0))

    return pl.pallas_call(
        functools.partial(_summ_kernel, n_pg=n_pg, n_ch=n_ch),
        grid_spec=pltpu.PrefetchScalarGridSpec(
            num_scalar_prefetch=1,
            grid=(n_batch, n_pages // n_pg),
            in_specs=[page_spec(k) for k in range(n_pg)]
            + [pl.BlockSpec((CMP_BLOCK, n_ch, HEAD_DIM), lambda bi, j, pt: (0, 0, 0))],
            out_specs=pl.BlockSpec((1, rows, HEAD_DIM), lambda bi, j, pt: (bi, j, 0))),
        out_shape=jax.ShapeDtypeStruct((n_batch, n_pages * blk_per_page * n_ch, HEAD_DIM), F32),
        compiler_params=_cparams(("arbitrary", "arbitrary")),
        name="cmp_summaries",
    )(pt_flat, *([cache4] * n_pg), pe_ch)


def _cmpsel_kernel(summ_ref, q_ref, wphi_ref, ocmp_ref, sel_ref, *, past, t_new, kvh_n, n_selp):
    grp = NSA_GROUP
    scale = HEAD_DIM ** -0.5
    n_ch = 2 * kvh_n
    n_half = summ_ref.shape[1] // (2 * n_ch)
    n_sel = n_half + 1

    def summaries(ch, odd):
        return summ_ref[0, pl.ds(odd * n_ch + ch, n_half, stride=2 * n_ch), :]
    tpos = lax.broadcasted_iota(jnp.int32, (t_new, 1), 0) + past
    pos = jnp.concatenate([tpos] * grp, axis=0)
    lane = lax.broadcasted_iota(jnp.int32, (1, n_half), 1)
    ok_e = (2 * lane + 1) * CMP_BLOCK - 1 <= pos
    ok_o = (2 * lane + 2) * CMP_BLOCK - 1 <= pos
    w_k, w_v = wphi_ref[0].astype(BF16), wphi_ref[1].astype(BF16)
    q = q_ref[0]
    scores = []
    for k in range(kvh_n):
        proj = lambda ch, odd, w: jnp.dot(summaries(ch, odd).astype(BF16), w,
                                          preferred_element_type=F32).astype(BF16)
        k_e, k_o = proj(k, 0, w_k), proj(k, 1, w_k)
        v_e, v_o = proj(kvh_n + k, 0, w_v), proj(kvh_n + k, 1, w_v)
        qk = jnp.concatenate(
            [q[:, (k * grp + g) * HEAD_DIM:(k * grp + g + 1) * HEAD_DIM] for g in range(grp)], axis=0).astype(BF16)
        s_e = jnp.where(ok_e, _nt(qk, k_e) * scale, NEG_BIG)
        s_o = jnp.where(ok_o, _nt(qk, k_o) * scale, NEG_BIG)
        m = jnp.maximum(jnp.max(s_e, axis=-1, keepdims=True), jnp.max(s_o, axis=-1, keepdims=True))
        p_e = jnp.where(ok_e, jnp.exp(s_e - m), 0.0)
        p_o = jnp.where(ok_o, jnp.exp(s_o - m), 0.0)
        den = jnp.maximum(jnp.sum(p_e, axis=-1, keepdims=True) + jnp.sum(p_o, axis=-1, keepdims=True), TINY)
        p_e, p_o = p_e / den, p_o / den
        o = (jnp.dot(p_e.astype(BF16), v_e, preferred_element_type=F32)
             + jnp.dot(p_o.astype(BF16), v_o, preferred_element_type=F32))
        imp = jnp.zeros((t_new, n_half), F32)
        for g in range(grp):
            hsl = slice((k * grp + g) * HEAD_DIM, (k * grp + g + 1) * HEAD_DIM)
            ocmp_ref[0, :, hsl] = o[g * t_new:(g + 1) * t_new]
            imp = imp + (p_e[g * t_new:(g + 1) * t_new] + p_o[g * t_new:(g + 1) * t_new])
        tail_lane = lax.broadcasted_iota(jnp.int32, (t_new, n_selp - n_half), 1)
        tail = jnp.where(tail_lane == 0, 0.0, -jnp.inf)
        scores.append(jnp.concatenate([imp, tail], axis=1))
    score = jnp.concatenate(scores, axis=0)
    pos_r = jnp.concatenate([tpos] * kvh_n, axis=0)
    blk = lax.broadcasted_iota(jnp.int32, (1, n_selp), 1)
    q_blk = pos_r // SEL_BLOCK
    forced = (blk == 0) | (blk == q_blk) | (blk == q_blk - 1)
    score = jnp.where(forced, score + FORCE_BONUS, score)
    score = jnp.where((blk * SEL_BLOCK > pos_r) | (blk >= n_sel), -jnp.inf, score)
    rank = _topk_rank(score, blk, n_sel)

    n_slot = sel_ref.shape[3]
    n_top = min(N_SELECT, n_sel)
    blk_f = blk.astype(F32)
    blk_hi = jnp.floor(blk_f * (1.0 / 16.0))
    blk_lo = blk_f - 16.0 * blk_hi
    before = jnp.where(lax.broadcasted_iota(jnp.int32, (n_selp, 1), 0) < blk, 1.0, 0.0).astype(BF16)
    slot = lax.broadcasted_iota(jnp.int32, (n_slot, 1), 0).astype(F32)
    for k in range(kvh_n):
        member = jnp.where(rank[k * t_new:(k + 1) * t_new] < n_top, 1.0, 0.0)
        member_past = jnp.where(blk < n_half, member, 0.0)
        used = jnp.max(member_past, axis=0, keepdims=True)
        n_before = jnp.dot(jnp.broadcast_to(used, (SUBLANES, n_selp)).astype(BF16), before,
                           preferred_element_type=F32)[0:1]
        place = jnp.where(used > 0.5, jnp.where(n_before == slot, 1.0, 0.0), 0.0).astype(BF16)
        rows = jnp.concatenate([blk_hi, blk_lo, used, jnp.zeros((SUBLANES - 3, n_selp), F32), member_past], axis=0)
        packed = _nt(rows.astype(BF16), place)
        has_new = jnp.broadcast_to(member[:, n_half:n_half + 1], (t_new, n_slot))
        sel_ref[0, k] = jnp.concatenate([packed, has_new], axis=0)


def _cmpsel(summ, proj3, wphi, mo, nsa_w, past):
    b, summ_rows, _ = summ.shape
    t_new = proj3.shape[1]
    kvh_n = nsa_w // HEAD_DIM // NSA_GROUP
    n_half = summ_rows // (4 * kvh_n)
    assert n_half == past // SEL_BLOCK
    n_selp = -(-(n_half + 1) // LANES) * LANES
    n_slot = t_new * min(N_SELECT, n_half + 1)
    sel_rows = SUBLANES + 2 * t_new
    assert t_new == SUBLANES
    return pl.pallas_call(
        functools.partial(_cmpsel_kernel, past=past, t_new=t_new, kvh_n=kvh_n, n_selp=n_selp),
        grid=(b,),
        in_specs=[pl.BlockSpec((1, summ_rows, HEAD_DIM), lambda bi: (bi, 0, 0)),
                  pl.BlockSpec((1, t_new, nsa_w), lambda bi: (bi, 0, mo['nsa_q'] // nsa_w)),
                  pl.BlockSpec((2, HEAD_DIM, HEAD_DIM), lambda bi: (0, 0, 0))],
        out_specs=[pl.BlockSpec((1, t_new, nsa_w), lambda bi: (bi, 0, 0)),
                   pl.BlockSpec((1, kvh_n, sel_rows, n_slot), lambda bi: (bi, 0, 0, 0))],
        out_shape=[jax.ShapeDtypeStruct((b, t_new, nsa_w), F32),
                   jax.ShapeDtypeStruct((b, kvh_n, sel_rows, n_slot), F32)],
        compiler_params=_cparams(("arbitrary",)),
        name="cmp_select",
    )(summ, proj3, wphi)


def _two_part_attention(parts):
    m = None
    for s, msk, _ in parts:
        mx = jnp.max(jnp.where(msk, s, NEG_BIG), axis=-1, keepdims=True)
        m = mx if m is None else jnp.maximum(m, mx)
    den, acc = 0.0, 0.0
    for s, msk, v in parts:
        p = jnp.where(msk, jnp.exp(jnp.where(msk, s, NEG_BIG) - m), 0.0)
        den = den + jnp.sum(p, axis=-1, keepdims=True)
        acc = acc + jnp.dot(p.astype(BF16), v, preferred_element_type=F32)
    return acc / jnp.maximum(den, TINY)


SLOT_CHUNK = 16


def _selwin_kernel(idx_ref, cnt_ref, pt_ref, cache_ref, qr_ref, ksn_ref, vsn_ref, wbuf_ref, kwn_ref, vwn_ref,
                   sm_ref, ocmp_ref, nz_ref, sel_ref, o_ref, kbuf, vbuf, sem,
                   *, layer, past, t_new, kvh_n, n_pages, page, n_heads):
    bi, kvh = pl.program_id(0), pl.program_id(1)
    grp = NSA_GROUP
    n_ch = 2 * kvh_n
    n_slot = kbuf.shape[0]
    n_past_blk = past // SEL_BLOCK
    per_page = page // SEL_BLOCK
    base = (bi * kvh_n + kvh) * n_slot
    n_chunk = (cnt_ref[bi * kvh_n + kvh] + SLOT_CHUNK - 1) // SLOT_CHUNK
    n_issue = n_chunk * SLOT_CHUNK

    def copies(i, row0):
        return (pltpu.make_async_copy(cache_ref.at[layer, pl.ds(row0, SEL_BLOCK), kvh, :], kbuf.at[i], sem.at[0]),
                pltpu.make_async_copy(cache_ref.at[layer, pl.ds(row0, SEL_BLOCK), kvh_n + kvh, :], vbuf.at[i],
                                      sem.at[1]))

    def issue(i, carry):
        idc = jnp.minimum(idx_ref[base + i], n_past_blk - 1)
        phys = pt_ref[bi * n_pages + idc // per_page]
        row0 = pl.multiple_of(phys * page + (idc % per_page) * SEL_BLOCK, SEL_BLOCK)
        for cp in copies(i, row0):
            cp.start()
        return carry

    def wait(i, carry):
        for cp in copies(i, 0):
            cp.wait()
        return carry

    lax.fori_loop(0, n_issue, issue, 0)
    lax.fori_loop(0, n_issue, wait, 0)

    qr = qr_ref[0]
    q4 = jnp.concatenate([qr[:, g * HEAD_DIM:(g + 1) * HEAD_DIM] for g in range(grp)], axis=0)
    rows = grp * t_new
    row_t = jnp.concatenate([lax.broadcasted_iota(jnp.int32, (t_new, 1), 0)] * grp, axis=0)
    pos = row_t + past

    def padded(ref):
        x = ref[0].astype(F32)
        return jnp.concatenate([x, jnp.zeros((SEL_BLOCK - t_new, HEAD_DIM), F32)], axis=0).astype(BF16)

    new_j = lax.broadcasted_iota(jnp.int32, (1, SEL_BLOCK), 1)
    new_pos = past + new_j
    new_real = new_j < t_new

    flags = sel_ref[0, 0]
    uses_slot = jnp.concatenate([flags[SUBLANES:SUBLANES + t_new]] * grp, axis=0).astype(BF16)
    has_new = jnp.concatenate([flags[SUBLANES + t_new:SUBLANES + 2 * t_new, 0:1]] * grp, axis=0)
    width = SLOT_CHUNK * SEL_BLOCK

    def sel_body(c, carry):
        s0 = pl.multiple_of(c * SLOT_CHUNK, SLOT_CHUNK)
        k_c = kbuf[pl.ds(s0, SLOT_CHUNK)].reshape(width, HEAD_DIM).astype(BF16)
        v_c = vbuf[pl.ds(s0, SLOT_CHUNK)].reshape(width, HEAD_DIM).astype(BF16)
        slot_of_key = s0 + lax.broadcasted_iota(jnp.int32, (1, width), 1) // SEL_BLOCK
        expand = jnp.where(slot_of_key == lax.broadcasted_iota(jnp.int32, (n_slot, 1), 0), 1.0, 0.0).astype(BF16)
        bias = jnp.where(jnp.dot(uses_slot, expand, preferred_element_type=F32) > 0.5, 0.0, NEG_BIG)
        return _flash_step(carry, _nt(q4, k_c)[None], bias, v_c)

    carry = lax.fori_loop(0, n_chunk, sel_body, _flash_init((1, rows)))
    bias_new = jnp.where(jnp.where(new_real & (new_pos <= pos), has_new, 0.0) > 0.5, 0.0, NEG_BIG)
    _, l_sel, acc_sel = _flash_step(carry, _nt(q4, padded(ksn_ref))[None], bias_new, padded(vsn_ref))
    o_sel = acc_sel[0] / jnp.maximum(l_sel[0], TINY)

    n_buf = wbuf_ref.shape[1] // n_ch
    kwb = wbuf_ref[0, pl.ds(kvh, n_buf, stride=n_ch), :].astype(BF16)
    vwb = wbuf_ref[0, pl.ds(kvh_n + kvh, n_buf, stride=n_ch), :].astype(BF16)
    buf_pos = past - n_buf + lax.broadcasted_iota(jnp.int32, (1, n_buf), 1)
    msk_buf = jnp.where(buf_pos <= pos, pos - buf_pos, WINDOW) < WINDOW
    msk_wn = jnp.where(new_real & (new_pos <= pos), pos - new_pos, WINDOW) < WINDOW
    o_win = _two_part_attention([(_nt(q4, kwb), msk_buf, vwb),
                                 (_nt(q4, padded(kwn_ref)), msk_wn, padded(vwn_ref))])

    gates = jax.nn.sigmoid(sm_ref[0])
    ocmp, nz = ocmp_ref[0], nz_ref[0]
    for g in range(grp):
        head = kvh * grp + g
        rs = slice(g * t_new, (g + 1) * t_new)
        sl = slice(g * HEAD_DIM, (g + 1) * HEAD_DIM)
        o = (_lane_pick(gates, head) * ocmp[:, sl]
             + (_lane_pick(gates, n_heads + head) * o_sel[rs] + _lane_pick(gates, 2 * n_heads + head) * o_win[rs]))
        o_ref[0, :, sl] = (o * _silu(nz[:, sl])).astype(o_ref.dtype)


def _selwin(sel, pt_flat, cache4, layer, qr3, ksr3, win_state3, kwr3, small3, ocmp3, proj3, mo, nsa_w,
            past, n_pages, page):
    b, t_new, _ = qr3.shape
    n_heads = nsa_w // HEAD_DIM
    kvh_n = n_heads // NSA_GROUP
    gw = NSA_GROUP * HEAD_DIM
    sel_rows, n_slot = sel.shape[2], sel.shape[3]
    assert n_slot % SLOT_CHUNK == 0
    slot_blocks = (16.0 * sel[:, :, 0] + sel[:, :, 1]).astype(jnp.int32).reshape(-1)
    slot_count = jnp.sum(sel[:, :, 2], axis=-1).astype(jnp.int32).reshape(-1)
    win_rows = win_state3.shape[1]
    new_k = pl.BlockSpec((1, t_new, HEAD_DIM), lambda bi, k, *_: (bi, 0, k))
    new_v = pl.BlockSpec((1, t_new, HEAD_DIM), lambda bi, k, *_: (bi, 0, kvh_n + k))
    return pl.pallas_call(
        functools.partial(_selwin_kernel, layer=layer, past=past, t_new=t_new, kvh_n=kvh_n, n_pages=n_pages,
                          page=page, n_heads=n_heads),
        grid_spec=pltpu.PrefetchScalarGridSpec(
            num_scalar_prefetch=3,
            grid=(b, kvh_n),
            in_specs=[pl.BlockSpec(memory_space=pl.ANY),
                      pl.BlockSpec((1, t_new, gw), lambda bi, k, *_: (bi, 0, k)),
                      new_k, new_v,
                      pl.BlockSpec((1, win_rows, HEAD_DIM), lambda bi, k, *_: (layer * b + bi, 0, 0)),
                      new_k, new_v,
                      pl.BlockSpec((1, t_new, LANES), lambda bi, k, *_: (bi, 0, 0)),
                      pl.BlockSpec((1, t_new, gw), lambda bi, k, *_: (bi, 0, k)),
                      pl.BlockSpec((1, t_new, gw), lambda bi, k, *_: (bi, 0, mo['nsa_z'] // gw + k)),
                      pl.BlockSpec((1, 1, sel_rows, n_slot), lambda bi, k, *_: (bi, k, 0, 0))],
            out_specs=pl.BlockSpec((1, t_new, gw), lambda bi, k, *_: (bi, 0, k)),
            scratch_shapes=[pltpu.VMEM((n_slot, SEL_BLOCK, HEAD_DIM), F32),
                            pltpu.VMEM((n_slot, SEL_BLOCK, HEAD_DIM), F32),
                            pltpu.SemaphoreType.DMA((2,))]),
        out_shape=jax.ShapeDtypeStruct((b, t_new, nsa_w), BF16),
        compiler_params=_cparams(("arbitrary", "arbitrary")),
        name="sel_win_sample",
    )(slot_blocks, slot_count, pt_flat, cache4, qr3, ksr3, ksr3, win_state3, kwr3, kwr3, small3, ocmp3, proj3, sel)


def _conv_kernel(x_ref, b_ref, c_ref, z_ref, buf_ref, w_ref, o_ref, st_ref, prev):
    tc = x_ref.shape[1]

    @pl.when(pl.program_id(1) == 0)
    def _():
        prev[0:CONV_K - 1, :] = buf_ref[0]

    u = c_ref[0] * x_ref[0]
    row = lax.broadcasted_iota(jnp.int32, (tc, 1), 0)
    w = w_ref[...]
    y = u * w[CONV_K - 1:CONV_K, :]
    for back in range(1, CONV_K):
        shifted = pltpu.roll(u, back, axis=0)
        for r in range(back):
            shifted = jnp.where(row == r, prev[CONV_K - 1 - back + r:CONV_K - back + r, :], shifted)
        y = y + shifted * w[CONV_K - 1 - back:CONV_K - back, :]
    o_ref[0] = (b_ref[0] * y * _silu(z_ref[0])).astype(o_ref.dtype)
    tail = u[tc - (CONV_K - 1):, :]
    prev[0:CONV_K - 1, :] = tail
    st_ref[0] = tail


def _conv(proj3, buf, w, mo, conv_w):
    b, seq, _ = proj3.shape
    tc = _tile(seq, 512)
    assert tc >= CONV_K - 1
    col = lambda name: pl.BlockSpec((1, tc, conv_w), lambda bi, ti: (bi, ti, mo[name] // conv_w))
    return pl.pallas_call(
        _conv_kernel,
        grid=(b, seq // tc),
        in_specs=[col('conv_x'), col('conv_b'), col('conv_c'), col('conv_z'),
                  pl.BlockSpec((1, CONV_K - 1, conv_w), lambda bi, ti: (bi, 0, 0)),
                  pl.BlockSpec((CONV_K, conv_w), lambda bi, ti: (0, 0))],
        out_specs=[pl.BlockSpec((1, tc, conv_w), lambda bi, ti: (bi, ti, 0)),
                   pl.BlockSpec((1, CONV_K - 1, conv_w), lambda bi, ti: (bi, 0, 0))],
        out_shape=[jax.ShapeDtypeStruct((b, seq, conv_w), BF16),
                   jax.ShapeDtypeStruct((b, CONV_K - 1, conv_w), F32)],
        scratch_shapes=[pltpu.VMEM((8, conv_w), F32)],
        compiler_params=_cparams(("arbitrary", "arbitrary")),
        name="short_conv",
    )(proj3, proj3, proj3, proj3, buf, w)


def _log_sigmoid(x):
    return jnp.minimum(x, 0.0) - jnp.log(1.0 + jnp.exp(-jnp.abs(x)))


def _mlstm_kernel(q_ref, k_ref, v_ref, og_ref, z_ref, sm_ref, gr_ref, c0_ref, n0_ref, m0_ref, nw_ref,
                  o_ref, c_out, n_out, m_out, c_s, n_s, m_s, *, n_chunks, gate_i, gate_f):
    ci = pl.program_id(1)

    @pl.when(ci == 0)
    def _():
        c_s[...] = c0_ref[0]
        n_s[...] = n0_ref[0]
        m_s[...] = m0_ref[0]

    small, gr = sm_ref[0], gr_ref[0]
    dk = c_s.shape[1]
    for h in range(MLSTM_HEADS):
        hs = slice(h * dk, (h + 1) * dk)
        c_new, n_new, m_new, out = _mlstm_head(
            q_ref[0, :, hs], k_ref[0, :, hs], v_ref[0, :, hs], og_ref[0, :, hs], z_ref[0, :, hs],
            small[:, gate_i + h:gate_i + h + 1], small[:, gate_f + h:gate_f + h + 1],
            gr[h:h + 1, :], gr[MLSTM_HEADS + h:MLSTM_HEADS + h + 1, :],
            c_s[h], n_s[h], m_s[h], nw_ref[h])
        c_s[h] = c_new
        n_s[h] = n_new
        m_s[h] = m_new
        o_ref[0, :, hs] = out.astype(o_ref.dtype)

    @pl.when(ci == n_chunks - 1)
    def _():
        c_out[0] = c_s[...]
        n_out[0] = n_s[...]
        m_out[0] = m_s[...]


def _mlstm_head(q, k, v, og, z, i_col, f_col, i_row, f_row, c_prev, n_prev, m_prev, norm_w):
    lc, dk = q.shape
    lf_col, lf_row = _log_sigmoid(f_col), _log_sigmoid(f_row)
    ti = lax.broadcasted_iota(jnp.int32, (lc, 1), 0)
    si = lax.broadcasted_iota(jnp.int32, (1, lc), 1)
    causal = si <= ti
    b_col = jnp.sum(jnp.where(causal, lf_row, 0.0), axis=1, keepdims=True)
    b_row = jnp.sum(jnp.where(ti <= si, lf_col, 0.0), axis=0, keepdims=True)
    log_d = jnp.where(causal, b_col - b_row + i_row, -jnp.inf)
    m_t = jnp.maximum(b_col + m_prev, jnp.max(log_d, axis=1, keepdims=True))
    inter = jnp.exp(b_col + m_prev - m_t)
    kf = k * (dk ** -0.5)
    qb, kb, vb = q.astype(BF16), kf.astype(BF16), v.astype(BF16)
    s = _nt(qb, kb) * jnp.exp(log_d - m_t)
    num = (inter * jnp.dot(qb, c_prev.astype(BF16), preferred_element_type=F32)
           + jnp.dot(s.astype(BF16), vb, preferred_element_type=F32))
    den = inter * jnp.sum(q * n_prev, axis=1, keepdims=True) + jnp.sum(s, axis=1, keepdims=True)
    hid = num / jnp.maximum(jnp.abs(den), jnp.exp(-m_t))

    m_new = m_t[lc - 1:lc, :]
    b_last = b_col[lc - 1:lc, :]
    wdec = jnp.exp(b_last - b_col + i_col - m_new)
    cdec = jnp.exp(b_last + m_prev - m_new)
    c_new = cdec * c_prev + lax.dot_general(kb, (wdec * v).astype(BF16), (((0,), (0,)), ((), ())),
                                            preferred_element_type=F32)
    n_new = cdec * n_prev + jnp.sum(wdec * kf, axis=0, keepdims=True)
    hn = hid * lax.rsqrt(jnp.mean(hid * hid, axis=-1, keepdims=True) + RMS_EPS) * norm_w
    return c_new, n_new, m_new, jax.nn.sigmoid(og) * hn * _silu(z)


def _mlstm(proj3, small3, gates_row, c0, n0, m0, norm_w, mo, ml_w, gate_i, gate_f):
    b, seq, _ = proj3.shape
    nh = MLSTM_HEADS
    dk = ml_w // nh
    lc = _tile(seq, 256)
    n_chunks = seq // lc
    col = lambda name: pl.BlockSpec((1, lc, ml_w), lambda bi, ci: (bi, ci, mo[name] // ml_w))
    state = lambda shape: pl.BlockSpec((1, nh) + shape, lambda bi, ci: (bi, 0, 0, 0))
    return pl.pallas_call(
        functools.partial(_mlstm_kernel, n_chunks=n_chunks, gate_i=gate_i, gate_f=gate_f),
        grid=(b, n_chunks),
        in_specs=[col('m_q'), col('m_k'), col('m_v'), col('m_o'), col('m_z'),
                  pl.BlockSpec((1, lc, LANES), lambda bi, ci: (bi, ci, 0)),
                  pl.BlockSpec((1, 2 * nh, lc), lambda bi, ci: (bi, 0, ci)),
                  state((dk, dk)), state((1, dk)), state((1, 1)),
                  pl.BlockSpec((nh, 1, dk), lambda bi, ci: (0, 0, 0))],
        out_specs=[pl.BlockSpec((1, lc, ml_w), lambda bi, ci: (bi, ci, 0)),
                   state((dk, dk)), state((1, dk)), state((1, 1))],
        out_shape=[jax.ShapeDtypeStruct((b, seq, ml_w), BF16),
                   jax.ShapeDtypeStruct((b, nh, dk, dk), F32),
                   jax.ShapeDtypeStruct((b, nh, 1, dk), F32),
                   jax.ShapeDtypeStruct((b, nh, 1, 1), F32)],
        scratch_shapes=[pltpu.VMEM((nh, dk, dk), F32), pltpu.VMEM((nh, 1, dk), F32), pltpu.VMEM((nh, 1, 1), F32)],
        compiler_params=_cparams(("arbitrary", "arbitrary")),
        name="mlstm",
    )(proj3, proj3, proj3, proj3, proj3, small3, gates_row, c0, n0, m0, norm_w)


def _outproj_kernel(x_ref, g_ref, a_ref, c_ref, m_ref, w_ref, o_ref):
    ka, kc = a_ref.shape[1], c_ref.shape[1]
    acc = jnp.dot(a_ref[...], w_ref[0, 0:ka, :], preferred_element_type=F32)
    acc = acc + jnp.dot(c_ref[...], w_ref[0, ka:ka + kc, :], preferred_element_type=F32)
    acc = acc + jnp.dot(m_ref[...], w_ref[0, ka + kc:, :], preferred_element_type=F32)
    o_ref[...] = x_ref[...] + g_ref[0] * acc


def _outproj(x2, gate3, o_nsa, o_conv, o_ml, w_out, layer, tm, rows_per_mod):
    m, d = x2.shape
    tn = _tile(d, 1024)
    r = gate3.shape[1]
    kdim = w_out.shape[1]
    lhs = lambda a: pl.BlockSpec((tm, a.shape[1]), lambda i, j: (i, 0))
    return pl.pallas_call(
        _outproj_kernel,
        grid=(m // tm, d // tn),
        in_specs=[pl.BlockSpec((tm, tn), lambda i, j: (i, j)),
                  pl.BlockSpec((1, r, tn), lambda i, j: (i // rows_per_mod, 0, j)),
                  lhs(o_nsa), lhs(o_conv), lhs(o_ml),
                  pl.BlockSpec((1, kdim, tn), lambda i, j: (layer, 0, j))],
        out_specs=pl.BlockSpec((tm, tn), lambda i, j: (i, j)),
        out_shape=jax.ShapeDtypeStruct((m, d), F32),
        compiler_params=_cparams(("arbitrary", "arbitrary")),
        name="outproj",
    )(x2, gate3, o_nsa, o_conv, o_ml, w_out)


def _rmsnorm_kernel(x_ref, w_ref, o_ref):
    x = x_ref[...]
    o_ref[...] = x * lax.rsqrt(jnp.mean(x * x, axis=-1, keepdims=True) + RMS_EPS) * w_ref[...]


def _rmsnorm(x2, w):
    m, d = x2.shape
    tm = _tile(m, 256)
    return pl.pallas_call(
        _rmsnorm_kernel,
        grid=(m // tm,),
        in_specs=[pl.BlockSpec((tm, d), lambda i: (i, 0)), pl.BlockSpec((1, d), lambda i: (0, 0))],
        out_specs=pl.BlockSpec((tm, d), lambda i: (i, 0)),
        out_shape=jax.ShapeDtypeStruct((m, d), F32),
        compiler_params=_cparams(("arbitrary",)),
        name="final_norm",
    )(x2, w.reshape(1, d))


def _layer(x3, mod, lw, cfg, past, states):
    b, seq, d = x3.shape
    m = b * seq
    mo, nsa_w, kv_w, conv_w, ml_w = cfg['mo'], cfg['nsa_w'], cfg['kv_w'], cfg['conv_w'], cfg['ml_w']
    shift, scale, gate = mod
    x2 = x3.reshape(m, d)
    if seq >= 512:
        tm, rows_per_mod = 512, seq // 512
        tm_out = _tile(seq, 1024)
        per = lambda a: a.reshape(b, 1, d)
    else:
        tm, rows_per_mod = m, 1
        tm_out = m
        per = lambda a: jnp.repeat(a, seq, axis=0).reshape(1, m, d)
    h2 = _modnorm(x2, per(scale), per(shift), lw['norm_w'], tm, rows_per_mod)
    proj, small = _inproj(h2, lw['w_main'], lw['b_main'], lw['w_small'], lw['b_small'], lw['layer'])
    nm = proj.shape[1]
    proj3, small3 = proj.reshape(b, seq, nm), small.reshape(b, seq, LANES)

    pos = past + jnp.arange(seq, dtype=jnp.int32)
    cos, sin = _rope_tables(pos)
    qr, ksr, kwr, cmp_rows, sel_rows, win_rows = _rope(proj, jnp.tile(cos, (b, 1)), jnp.tile(sin, (b, 1)), mo,
                                                       nsa_w, kv_w)
    qr3, ksr3, kwr3 = qr.reshape(b, seq, nsa_w), ksr.reshape(b, seq, kv_w), kwr.reshape(b, seq, kv_w)

    if states is None:
        o_nsa = _nsa_prompt(proj3, small3, qr3, ksr3, kwr3, lw['cmp_pe'], lw['cmp_w'], mo, nsa_w)
        conv_buf = jnp.zeros((b, CONV_K - 1, conv_w), F32)
        dk = ml_w // MLSTM_HEADS
        c0 = jnp.zeros((b, MLSTM_HEADS, dk, dk), F32)
        n0 = jnp.zeros((b, MLSTM_HEADS, 1, dk), F32)
        m0 = jnp.zeros((b, MLSTM_HEADS, 1, 1), F32)
    else:
        n_pages, page = states['n_pages'], states['page']
        summ = _summaries(states['cache_cmp'], states['layer'], states['pt_flat'], lw['pe_ch'], b, n_pages,
                          kv_w // HEAD_DIM)
        o_cmp, sel = _cmpsel(summ, proj3, lw['cmp_w'], mo, nsa_w, past)
        o_nsa = _selwin(sel, states['pt_flat'], states['cache_sel'], states['layer'], qr3, ksr3,
                        states['win'], kwr3, small3, o_cmp, proj3, mo, nsa_w, past, n_pages, page)
        conv_buf = states['conv']
        dk = ml_w // MLSTM_HEADS
        c0 = states['C']
        n0 = states['n'].reshape(b, MLSTM_HEADS, 1, dk)
        m0 = states['m'].reshape(b, MLSTM_HEADS, 1, 1)

    o_conv, new_conv = _conv(proj3, conv_buf, lw['conv_w'], mo, conv_w)
    gates_row = jnp.swapaxes(small3[:, :, cfg['gate_i']:cfg['gate_i'] + 2 * MLSTM_HEADS], 1, 2)
    o_ml, c1, n1, m1 = _mlstm(proj3, small3, gates_row, c0, n0, m0, lw['mlstm_norm_w'], mo, ml_w,
                              cfg['gate_i'], cfg['gate_f'])

    y = _outproj(x2, per(gate), o_nsa.reshape(m, nsa_w), o_conv.reshape(m, conv_w), o_ml.reshape(m, ml_w),
                 lw['w_out'], lw['layer'], tm_out, max(seq // tm_out, 1))
    kv_shape = (b, seq, 2, kv_w // (2 * HEAD_DIM), HEAD_DIM)
    dk = ml_w // MLSTM_HEADS
    st = (cmp_rows.reshape(kv_shape), sel_rows.reshape(kv_shape), win_rows.reshape(kv_shape), new_conv, c1,
          n1.reshape(b, MLSTM_HEADS, dk), m1.reshape(b, MLSTM_HEADS))
    return y.reshape(b, seq, d), st


def kernel(x_prompt, x_sample, cache_cmp_kv, cache_sel_kv, state_win_kv, state_conv, state_mlstm_C, state_mlstm_n,
           state_mlstm_m, page_table, c_prompt, c_sample, norm_w, w_ada, b_ada, w_in, b_in, cmp_pe, cmp_w, conv_w,
           mlstm_norm_w, w_out, final_norm_w):
    bp, seq, d = x_prompt.shape
    bs, t_new, _ = x_sample.shape
    depth = w_in.shape[0]
    layout = _in_layout(d)
    sizes = dict(layout)
    in_off, _ = _offsets([n for n, _ in layout], sizes)
    mo, n_main = _offsets(MAIN_NAMES, sizes)
    so, n_small = _offsets(SMALL_NAMES, sizes)
    assert n_small <= LANES
    nsa_w, kv_w, conv_wd, ml_w = sizes['nsa_q'], sizes['kv_cmp'], sizes['conv_x'], sizes['m_q']
    cfg = dict(mo=mo, nsa_w=nsa_w, kv_w=kv_w, conv_w=conv_wd, ml_w=ml_w, gate_i=so['m_i'], gate_f=so['m_f'])
    n_kv = kv_w // (2 * HEAD_DIM)

    def gather_cols(a, names, pad_to):
        parts = [a[..., in_off[n]:in_off[n] + sizes[n]] for n in names]
        width = sum(sizes[n] for n in names)
        if pad_to > width:
            parts.append(jnp.zeros(a.shape[:-1] + (pad_to - width,), a.dtype))
        return jnp.concatenate(parts, axis=-1)

    segs_main = tuple((in_off[n], sizes[n], mo[n]) for n in MAIN_NAMES)
    w_main_all = _wprep(w_in, segs_main, n_main)
    w_small_all = gather_cols(w_in, SMALL_NAMES, LANES).astype(BF16)
    w_out_bf = w_out.astype(BF16)

    layers = []
    for l in range(depth):
        pe_ch = jnp.repeat(jnp.transpose(cmp_pe[l], (1, 0, 2)), n_kv, axis=1)
        layers.append(dict(
            layer=l, norm_w=norm_w[l].reshape(1, d),
            w_main=w_main_all,
            b_main=gather_cols(b_in[l], MAIN_NAMES, n_main).reshape(1, n_main),
            w_small=w_small_all,
            b_small=gather_cols(b_in[l], SMALL_NAMES, LANES).reshape(1, LANES),
            cmp_pe=cmp_pe[l], cmp_w=cmp_w[l], pe_ch=pe_ch, conv_w=conv_w[l],
            mlstm_norm_w=mlstm_norm_w[l].reshape(MLSTM_HEADS, 1, ml_w // MLSTM_HEADS),
            w_out=w_out_bf))

    n_rows = bp + bs
    pad_rows = -(-n_rows // 8) * 8
    c_all = jnp.concatenate([c_prompt, c_sample, jnp.zeros((pad_rows - n_rows, d), F32)], axis=0)
    mod = _ada(c_all, w_ada, b_ada)

    def mods(l, lo, hi):
        return tuple(mod[l, lo:hi, i * d:(i + 1) * d] for i in range(3))

    xp = x_prompt
    ps = []
    for l in range(depth):
        xp, st = _layer(xp, mods(l, 0, bp), layers[l], cfg, 0, None)
        ps.append(st)

    n_phys, page = cache_cmp_kv.shape[1], cache_cmp_kv.shape[2]
    n_pages = page_table.shape[1]
    past = n_pages * page
    assert past % SEL_BLOCK == 0 and t_new <= CMP_BLOCK and state_win_kv.shape[2] <= WINDOW
    pt_flat = page_table.reshape(-1).astype(jnp.int32)
    n_ch = 2 * n_kv
    cache_cmp4 = cache_cmp_kv.reshape(depth, n_phys, page * n_ch, HEAD_DIM)
    cache_sel3 = cache_sel_kv.reshape(depth, n_phys * page, n_ch, HEAD_DIM)
    win4 = state_win_kv.reshape(depth * bs, state_win_kv.shape[2] * n_ch, HEAD_DIM)
    xs = x_sample
    ss = []
    for l in range(depth):
        states = dict(layer=l, n_pages=n_pages, page=page, pt_flat=pt_flat, cache_cmp=cache_cmp4,
                      cache_sel=cache_sel3, win=win4, conv=state_conv[l], C=state_mlstm_C[l], n=state_mlstm_n[l],
                      m=state_mlstm_m[l])
        xs, st = _layer(xs, mods(l, bp, bp + bs), layers[l], cfg, past, states)
        ss.append(st)

    y_prompt = _rmsnorm(xp.reshape(bp * seq, d), final_norm_w).reshape(bp, seq, d)
    y_sample = _rmsnorm(xs.reshape(bs * t_new, d), final_norm_w).reshape(bs, t_new, d)
    n_keep = min(WINDOW, seq)
    stack = lambda sts, i: jnp.stack([s[i] for s in sts])
    p_win = jnp.stack([s[2][:, seq - n_keep:] for s in ps])
    return (y_prompt, y_sample, stack(ps, 0), stack(ps, 1), p_win, stack(ps, 3), stack(ps, 4), stack(ps, 5),
            stack(ps, 6), stack(ss, 0), stack(ss, 1), stack(ss, 2), stack(ss, 3), stack(ss, 4), stack(ss, 5),
            stack(ss, 6))
```

```python
import functools

import jax
import jax.numpy as jnp
from jax import lax
from jax.experimental import pallas as pl
from jax.experimental.pallas import tpu as pltpu

HEAD_DIM = 128
NSA_GROUP = 4
CMP_BLOCK = 32
SEL_BLOCK = 64
N_SELECT = 16
WINDOW = 512
FORCE_BONUS = 1.0e4
CONV_K = 3
MLSTM_HEADS = 4
ROPE_THETA = 10000.0
RMS_EPS = 1e-6
NEG_BIG = -1e30
TINY = 1e-30

LANES = 128
VMEM_LIMIT = 56 * 1024 * 1024

F32 = jnp.float32
BF16 = jnp.bfloat16

MAIN_NAMES = ('nsa_q', 'kv_cmp', 'kv_sel', 'kv_win', 'nsa_z', 'conv_x', 'conv_b', 'conv_c', 'conv_z',
              'm_q', 'm_k', 'm_v', 'm_o', 'm_z')
SMALL_NAMES = ('nsa_gate', 'm_i', 'm_f')


def _in_layout(d_model):
    nsa_w, kv_w, conv_w, ml_w = d_model // 2, d_model // 4, d_model // 4, d_model // 4
    nsa_heads = nsa_w // HEAD_DIM
    return (('nsa_q', nsa_w), ('kv_cmp', kv_w), ('kv_sel', kv_w), ('kv_win', kv_w),
            ('nsa_gate', 3 * nsa_heads), ('nsa_z', nsa_w),
            ('conv_x', conv_w), ('conv_b', conv_w), ('conv_c', conv_w), ('conv_z', conv_w),
            ('m_q', ml_w), ('m_k', ml_w), ('m_v', ml_w),
            ('m_i', MLSTM_HEADS), ('m_f', MLSTM_HEADS), ('m_o', ml_w), ('m_z', ml_w))


def _offsets(names, sizes):
    off, out = 0, {}
    for n in names:
        out[n] = off
        off += sizes[n]
    return out, off


def _cparams(sem):
    return pltpu.CompilerParams(dimension_semantics=sem, vmem_limit_bytes=VMEM_LIMIT)


def _tile(dim, pref):
    t = min(dim, pref)
    assert dim % t == 0, (dim, pref)
    return t


def _nt(a, b):
    return lax.dot_general(a, b, (((1,), (1,)), ((), ())), preferred_element_type=F32)


def _silu(x):
    return x * jax.nn.sigmoid(x)


def _lane_pick(x, col):
    lane = lax.broadcasted_iota(jnp.int32, x.shape, 1)
    return jnp.sum(jnp.where(lane == col, x, 0.0), axis=1, keepdims=True)


def _ada_kernel(c_ref, w_ref, b_ref, o_ref):
    a = _silu(c_ref[...]).astype(BF16)
    o_ref[0] = jnp.dot(a, w_ref[0].astype(BF16), preferred_element_type=F32) + b_ref[0]


def _ada(c_all, w_ada, b_ada):
    rows, d = c_all.shape
    depth, _, n = w_ada.shape
    tn = _tile(n, 512)
    return pl.pallas_call(
        _ada_kernel,
        grid=(depth, n // tn),
        in_specs=[pl.BlockSpec((rows, d), lambda l, j: (0, 0)),
                  pl.BlockSpec((1, d, tn), lambda l, j: (l, 0, j)),
                  pl.BlockSpec((1, 1, tn), lambda l, j: (l, 0, j))],
        out_specs=pl.BlockSpec((1, rows, tn), lambda l, j: (l, 0, j)),
        out_shape=jax.ShapeDtypeStruct((depth, rows, n), F32),
        compiler_params=_cparams(("arbitrary", "arbitrary")),
        name="ada",
    )(c_all, w_ada, b_ada.reshape(depth, 1, n))


SUBLANES = 8


def _wprep_kernel(tab_ref, wt_ref, wm_ref):
    del tab_ref
    wm_ref[0] = wt_ref[0].T.astype(wm_ref.dtype)


def _wprep(w_in, segs_main, n_main):
    depth, d, _ = w_in.shape
    tn = 512
    src_rows = []
    for src, size, dst in segs_main:
        assert size % tn == 0 and dst % tn == 0 and src % SUBLANES == 0
        src_rows += [(src + i) // SUBLANES for i in range(0, size, tn)]
    table = jnp.asarray(src_rows, jnp.int32)
    elem = lambda n: pl.Element(n)
    return pl.pallas_call(
        _wprep_kernel,
        grid_spec=pltpu.PrefetchScalarGridSpec(
            num_scalar_prefetch=1,
            grid=(depth, n_main // tn),
            in_specs=[pl.BlockSpec((elem(1), elem(tn), elem(d)), lambda l, j, tab: (l, tab[j] * SUBLANES, 0))],
            out_specs=pl.BlockSpec((1, d, tn), lambda l, j, tab: (l, 0, j))),
        out_shape=jax.ShapeDtypeStruct((depth, d, n_main), BF16),
        compiler_params=_cparams(("arbitrary", "arbitrary")),
        name="weight_prep",
    )(table, jnp.swapaxes(w_in, 1, 2))


def _modnorm_kernel(x_ref, sc_ref, sh_ref, nw_ref, h_ref):
    x = x_ref[...]
    y = x * lax.rsqrt(jnp.mean(x * x, axis=-1, keepdims=True) + RMS_EPS) * nw_ref[...]
    h_ref[...] = (y * (1.0 + sc_ref[0]) + sh_ref[0]).astype(h_ref.dtype)


def _modnorm(x2, scale3, shift3, nw, tm, rows_per_mod):
    m, d = x2.shape
    r = scale3.shape[1]
    mod_spec = pl.BlockSpec((1, r, d), lambda i: (i // rows_per_mod, 0, 0))
    return pl.pallas_call(
        _modnorm_kernel,
        grid=(m // tm,),
        in_specs=[pl.BlockSpec((tm, d), lambda i: (i, 0)), mod_spec, mod_spec,
                  pl.BlockSpec((1, d), lambda i: (0, 0))],
        out_specs=pl.BlockSpec((tm, d), lambda i: (i, 0)),
        out_shape=jax.ShapeDtypeStruct((m, d), BF16),
        compiler_params=_cparams(("arbitrary",)),
        name="modnorm",
    )(x2, scale3, shift3, nw)


def _inproj_kernel(h_ref, wm_ref, bm_ref, ws_ref, bs_ref, o_ref, os_ref):
    @pl.when(pl.program_id(1) == 0)
    def _():
        os_ref[...] = jnp.dot(h_ref[...], ws_ref[0], preferred_element_type=F32) + bs_ref[...]

    o_ref[...] = jnp.dot(h_ref[...], wm_ref[0], preferred_element_type=F32) + bm_ref[...]


def _inproj(h2, w_main, b_main, w_small, b_small, layer):
    m, d = h2.shape
    nm = w_main.shape[2]
    tm, tn = _tile(m, 1024), _tile(nm, 1024)
    return pl.pallas_call(
        _inproj_kernel,
        grid=(m // tm, nm // tn),
        in_specs=[pl.BlockSpec((tm, d), lambda i, j: (i, 0)),
                  pl.BlockSpec((1, d, tn), lambda i, j: (layer, 0, j)),
                  pl.BlockSpec((1, tn), lambda i, j: (0, j)),
                  pl.BlockSpec((1, d, LANES), lambda i, j: (layer, 0, 0)),
                  pl.BlockSpec((1, LANES), lambda i, j: (0, 0))],
        out_specs=[pl.BlockSpec((tm, tn), lambda i, j: (i, j)),
                   pl.BlockSpec((tm, LANES), lambda i, j: (i, 0))],
        out_shape=[jax.ShapeDtypeStruct((m, nm), F32), jax.ShapeDtypeStruct((m, LANES), F32)],
        compiler_params=_cparams(("arbitrary", "arbitrary")),
        name="inproj",
    )(h2, w_main, b_main, w_small, b_small)


def _rope_kernel(q_ref, kc_ref, ks_ref, kw_ref, cos_ref, sin_ref, qr_ref, ksr_ref, kwr_ref, co_ref, so_ref, wo_ref,
                 *, n_q, n_kv):
    cos, sin = cos_ref[...], sin_ref[...]
    tr, n_ch = q_ref.shape[0], 2 * n_kv

    def rot(x):
        return x * cos + pltpu.roll(x, HEAD_DIM // 2, axis=1) * sin

    for h in range(n_q):
        sl = slice(h * HEAD_DIM, (h + 1) * HEAD_DIM)
        qr_ref[:, sl] = (rot(q_ref[:, sl]) * HEAD_DIM ** -0.5).astype(qr_ref.dtype)
    for src, dense, state in ((kc_ref, None, co_ref), (ks_ref, ksr_ref, so_ref), (kw_ref, kwr_ref, wo_ref)):
        for ch in range(n_ch):
            sl = slice(ch * HEAD_DIM, (ch + 1) * HEAD_DIM)
            x = src[:, sl]
            if dense is not None and ch < n_kv:
                x = rot(x)
            if dense is not None:
                dense[:, sl] = x.astype(dense.dtype)
            state[pl.ds(ch, tr, stride=n_ch), :] = x


def _rope(proj, cos, sin, mo, nsa_w, kv_w):
    m = proj.shape[0]
    tr = _tile(m, 256)
    n_q, n_kv = nsa_w // HEAD_DIM, kv_w // (2 * HEAD_DIM)
    n_ch = 2 * n_kv
    kv_in = lambda name: pl.BlockSpec((tr, kv_w), lambda i: (i, mo[name] // kv_w))
    state_spec = pl.BlockSpec((tr * n_ch, HEAD_DIM), lambda i: (i, 0))
    state_sds = jax.ShapeDtypeStruct((m * n_ch, HEAD_DIM), F32)
    return pl.pallas_call(
        functools.partial(_rope_kernel, n_q=n_q, n_kv=n_kv),
        grid=(m // tr,),
        in_specs=[pl.BlockSpec((tr, nsa_w), lambda i: (i, mo['nsa_q'] // nsa_w)),
                  kv_in('kv_cmp'), kv_in('kv_sel'), kv_in('kv_win'),
                  pl.BlockSpec((tr, HEAD_DIM), lambda i: (i, 0)),
                  pl.BlockSpec((tr, HEAD_DIM), lambda i: (i, 0))],
        out_specs=[pl.BlockSpec((tr, nsa_w), lambda i: (i, 0)),
                   pl.BlockSpec((tr, kv_w), lambda i: (i, 0)),
                   pl.BlockSpec((tr, kv_w), lambda i: (i, 0)),
                   state_spec, state_spec, state_spec],
        out_shape=[jax.ShapeDtypeStruct((m, nsa_w), BF16),
                   jax.ShapeDtypeStruct((m, kv_w), BF16),
                   jax.ShapeDtypeStruct((m, kv_w), BF16),
                   state_sds, state_sds, state_sds],
        compiler_params=_cparams(("arbitrary",)),
        name="rope",
    )(proj, proj, proj, proj, cos, sin)


def _rope_tables(pos):
    half = HEAD_DIM // 2
    inv = ROPE_THETA ** (-jnp.arange(half, dtype=F32) / half)
    ang = pos.astype(F32)[:, None] * inv[None, :]
    cos, sin = jnp.cos(ang), jnp.sin(ang)
    return jnp.concatenate([cos, cos], axis=1), jnp.concatenate([-sin, sin], axis=1)


def _flash_step(carry, s, bias, v_bf):
    m, l, acc = carry
    if bias is not None:
        s = s + bias[None]
    m_new = jnp.maximum(m, jnp.max(s, axis=-1, keepdims=True))
    p = jnp.exp(s - m_new)
    alpha = jnp.exp(m - m_new)
    l = alpha * l + jnp.sum(p, axis=-1, keepdims=True)
    pv = jnp.dot(p.reshape(-1, p.shape[-1]).astype(BF16), v_bf, preferred_element_type=F32)
    acc = alpha * acc + pv.reshape(acc.shape)
    return m_new, l, acc


def _flash_init(shape_rows):
    return (jnp.full(shape_rows + (1,), NEG_BIG, F32), jnp.zeros(shape_rows + (1,), F32),
            jnp.zeros(shape_rows + (HEAD_DIM,), F32))


def _topk_rank(score, blk, n_blocks, axis=1):
    rank = jnp.zeros(score.shape, F32)
    for i in range(n_blocks):
        si = score[:, i:i + 1] if axis == 1 else score[i:i + 1, :]
        ge = jnp.where(si >= score, 1.0, 0.0)
        gt = jnp.where(si > score, 1.0, 0.0)
        rank = rank + jnp.where(blk > i, ge, gt)
    return rank


def _nsa_prompt_kernel(qc_ref, qr_ref, kc_ref, vc_ref, ks_ref, vs_ref, kw_ref, vw_ref, nz_ref, sm_ref,
                       pe_ref, wphi_ref, o_ref, kcs, vcs, *, seq, tq, tk, n_heads):
    kvh, qi = pl.program_id(1), pl.program_id(2)
    n_sel = seq // SEL_BLOCK
    n_cmp = 2 * n_sel
    grp = NSA_GROUP
    scale = HEAD_DIM ** -0.5

    @pl.when(qi == 0)
    def _():
        for c, (src, dst) in enumerate(((kc_ref, kcs), (vc_ref, vcs))):
            x = src[0].reshape(n_sel, 2 * CMP_BLOCK, HEAD_DIM)
            pe = pe_ref[c][None]
            ev = jnp.mean(x[:, :CMP_BLOCK, :] + pe, axis=1)
            od = jnp.mean(x[:, CMP_BLOCK:, :] + pe, axis=1)
            summ = jnp.concatenate([ev, od], axis=0).astype(BF16)
            dst[...] = jnp.dot(summ, wphi_ref[c].astype(BF16), preferred_element_type=F32)

    t0 = qi * tq
    pos = t0 + lax.broadcasted_iota(jnp.int32, (tq, 1), 0)
    pos_l = t0 + lax.broadcasted_iota(jnp.int32, (1, tq), 1)

    row_c = lax.broadcasted_iota(jnp.int32, (n_cmp, 1), 0)
    cblk = jnp.where(row_c < n_sel, 2 * row_c, 2 * (row_c - n_sel) + 1)
    cmp_ok = (cblk + 1) * CMP_BLOCK - 1 <= pos_l
    kcb, vcb = kcs[...].astype(BF16), vcs[...].astype(BF16)
    qc = qc_ref[0]
    imp = jnp.zeros((n_cmp, tq), F32)
    o_cmp = []
    for g in range(grp):
        s = _nt(kcb, qc[:, g * HEAD_DIM:(g + 1) * HEAD_DIM].astype(BF16)) * scale
        s = jnp.where(cmp_ok, s, NEG_BIG)
        s = s - jnp.max(s, axis=0, keepdims=True)
        p = jnp.where(cmp_ok, jnp.exp(s), 0.0)
        p = p / jnp.maximum(jnp.sum(p, axis=0, keepdims=True), TINY)
        imp = imp + p
        o_cmp.append(lax.dot_general(p.astype(BF16), vcb, (((0,), (0,)), ((), ())), preferred_element_type=F32))

    imp_sel = imp[:n_sel] + imp[n_sel:]
    blk = lax.broadcasted_iota(jnp.int32, (n_sel, 1), 0)
    q_blk = pos_l // SEL_BLOCK
    forced = (blk == 0) | (blk == q_blk) | (blk == q_blk - 1)
    score = jnp.where(forced, imp_sel + FORCE_BONUS, imp_sel)
    score = jnp.where(blk * SEL_BLOCK > pos_l, -jnp.inf, score)
    rank = _topk_rank(score, blk, n_sel, axis=0)
    sel_neg = jnp.concatenate([jnp.where(rank < N_SELECT, 0.0, NEG_BIG), jnp.zeros((LANES - n_sel, tq), F32)], axis=0)
    sel_neg = sel_neg.T.astype(BF16)

    qr = qr_ref[0]
    q4 = jnp.concatenate([qr[:, g * HEAD_DIM:(g + 1) * HEAD_DIM] for g in range(grp)], axis=0)
    q4_sel = jnp.concatenate([q4, jnp.concatenate([sel_neg] * grp, axis=0)], axis=1)

    def sel_keys(k0):
        kch = ks_ref[0, pl.ds(k0, tk), :].astype(BF16)
        kblk = (k0 + lax.broadcasted_iota(jnp.int32, (tk, 1), 0)) // SEL_BLOCK
        one_hot = jnp.where(kblk == lax.broadcasted_iota(jnp.int32, (1, LANES), 1), 1.0, 0.0).astype(BF16)
        return jnp.concatenate([kch, one_hot], axis=1), vs_ref[0, pl.ds(k0, tk), :].astype(BF16)

    def sel_step(k0, carry, bias):
        k_aug, vch = sel_keys(k0)
        return _flash_step(carry, _nt(q4_sel, k_aug).reshape(grp, tq, tk), bias, vch)

    n_chunks = (t0 + tq + tk - 1) // tk
    carry = lax.fori_loop(0, n_chunks - 1, lambda c, cr: sel_step(pl.multiple_of(c * tk, tk), cr, None),
                          _flash_init((grp, tq)))
    k_last = pl.multiple_of((n_chunks - 1) * tk, tk)
    kpos = k_last + lax.broadcasted_iota(jnp.int32, (1, tk), 1)
    _, l_s, acc_s = sel_step(k_last, carry, jnp.where(kpos <= pos, 0.0, NEG_BIG))

    def win_body(i, carry):
        k0 = pl.multiple_of((qi - i) * tq, tq)
        kpos = k0 + lax.broadcasted_iota(jnp.int32, (1, tq), 1)
        bias = jnp.where(jnp.where(kpos <= pos, pos - kpos, WINDOW) < WINDOW, 0.0, NEG_BIG)
        s = _nt(q4, kw_ref[0, pl.ds(k0, tq), :].astype(BF16)).reshape(grp, tq, tq)
        return _flash_step(carry, s, bias, vw_ref[0, pl.ds(k0, tq), :].astype(BF16))

    n_back = jnp.minimum(qi, (WINDOW + tq - 1) // tq)
    _, l_w, acc_w = lax.fori_loop(0, n_back + 1, win_body, _flash_init((grp, tq)))

    gates = jax.nn.sigmoid(sm_ref[0])
    nz = nz_ref[0]
    for g in range(grp):
        head = kvh * grp + g
        w_cmp = _lane_pick(gates, head)
        w_sel = _lane_pick(gates, n_heads + head) / jnp.maximum(l_s[g], TINY)
        w_win = _lane_pick(gates, 2 * n_heads + head) / jnp.maximum(l_w[g], TINY)
        o = w_cmp * o_cmp[g] + (w_sel * acc_s[g] + w_win * acc_w[g])
        sl = slice(g * HEAD_DIM, (g + 1) * HEAD_DIM)
        o_ref[0, :, sl] = (o * _silu(nz[:, sl])).astype(o_ref.dtype)


def _nsa_prompt(proj3, small3, qr3, ksr3, kwr3, pe, wphi, mo, nsa_w):
    b, seq, _ = proj3.shape
    n_heads = nsa_w // HEAD_DIM
    kvh = n_heads // NSA_GROUP
    gw = NSA_GROUP * HEAD_DIM
    tq = _tile(seq, 512)
    tk = _tile(seq, 512)
    assert seq % (2 * CMP_BLOCK) == 0 and tk % SEL_BLOCK == 0
    assert N_SELECT >= 3, "the forced blocks (first, current, previous) must always fit in the selection"
    assert seq // SEL_BLOCK <= LANES
    hb = lambda name: mo[name] // HEAD_DIM

    def head_spec(off_blocks):
        return pl.BlockSpec((1, seq, HEAD_DIM), lambda bi, k, qi: (bi, 0, off_blocks + k))

    return pl.pallas_call(
        functools.partial(_nsa_prompt_kernel, seq=seq, tq=tq, tk=tk, n_heads=n_heads),
        grid=(b, kvh, seq // tq),
        in_specs=[pl.BlockSpec((1, tq, gw), lambda bi, k, qi: (bi, qi, mo['nsa_q'] // gw + k)),
                  pl.BlockSpec((1, tq, gw), lambda bi, k, qi: (bi, qi, k)),
                  head_spec(hb('kv_cmp')), head_spec(hb('kv_cmp') + kvh),
                  head_spec(0), head_spec(kvh), head_spec(0), head_spec(kvh),
                  pl.BlockSpec((1, tq, gw), lambda bi, k, qi: (bi, qi, mo['nsa_z'] // gw + k)),
                  pl.BlockSpec((1, tq, LANES), lambda bi, k, qi: (bi, qi, 0)),
                  pl.BlockSpec((2, CMP_BLOCK, HEAD_DIM), lambda bi, k, qi: (0, 0, 0)),
                  pl.BlockSpec((2, HEAD_DIM, HEAD_DIM), lambda bi, k, qi: (0, 0, 0))],
        out_specs=pl.BlockSpec((1, tq, gw), lambda bi, k, qi: (bi, qi, k)),
        out_shape=jax.ShapeDtypeStruct((b, seq, nsa_w), BF16),
        scratch_shapes=[pltpu.VMEM((seq // CMP_BLOCK, HEAD_DIM), F32),
                        pltpu.VMEM((seq // CMP_BLOCK, HEAD_DIM), F32)],
        compiler_params=_cparams(("arbitrary", "arbitrary", "arbitrary")),
        name="nsa_prompt",
    )(proj3, qr3, proj3, proj3, ksr3, ksr3, kwr3, kwr3, proj3, small3, pe, wphi)


def _summ_kernel(pt_ref, *refs, n_pg, n_ch):
    del pt_ref
    x_refs, pe_ref, o_ref = refs[:n_pg], refs[n_pg], refs[n_pg + 1]
    pe = pe_ref[...][None]
    outs = []
    for r in x_refs:
        page = r[0, 0]
        n_blk = page.shape[0] // (CMP_BLOCK * n_ch)
        x = page.reshape(n_blk, CMP_BLOCK, n_ch, HEAD_DIM)
        outs.append(jnp.mean(x + pe, axis=1).reshape(n_blk * n_ch, HEAD_DIM))
    o_ref[0] = jnp.concatenate(outs, axis=0)


def _summaries(cache4, layer, pt_flat, pe_ch, n_batch, n_pages, n_ch):
    page_rows = cache4.shape[2]
    blk_per_page = page_rows // (CMP_BLOCK * n_ch)
    n_pg = _tile(n_pages, 8)
    rows = n_pg * blk_per_page * n_ch

    def page_spec(k):
        return pl.BlockSpec((1, 1, page_rows, HEAD_DIM),
                            lambda bi, j, pt: (layer, pt[bi * n_pages + j * n_pg + k], 0, 0))

    return pl.pallas_call(
        functools.partial(_summ_kernel, n_pg=n_pg, n_ch=n_ch),
        grid_spec=pltpu.PrefetchScalarGridSpec(
            num_scalar_prefetch=1,
            grid=(n_batch, n_pages // n_pg),
            in_specs=[page_spec(k) for k in range(n_pg)]
            + [pl.BlockSpec((CMP_BLOCK, n_ch, HEAD_DIM), lambda bi, j, pt: (0, 0, 0))],
            out_specs=pl.BlockSpec((1, rows, HEAD_DIM), lambda bi, j, pt: (bi, j, 0))),
        out_shape=jax.ShapeDtypeStruct((n_batch, n_pages * blk_per_page * n_ch, HEAD_DIM), F32),
        compiler_params=_cparams(("arbitrary", "arbitrary")),
        name="cmp_summaries",
    )(pt_flat, *([cache4] * n_pg), pe_ch)


def _cmpsel_kernel(summ_ref, q_ref, wphi_ref, ocmp_ref, sel_ref, *, past, t_new, kvh_n, n_selp):
    grp = NSA_GROUP
    scale = HEAD_DIM ** -0.5
    n_ch = 2 * kvh_n
    n_half = summ_ref.shape[1] // (2 * n_ch)
    n_sel = n_half + 1

    def summaries(ch, odd):
        return summ_ref[0, pl.ds(odd * n_ch + ch, n_half, stride=2 * n_ch), :]
    tpos = lax.broadcasted_iota(jnp.int32, (t_new, 1), 0) + past
    pos = jnp.concatenate([tpos] * grp, axis=0)
    lane = lax.broadcasted_iota(jnp.int32, (1, n_half), 1)
    ok_e = (2 * lane + 1) * CMP_BLOCK - 1 <= pos
    ok_o = (2 * lane + 2) * CMP_BLOCK - 1 <= pos
    w_k, w_v = wphi_ref[0].astype(BF16), wphi_ref[1].astype(BF16)
    q = q_ref[0]
    scores = []
    for k in range(kvh_n):
        proj = lambda ch, odd, w: jnp.dot(summaries(ch, odd).astype(BF16), w,
                                          preferred_element_type=F32).astype(BF16)
        k_e, k_o = proj(k, 0, w_k), proj(k, 1, w_k)
        v_e, v_o = proj(kvh_n + k, 0, w_v), proj(kvh_n + k, 1, w_v)
        qk = jnp.concatenate(
            [q[:, (k * grp + g) * HEAD_DIM:(k * grp + g + 1) * HEAD_DIM] for g in range(grp)], axis=0).astype(BF16)
        s_e = jnp.where(ok_e, _nt(qk, k_e) * scale, NEG_BIG)
        s_o = jnp.where(ok_o, _nt(qk, k_o) * scale, NEG_BIG)
        m = jnp.maximum(jnp.max(s_e, axis=-1, keepdims=True), jnp.max(s_o, axis=-1, keepdims=True))
        p_e = jnp.where(ok_e, jnp.exp(s_e - m), 0.0)
        p_o = jnp.where(ok_o, jnp.exp(s_o - m), 0.0)
        den = jnp.maximum(jnp.sum(p_e, axis=-1, keepdims=True) + jnp.sum(p_o, axis=-1, keepdims=True), TINY)
        p_e, p_o = p_e / den, p_o / den
        o = (jnp.dot(p_e.astype(BF16), v_e, preferred_element_type=F32)
             + jnp.dot(p_o.astype(BF16), v_o, preferred_element_type=F32))
        imp = jnp.zeros((t_new, n_half), F32)
        for g in range(grp):
            hsl = slice((k * grp + g) * HEAD_DIM, (k * grp + g + 1) * HEAD_DIM)
            ocmp_ref[0, :, hsl] = o[g * t_new:(g + 1) * t_new]
            imp = imp + (p_e[g * t_new:(g + 1) * t_new] + p_o[g * t_new:(g + 1) * t_new])
        tail_lane = lax.broadcasted_iota(jnp.int32, (t_new, n_selp - n_half), 1)
        tail = jnp.where(tail_lane == 0, 0.0, -jnp.inf)
        scores.append(jnp.concatenate([imp, tail], axis=1))
    score = jnp.concatenate(scores, axis=0)
    pos_r = jnp.concatenate([tpos] * kvh_n, axis=0)
    blk = lax.broadcasted_iota(jnp.int32, (1, n_selp), 1)
    q_blk = pos_r // SEL_BLOCK
    forced = (blk == 0) | (blk == q_blk) | (blk == q_blk - 1)
    score = jnp.where(forced, score + FORCE_BONUS, score)
    score = jnp.where((blk * SEL_BLOCK > pos_r) | (blk >= n_sel), -jnp.inf, score)
    rank = _topk_rank(score, blk, n_sel)

    n_slot = sel_ref.shape[3]
    n_top = min(N_SELECT, n_sel)
    blk_f = blk.astype(F32)
    blk_hi = jnp.floor(blk_f * (1.0 / 16.0))
    blk_lo = blk_f - 16.0 * blk_hi
    before = jnp.where(lax.broadcasted_iota(jnp.int32, (n_selp, 1), 0) < blk, 1.0, 0.0).astype(BF16)
    slot = lax.broadcasted_iota(jnp.int32, (n_slot, 1), 0).astype(F32)
    for k in range(kvh_n):
        member = jnp.where(rank[k * t_new:(k + 1) * t_new] < n_top, 1.0, 0.0)
        member_past = jnp.where(blk < n_half, member, 0.0)
        used = jnp.max(member_past, axis=0, keepdims=True)
        n_before = jnp.dot(jnp.broadcast_to(used, (SUBLANES, n_selp)).astype(BF16), before,
                           preferred_element_type=F32)[0:1]
        place = jnp.where(used > 0.5, jnp.where(n_before == slot, 1.0, 0.0), 0.0).astype(BF16)
        rows = jnp.concatenate([blk_hi, blk_lo, used, jnp.zeros((SUBLANES - 3, n_selp), F32), member_past], axis=0)
        packed = _nt(rows.astype(BF16), place)
        has_new = jnp.broadcast_to(member[:, n_half:n_half + 1], (t_new, n_slot))
        sel_ref[0, k] = jnp.concatenate([packed, has_new], axis=0)


def _cmpsel(summ, proj3, wphi, mo, nsa_w, past):
    b, summ_rows, _ = summ.shape
    t_new = proj3.shape[1]
    kvh_n = nsa_w // HEAD_DIM // NSA_GROUP
    n_half = summ_rows // (4 * kvh_n)
    assert n_half == past // SEL_BLOCK
    n_selp = -(-(n_half + 1) // LANES) * LANES
    n_slot = t_new * min(N_SELECT, n_half + 1)
    sel_rows = SUBLANES + 2 * t_new
    assert t_new == SUBLANES
    return pl.pallas_call(
        functools.partial(_cmpsel_kernel, past=past, t_new=t_new, kvh_n=kvh_n, n_selp=n_selp),
        grid=(b,),
        in_specs=[pl.BlockSpec((1, summ_rows, HEAD_DIM), lambda bi: (bi, 0, 0)),
                  pl.BlockSpec((1, t_new, nsa_w), lambda bi: (bi, 0, mo['nsa_q'] // nsa_w)),
                  pl.BlockSpec((2, HEAD_DIM, HEAD_DIM), lambda bi: (0, 0, 0))],
        out_specs=[pl.BlockSpec((1, t_new, nsa_w), lambda bi: (bi, 0, 0)),
                   pl.BlockSpec((1, kvh_n, sel_rows, n_slot), lambda bi: (bi, 0, 0, 0))],
        out_shape=[jax.ShapeDtypeStruct((b, t_new, nsa_w), F32),
                   jax.ShapeDtypeStruct((b, kvh_n, sel_rows, n_slot), F32)],
        compiler_params=_cparams(("arbitrary",)),
        name="cmp_select",
    )(summ, proj3, wphi)


def _two_part_attention(parts):
    m = None
    for s, msk, _ in parts:
        mx = jnp.max(jnp.where(msk, s, NEG_BIG), axis=-1, keepdims=True)
        m = mx if m is None else jnp.maximum(m, mx)
    den, acc = 0.0, 0.0
    for s, msk, v in parts:
        p = jnp.where(msk, jnp.exp(jnp.where(msk, s, NEG_BIG) - m), 0.0)
        den = den + jnp.sum(p, axis=-1, keepdims=True)
        acc = acc + jnp.dot(p.astype(BF16), v, preferred_element_type=F32)
    return acc / jnp.maximum(den, TINY)


SLOT_CHUNK = 16


def _selwin_kernel(idx_ref, cnt_ref, pt_ref, cache_ref, qr_ref, ksn_ref, vsn_ref, wbuf_ref, kwn_ref, vwn_ref,
                   sm_ref, ocmp_ref, nz_ref, sel_ref, o_ref, kbuf, vbuf, sem,
                   *, layer, past, t_new, kvh_n, n_pages, page, n_heads):
    bi, kvh = pl.program_id(0), pl.program_id(1)
    grp = NSA_GROUP
    n_ch = 2 * kvh_n
    n_slot = kbuf.shape[1]
    n_past_blk = past // SEL_BLOCK
    per_page = page // SEL_BLOCK
    step = bi * kvh_n + kvh
    n_steps = pl.num_programs(0) * kvh_n
    buf = step % 2

    def n_chunks_of(s):
        return (cnt_ref[s] + SLOT_CHUNK - 1) // SLOT_CHUNK

    def copies(half, i, row0, head):
        return (pltpu.make_async_copy(cache_ref.at[layer, pl.ds(row0, SEL_BLOCK), head, :], kbuf.at[half, i],
                                      sem.at[half, 0]),
                pltpu.make_async_copy(cache_ref.at[layer, pl.ds(row0, SEL_BLOCK), kvh_n + head, :], vbuf.at[half, i],
                                      sem.at[half, 1]))

    def gather(s, half):
        b_s, head = s // kvh_n, s % kvh_n

        def issue(i, carry):
            idc = jnp.minimum(idx_ref[s * n_slot + i], n_past_blk - 1)
            phys = pt_ref[b_s * n_pages + idc // per_page]
            row0 = pl.multiple_of(phys * page + (idc % per_page) * SEL_BLOCK, SEL_BLOCK)
            for cp in copies(half, i, row0, head):
                cp.start()
            return carry

        lax.fori_loop(0, n_chunks_of(s) * SLOT_CHUNK, issue, 0)

    @pl.when(step == 0)
    def _():
        gather(0, 0)

    @pl.when(step + 1 < n_steps)
    def _():
        gather(step + 1, 1 - buf)

    n_chunk = n_chunks_of(step)

    def wait(i, carry):
        for cp in copies(buf, i, 0, kvh):
            cp.wait()
        return carry

    lax.fori_loop(0, n_chunk * SLOT_CHUNK, wait, 0)

    qr = qr_ref[0]
    q4 = jnp.concatenate([qr[:, g * HEAD_DIM:(g + 1) * HEAD_DIM] for g in range(grp)], axis=0)
    rows = grp * t_new
    row_t = jnp.concatenate([lax.broadcasted_iota(jnp.int32, (t_new, 1), 0)] * grp, axis=0)
    pos = row_t + past

    def padded(ref):
        x = ref[0].astype(F32)
        return jnp.concatenate([x, jnp.zeros((SEL_BLOCK - t_new, HEAD_DIM), F32)], axis=0).astype(BF16)

    new_j = lax.broadcasted_iota(jnp.int32, (1, SEL_BLOCK), 1)
    new_pos = past + new_j
    new_real = new_j < t_new

    flags = sel_ref[0, 0]
    uses_slot = jnp.concatenate([flags[SUBLANES:SUBLANES + t_new]] * grp, axis=0).astype(BF16)
    has_new = jnp.concatenate([flags[SUBLANES + t_new:SUBLANES + 2 * t_new, 0:1]] * grp, axis=0)
    width = SLOT_CHUNK * SEL_BLOCK

    def sel_body(c, carry):
        s0 = pl.multiple_of(c * SLOT_CHUNK, SLOT_CHUNK)
        k_c = kbuf[buf, pl.ds(s0, SLOT_CHUNK)].reshape(width, HEAD_DIM).astype(BF16)
        v_c = vbuf[buf, pl.ds(s0, SLOT_CHUNK)].reshape(width, HEAD_DIM).astype(BF16)
        slot_of_key = s0 + lax.broadcasted_iota(jnp.int32, (1, width), 1) // SEL_BLOCK
        expand = jnp.where(slot_of_key == lax.broadcasted_iota(jnp.int32, (n_slot, 1), 0), 1.0, 0.0).astype(BF16)
        bias = jnp.where(jnp.dot(uses_slot, expand, preferred_element_type=F32) > 0.5, 0.0, NEG_BIG)
        return _flash_step(carry, _nt(q4, k_c)[None], bias, v_c)

    carry = lax.fori_loop(0, n_chunk, sel_body, _flash_init((1, rows)))
    bias_new = jnp.where(jnp.where(new_real & (new_pos <= pos), has_new, 0.0) > 0.5, 0.0, NEG_BIG)
    _, l_sel, acc_sel = _flash_step(carry, _nt(q4, padded(ksn_ref))[None], bias_new, padded(vsn_ref))
    o_sel = acc_sel[0] / jnp.maximum(l_sel[0], TINY)

    n_buf = wbuf_ref.shape[1] // n_ch
    kwb = wbuf_ref[0, pl.ds(kvh, n_buf, stride=n_ch), :].astype(BF16)
    vwb = wbuf_ref[0, pl.ds(kvh_n + kvh, n_buf, stride=n_ch), :].astype(BF16)
    buf_pos = past - n_buf + lax.broadcasted_iota(jnp.int32, (1, n_buf), 1)
    msk_buf = jnp.where(buf_pos <= pos, pos - buf_pos, WINDOW) < WINDOW
    msk_wn = jnp.where(new_real & (new_pos <= pos), pos - new_pos, WINDOW) < WINDOW
    o_win = _two_part_attention([(_nt(q4, kwb), msk_buf, vwb),
                                 (_nt(q4, padded(kwn_ref)), msk_wn, padded(vwn_ref))])

    gates = jax.nn.sigmoid(sm_ref[0])
    ocmp, nz = ocmp_ref[0], nz_ref[0]
    for g in range(grp):
        head = kvh * grp + g
        rs = slice(g * t_new, (g + 1) * t_new)
        sl = slice(g * HEAD_DIM, (g + 1) * HEAD_DIM)
        o = (_lane_pick(gates, head) * ocmp[:, sl]
             + (_lane_pick(gates, n_heads + head) * o_sel[rs] + _lane_pick(gates, 2 * n_heads + head) * o_win[rs]))
        o_ref[0, :, sl] = (o * _silu(nz[:, sl])).astype(o_ref.dtype)


def _selwin(sel, pt_flat, cache4, layer, qr3, ksr3, win_state3, kwr3, small3, ocmp3, proj3, mo, nsa_w,
            past, n_pages, page):
    b, t_new, _ = qr3.shape
    n_heads = nsa_w // HEAD_DIM
    kvh_n = n_heads // NSA_GROUP
    gw = NSA_GROUP * HEAD_DIM
    sel_rows, n_slot = sel.shape[2], sel.shape[3]
    assert n_slot % SLOT_CHUNK == 0
    slot_blocks = (16.0 * sel[:, :, 0] + sel[:, :, 1]).astype(jnp.int32).reshape(-1)
    slot_count = jnp.sum(sel[:, :, 2], axis=-1).astype(jnp.int32).reshape(-1)
    win_rows = win_state3.shape[1]
    new_k = pl.BlockSpec((1, t_new, HEAD_DIM), lambda bi, k, *_: (bi, 0, k))
    new_v = pl.BlockSpec((1, t_new, HEAD_DIM), lambda bi, k, *_: (bi, 0, kvh_n + k))
    return pl.pallas_call(
        functools.partial(_selwin_kernel, layer=layer, past=past, t_new=t_new, kvh_n=kvh_n, n_pages=n_pages,
                          page=page, n_heads=n_heads),
        grid_spec=pltpu.PrefetchScalarGridSpec(
            num_scalar_prefetch=3,
            grid=(b, kvh_n),
            in_specs=[pl.BlockSpec(memory_space=pl.ANY),
                      pl.BlockSpec((1, t_new, gw), lambda bi, k, *_: (bi, 0, k)),
                      new_k, new_v,
                      pl.BlockSpec((1, win_rows, HEAD_DIM), lambda bi, k, *_: (layer * b + bi, 0, 0)),
                      new_k, new_v,
                      pl.BlockSpec((1, t_new, LANES), lambda bi, k, *_: (bi, 0, 0)),
                      pl.BlockSpec((1, t_new, gw), lambda bi, k, *_: (bi, 0, k)),
                      pl.BlockSpec((1, t_new, gw), lambda bi, k, *_: (bi, 0, mo['nsa_z'] // gw + k)),
                      pl.BlockSpec((1, 1, sel_rows, n_slot), lambda bi, k, *_: (bi, k, 0, 0))],
            out_specs=pl.BlockSpec((1, t_new, gw), lambda bi, k, *_: (bi, 0, k)),
            scratch_shapes=[pltpu.VMEM((2, n_slot, SEL_BLOCK, HEAD_DIM), F32),
                            pltpu.VMEM((2, n_slot, SEL_BLOCK, HEAD_DIM), F32),
                            pltpu.SemaphoreType.DMA((2, 2))]),
        out_shape=jax.ShapeDtypeStruct((b, t_new, nsa_w), BF16),
        compiler_params=_cparams(("arbitrary", "arbitrary")),
        name="sel_win_sample",
    )(slot_blocks, slot_count, pt_flat, cache4, qr3, ksr3, ksr3, win_state3, kwr3, kwr3, small3, ocmp3, proj3, sel)


def _conv_kernel(x_ref, b_ref, c_ref, z_ref, buf_ref, w_ref, o_ref, st_ref, prev):
    tc = x_ref.shape[1]

    @pl.when(pl.program_id(1) == 0)
    def _():
        prev[0:CONV_K - 1, :] = buf_ref[0]

    u = c_ref[0] * x_ref[0]
    row = lax.broadcasted_iota(jnp.int32, (tc, 1), 0)
    w = w_ref[...]
    y = u * w[CONV_K - 1:CONV_K, :]
    for back in range(1, CONV_K):
        shifted = pltpu.roll(u, back, axis=0)
        for r in range(back):
            shifted = jnp.where(row == r, prev[CONV_K - 1 - back + r:CONV_K - back + r, :], shifted)
        y = y + shifted * w[CONV_K - 1 - back:CONV_K - back, :]
    o_ref[0] = (b_ref[0] * y * _silu(z_ref[0])).astype(o_ref.dtype)
    tail = u[tc - (CONV_K - 1):, :]
    prev[0:CONV_K - 1, :] = tail
    st_ref[0] = tail


def _conv(proj3, buf, w, mo, conv_w):
    b, seq, _ = proj3.shape
    tc = _tile(seq, 512)
    assert tc >= CONV_K - 1
    col = lambda name: pl.BlockSpec((1, tc, conv_w), lambda bi, ti: (bi, ti, mo[name] // conv_w))
    return pl.pallas_call(
        _conv_kernel,
        grid=(b, seq // tc),
        in_specs=[col('conv_x'), col('conv_b'), col('conv_c'), col('conv_z'),
                  pl.BlockSpec((1, CONV_K - 1, conv_w), lambda bi, ti: (bi, 0, 0)),
                  pl.BlockSpec((CONV_K, conv_w), lambda bi, ti: (0, 0))],
        out_specs=[pl.BlockSpec((1, tc, conv_w), lambda bi, ti: (bi, ti, 0)),
                   pl.BlockSpec((1, CONV_K - 1, conv_w), lambda bi, ti: (bi, 0, 0))],
        out_shape=[jax.ShapeDtypeStruct((b, seq, conv_w), BF16),
                   jax.ShapeDtypeStruct((b, CONV_K - 1, conv_w), F32)],
        scratch_shapes=[pltpu.VMEM((8, conv_w), F32)],
        compiler_params=_cparams(("arbitrary", "arbitrary")),
        name="short_conv",
    )(proj3, proj3, proj3, proj3, buf, w)


def _log_sigmoid(x):
    return jnp.minimum(x, 0.0) - jnp.log(1.0 + jnp.exp(-jnp.abs(x)))


def _mlstm_kernel(q_ref, k_ref, v_ref, og_ref, z_ref, sm_ref, gr_ref, c0_ref, n0_ref, m0_ref, nw_ref,
                  o_ref, c_out, n_out, m_out, c_s, n_s, m_s, *, n_chunks, gate_i, gate_f):
    ci = pl.program_id(1)

    @pl.when(ci == 0)
    def _():
        c_s[...] = c0_ref[0]
        n_s[...] = n0_ref[0]
        m_s[...] = m0_ref[0]

    small, gr = sm_ref[0], gr_ref[0]
    dk = c_s.shape[1]
    for h in range(MLSTM_HEADS):
        hs = slice(h * dk, (h + 1) * dk)
        c_new, n_new, m_new, out = _mlstm_head(
            q_ref[0, :, hs], k_ref[0, :, hs], v_ref[0, :, hs], og_ref[0, :, hs], z_ref[0, :, hs],
            small[:, gate_i + h:gate_i + h + 1], small[:, gate_f + h:gate_f + h + 1],
            gr[h:h + 1, :], gr[MLSTM_HEADS + h:MLSTM_HEADS + h + 1, :],
            c_s[h], n_s[h], m_s[h], nw_ref[h])
        c_s[h] = c_new
        n_s[h] = n_new
        m_s[h] = m_new
        o_ref[0, :, hs] = out.astype(o_ref.dtype)

    @pl.when(ci == n_chunks - 1)
    def _():
        c_out[0] = c_s[...]
        n_out[0] = n_s[...]
        m_out[0] = m_s[...]


def _mlstm_head(q, k, v, og, z, i_col, f_col, i_row, f_row, c_prev, n_prev, m_prev, norm_w):
    lc, dk = q.shape
    lf_col, lf_row = _log_sigmoid(f_col), _log_sigmoid(f_row)
    ti = lax.broadcasted_iota(jnp.int32, (lc, 1), 0)
    si = lax.broadcasted_iota(jnp.int32, (1, lc), 1)
    causal = si <= ti
    b_col = jnp.sum(jnp.where(causal, lf_row, 0.0), axis=1, keepdims=True)
    b_row = jnp.sum(jnp.where(ti <= si, lf_col, 0.0), axis=0, keepdims=True)
    log_d = jnp.where(causal, b_col - b_row + i_row, -jnp.inf)
    m_t = jnp.maximum(b_col + m_prev, jnp.max(log_d, axis=1, keepdims=True))
    inter = jnp.exp(b_col + m_prev - m_t)
    kf = k * (dk ** -0.5)
    qb, kb, vb = q.astype(BF16), kf.astype(BF16), v.astype(BF16)
    s = _nt(qb, kb) * jnp.exp(log_d - m_t)
    num = (inter * jnp.dot(qb, c_prev.astype(BF16), preferred_element_type=F32)
           + jnp.dot(s.astype(BF16), vb, preferred_element_type=F32))
    den = inter * jnp.sum(q * n_prev, axis=1, keepdims=True) + jnp.sum(s, axis=1, keepdims=True)
    hid = num / jnp.maximum(jnp.abs(den), jnp.exp(-m_t))

    m_new = m_t[lc - 1:lc, :]
    b_last = b_col[lc - 1:lc, :]
    wdec = jnp.exp(b_last - b_col + i_col - m_new)
    cdec = jnp.exp(b_last + m_prev - m_new)
    c_new = cdec * c_prev + lax.dot_general(kb, (wdec * v).astype(BF16), (((0,), (0,)), ((), ())),
                                            preferred_element_type=F32)
    n_new = cdec * n_prev + jnp.sum(wdec * kf, axis=0, keepdims=True)
    hn = hid * lax.rsqrt(jnp.mean(hid * hid, axis=-1, keepdims=True) + RMS_EPS) * norm_w
    return c_new, n_new, m_new, jax.nn.sigmoid(og) * hn * _silu(z)


def _mlstm(proj3, small3, gates_row, c0, n0, m0, norm_w, mo, ml_w, gate_i, gate_f):
    b, seq, _ = proj3.shape
    nh = MLSTM_HEADS
    dk = ml_w // nh
    lc = _tile(seq, 256)
    n_chunks = seq // lc
    col = lambda name: pl.BlockSpec((1, lc, ml_w), lambda bi, ci: (bi, ci, mo[name] // ml_w))
    state = lambda shape: pl.BlockSpec((1, nh) + shape, lambda bi, ci: (bi, 0, 0, 0))
    return pl.pallas_call(
        functools.partial(_mlstm_kernel, n_chunks=n_chunks, gate_i=gate_i, gate_f=gate_f),
        grid=(b, n_chunks),
        in_specs=[col('m_q'), col('m_k'), col('m_v'), col('m_o'), col('m_z'),
                  pl.BlockSpec((1, lc, LANES), lambda bi, ci: (bi, ci, 0)),
                  pl.BlockSpec((1, 2 * nh, lc), lambda bi, ci: (bi, 0, ci)),
                  state((dk, dk)), state((1, dk)), state((1, 1)),
                  pl.BlockSpec((nh, 1, dk), lambda bi, ci: (0, 0, 0))],
        out_specs=[pl.BlockSpec((1, lc, ml_w), lambda bi, ci: (bi, ci, 0)),
                   state((dk, dk)), state((1, dk)), state((1, 1))],
        out_shape=[jax.ShapeDtypeStruct((b, seq, ml_w), BF16),
                   jax.ShapeDtypeStruct((b, nh, dk, dk), F32),
                   jax.ShapeDtypeStruct((b, nh, 1, dk), F32),
                   jax.ShapeDtypeStruct((b, nh, 1, 1), F32)],
        scratch_shapes=[pltpu.VMEM((nh, dk, dk), F32), pltpu.VMEM((nh, 1, dk), F32), pltpu.VMEM((nh, 1, 1), F32)],
        compiler_params=_cparams(("arbitrary", "arbitrary")),
        name="mlstm",
    )(proj3, proj3, proj3, proj3, proj3, small3, gates_row, c0, n0, m0, norm_w)


def _outproj_kernel(x_ref, g_ref, a_ref, c_ref, m_ref, w_ref, o_ref):
    ka, kc = a_ref.shape[1], c_ref.shape[1]
    acc = jnp.dot(a_ref[...], w_ref[0, 0:ka, :], preferred_element_type=F32)
    acc = acc + jnp.dot(c_ref[...], w_ref[0, ka:ka + kc, :], preferred_element_type=F32)
    acc = acc + jnp.dot(m_ref[...], w_ref[0, ka + kc:, :], preferred_element_type=F32)
    o_ref[...] = x_ref[...] + g_ref[0] * acc


def _outproj(x2, gate3, o_nsa, o_conv, o_ml, w_out, layer, tm, rows_per_mod):
    m, d = x2.shape
    tn = _tile(d, 1024)
    r = gate3.shape[1]
    kdim = w_out.shape[1]
    lhs = lambda a: pl.BlockSpec((tm, a.shape[1]), lambda i, j: (i, 0))
    return pl.pallas_call(
        _outproj_kernel,
        grid=(m // tm, d // tn),
        in_specs=[pl.BlockSpec((tm, tn), lambda i, j: (i, j)),
                  pl.BlockSpec((1, r, tn), lambda i, j: (i // rows_per_mod, 0, j)),
                  lhs(o_nsa), lhs(o_conv), lhs(o_ml),
                  pl.BlockSpec((1, kdim, tn), lambda i, j: (layer, 0, j))],
        out_specs=pl.BlockSpec((tm, tn), lambda i, j: (i, j)),
        out_shape=jax.ShapeDtypeStruct((m, d), F32),
        compiler_params=_cparams(("arbitrary", "arbitrary")),
        name="outproj",
    )(x2, gate3, o_nsa, o_conv, o_ml, w_out)


def _rmsnorm_kernel(x_ref, w_ref, o_ref):
    x = x_ref[...]
    o_ref[...] = x * lax.rsqrt(jnp.mean(x * x, axis=-1, keepdims=True) + RMS_EPS) * w_ref[...]


def _rmsnorm(x2, w):
    m, d = x2.shape
    tm = _tile(m, 256)
    return pl.pallas_call(
        _rmsnorm_kernel,
        grid=(m // tm,),
        in_specs=[pl.BlockSpec((tm, d), lambda i: (i, 0)), pl.BlockSpec((1, d), lambda i: (0, 0))],
        out_specs=pl.BlockSpec((tm, d), lambda i: (i, 0)),
        out_shape=jax.ShapeDtypeStruct((m, d), F32),
        compiler_params=_cparams(("arbitrary",)),
        name="final_norm",
    )(x2, w.reshape(1, d))


def _layer(x3, mod, lw, cfg, past, states):
    b, seq, d = x3.shape
    m = b * seq
    mo, nsa_w, kv_w, conv_w, ml_w = cfg['mo'], cfg['nsa_w'], cfg['kv_w'], cfg['conv_w'], cfg['ml_w']
    shift, scale, gate = mod
    x2 = x3.reshape(m, d)
    if seq >= 512:
        tm, rows_per_mod = 512, seq // 512
        tm_out = _tile(seq, 1024)
        per = lambda a: a.reshape(b, 1, d)
    else:
        tm, rows_per_mod = m, 1
        tm_out = m
        per = lambda a: jnp.repeat(a, seq, axis=0).reshape(1, m, d)
    h2 = _modnorm(x2, per(scale), per(shift), lw['norm_w'], tm, rows_per_mod)
    proj, small = _inproj(h2, lw['w_main'], lw['b_main'], lw['w_small'], lw['b_small'], lw['layer'])
    nm = proj.shape[1]
    proj3, small3 = proj.reshape(b, seq, nm), small.reshape(b, seq, LANES)

    pos = past + jnp.arange(seq, dtype=jnp.int32)
    cos, sin = _rope_tables(pos)
    qr, ksr, kwr, cmp_rows, sel_rows, win_rows = _rope(proj, jnp.tile(cos, (b, 1)), jnp.tile(sin, (b, 1)), mo,
                                                       nsa_w, kv_w)
    qr3, ksr3, kwr3 = qr.reshape(b, seq, nsa_w), ksr.reshape(b, seq, kv_w), kwr.reshape(b, seq, kv_w)

    if states is None:
        o_nsa = _nsa_prompt(proj3, small3, qr3, ksr3, kwr3, lw['cmp_pe'], lw['cmp_w'], mo, nsa_w)
        conv_buf = jnp.zeros((b, CONV_K - 1, conv_w), F32)
        dk = ml_w // MLSTM_HEADS
        c0 = jnp.zeros((b, MLSTM_HEADS, dk, dk), F32)
        n0 = jnp.zeros((b, MLSTM_HEADS, 1, dk), F32)
        m0 = jnp.zeros((b, MLSTM_HEADS, 1, 1), F32)
    else:
        n_pages, page = states['n_pages'], states['page']
        summ = _summaries(states['cache_cmp'], states['layer'], states['pt_flat'], lw['pe_ch'], b, n_pages,
                          kv_w // HEAD_DIM)
        o_cmp, sel = _cmpsel(summ, proj3, lw['cmp_w'], mo, nsa_w, past)
        o_nsa = _selwin(sel, states['pt_flat'], states['cache_sel'], states['layer'], qr3, ksr3,
                        states['win'], kwr3, small3, o_cmp, proj3, mo, nsa_w, past, n_pages, page)
        conv_buf = states['conv']
        dk = ml_w // MLSTM_HEADS
        c0 = states['C']
        n0 = states['n'].reshape(b, MLSTM_HEADS, 1, dk)
        m0 = states['m'].reshape(b, MLSTM_HEADS, 1, 1)

    o_conv, new_conv = _conv(proj3, conv_buf, lw['conv_w'], mo, conv_w)
    gates_row = jnp.swapaxes(small3[:, :, cfg['gate_i']:cfg['gate_i'] + 2 * MLSTM_HEADS], 1, 2)
    o_ml, c1, n1, m1 = _mlstm(proj3, small3, gates_row, c0, n0, m0, lw['mlstm_norm_w'], mo, ml_w,
                              cfg['gate_i'], cfg['gate_f'])

    y = _outproj(x2, per(gate), o_nsa.reshape(m, nsa_w), o_conv.reshape(m, conv_w), o_ml.reshape(m, ml_w),
                 lw['w_out'], lw['layer'], tm_out, max(seq // tm_out, 1))
    kv_shape = (b, seq, 2, kv_w // (2 * HEAD_DIM), HEAD_DIM)
    dk = ml_w // MLSTM_HEADS
    st = (cmp_rows.reshape(kv_shape), sel_rows.reshape(kv_shape), win_rows.reshape(kv_shape), new_conv, c1,
          n1.reshape(b, MLSTM_HEADS, dk), m1.reshape(b, MLSTM_HEADS))
    return y.reshape(b, seq, d), st


def kernel(x_prompt, x_sample, cache_cmp_kv, cache_sel_kv, state_win_kv, state_conv, state_mlstm_C, state_mlstm_n,
           state_mlstm_m, page_table, c_prompt, c_sample, norm_w, w_ada, b_ada, w_in, b_in, cmp_pe, cmp_w, conv_w,
           mlstm_norm_w, w_out, final_norm_w):
    bp, seq, d = x_prompt.shape
    bs, t_new, _ = x_sample.shape
    depth = w_in.shape[0]
    layout = _in_layout(d)
    sizes = dict(layout)
    in_off, _ = _offsets([n for n, _ in layout], sizes)
    mo, n_main = _offsets(MAIN_NAMES, sizes)
    so, n_small = _offsets(SMALL_NAMES, sizes)
    assert n_small <= LANES
    nsa_w, kv_w, conv_wd, ml_w = sizes['nsa_q'], sizes['kv_cmp'], sizes['conv_x'], sizes['m_q']
    cfg = dict(mo=mo, nsa_w=nsa_w, kv_w=kv_w, conv_w=conv_wd, ml_w=ml_w, gate_i=so['m_i'], gate_f=so['m_f'])
    n_kv = kv_w // (2 * HEAD_DIM)

    def gather_cols(a, names, pad_to):
        parts = [a[..., in_off[n]:in_off[n] + sizes[n]] for n in names]
        width = sum(sizes[n] for n in names)
        if pad_to > width:
            parts.append(jnp.zeros(a.shape[:-1] + (pad_to - width,), a.dtype))
        return jnp.concatenate(parts, axis=-1)

    segs_main = tuple((in_off[n], sizes[n], mo[n]) for n in MAIN_NAMES)
    w_main_all = _wprep(w_in, segs_main, n_main)
    w_small_all = gather_cols(w_in, SMALL_NAMES, LANES).astype(BF16)
    w_out_bf = w_out.astype(BF16)

    layers = []
    for l in range(depth):
        pe_ch = jnp.repeat(jnp.transpose(cmp_pe[l], (1, 0, 2)), n_kv, axis=1)
        layers.append(dict(
            layer=l, norm_w=norm_w[l].reshape(1, d),
            w_main=w_main_all,
            b_main=gather_cols(b_in[l], MAIN_NAMES, n_main).reshape(1, n_main),
            w_small=w_small_all,
            b_small=gather_cols(b_in[l], SMALL_NAMES, LANES).reshape(1, LANES),
            cmp_pe=cmp_pe[l], cmp_w=cmp_w[l], pe_ch=pe_ch, conv_w=conv_w[l],
            mlstm_norm_w=mlstm_norm_w[l].reshape(MLSTM_HEADS, 1, ml_w // MLSTM_HEADS),
            w_out=w_out_bf))

    n_rows = bp + bs
    pad_rows = -(-n_rows // 8) * 8
    c_all = jnp.concatenate([c_prompt, c_sample, jnp.zeros((pad_rows - n_rows, d), F32)], axis=0)
    mod = _ada(c_all, w_ada, b_ada)

    def mods(l, lo, hi):
        return tuple(mod[l, lo:hi, i * d:(i + 1) * d] for i in range(3))

    xp = x_prompt
    ps = []
    for l in range(depth):
        xp, st = _layer(xp, mods(l, 0, bp), layers[l], cfg, 0, None)
        ps.append(st)

    n_phys, page = cache_cmp_kv.shape[1], cache_cmp_kv.shape[2]
    n_pages = page_table.shape[1]
    past = n_pages * page
    assert past % SEL_BLOCK == 0 and t_new <= CMP_BLOCK and state_win_kv.shape[2] <= WINDOW
    pt_flat = page_table.reshape(-1).astype(jnp.int32)
    n_ch = 2 * n_kv
    cache_cmp4 = cache_cmp_kv.reshape(depth, n_phys, page * n_ch, HEAD_DIM)
    cache_sel3 = cache_sel_kv.reshape(depth, n_phys * page, n_ch, HEAD_DIM)
    win4 = state_win_kv.reshape(depth * bs, state_win_kv.shape[2] * n_ch, HEAD_DIM)
    xs = x_sample
    ss = []
    for l in range(depth):
        states = dict(layer=l, n_pages=n_pages, page=page, pt_flat=pt_flat, cache_cmp=cache_cmp4,
                      cache_sel=cache_sel3, win=win4, conv=state_conv[l], C=state_mlstm_C[l], n=state_mlstm_n[l],
                      m=state_mlstm_m[l])
        xs, st = _layer(xs, mods(l, bp, bp + bs), layers[l], cfg, past, states)
        ss.append(st)

    y_prompt = _rmsnorm(xp.reshape(bp * seq, d), final_norm_w).reshape(bp, seq, d)
    y_sample = _rmsnorm(xs.reshape(bs * t_new, d), final_norm_w).reshape(bs, t_new, d)
    n_keep = min(WINDOW, seq)
    stack = lambda sts, i: jnp.stack([s[i] for s in sts])
    p_win = jnp.stack([s[2][:, seq - n_keep:] for s in ps])
    return (y_prompt, y_sample, stack(ps, 0), stack(ps, 1), p_win, stack(ps, 3), stack(ps, 4), stack(ps, 5),
            stack(ps, 6), stack(ss, 0), stack(ss, 1), stack(ss, 2), stack(ss, 3), stack(ss, 4), stack(ss, 5),
            stack(ss, 6))
```

```python
import functools

import jax
import jax.numpy as jnp
from jax import lax
from jax.experimental import pallas as pl
from jax.experimental.pallas import tpu as pltpu

HEAD_DIM = 128
NSA_GROUP = 4
CMP_BLOCK = 32
SEL_BLOCK = 64
N_SELECT = 16
WINDOW = 512
FORCE_BONUS = 1.0e4
CONV_K = 3
MLSTM_HEADS = 4
ROPE_THETA = 10000.0
RMS_EPS = 1e-6
NEG_BIG = -1e30
TINY = 1e-30

LANES = 128
VMEM_LIMIT = 56 * 1024 * 1024

F32 = jnp.float32
BF16 = jnp.bfloat16

MAIN_NAMES = ('nsa_q', 'kv_cmp', 'kv_sel', 'kv_win', 'nsa_z', 'conv_x', 'conv_b', 'conv_c', 'conv_z',
              'm_q', 'm_k', 'm_v', 'm_o', 'm_z')
SMALL_NAMES = ('nsa_gate', 'm_i', 'm_f')


def _in_layout(d_model):
    nsa_w, kv_w, conv_w, ml_w = d_model // 2, d_model // 4, d_model // 4, d_model // 4
    nsa_heads = nsa_w // HEAD_DIM
    return (('nsa_q', nsa_w), ('kv_cmp', kv_w), ('kv_sel', kv_w), ('kv_win', kv_w),
            ('nsa_gate', 3 * nsa_heads), ('nsa_z', nsa_w),
            ('conv_x', conv_w), ('conv_b', conv_w), ('conv_c', conv_w), ('conv_z', conv_w),
            ('m_q', ml_w), ('m_k', ml_w), ('m_v', ml_w),
            ('m_i', MLSTM_HEADS), ('m_f', MLSTM_HEADS), ('m_o', ml_w), ('m_z', ml_w))


def _offsets(names, sizes):
    off, out = 0, {}
    for n in names:
        out[n] = off
        off += sizes[n]
    return out, off


def _cparams(sem):
    return pltpu.CompilerParams(dimension_semantics=sem, vmem_limit_bytes=VMEM_LIMIT)


def _tile(dim, pref):
    t = min(dim, pref)
    assert dim % t == 0, (dim, pref)
    return t


def _nt(a, b):
    return lax.dot_general(a, b, (((1,), (1,)), ((), ())), preferred_element_type=F32)


def _silu(x):
    return x * jax.nn.sigmoid(x)


def _lane_pick(x, col):
    lane = lax.broadcasted_iota(jnp.int32, x.shape, 1)
    return jnp.sum(jnp.where(lane == col, x, 0.0), axis=1, keepdims=True)


def _ada_kernel(c_ref, w_ref, b_ref, o_ref):
    a = _silu(c_ref[...]).astype(BF16)
    o_ref[0] = jnp.dot(a, w_ref[0].astype(BF16), preferred_element_type=F32) + b_ref[0]


def _ada(c_all, w_ada, b_ada):
    rows, d = c_all.shape
    depth, _, n = w_ada.shape
    tn = _tile(n, 1024)
    return pl.pallas_call(
        _ada_kernel,
        grid=(depth, n // tn),
        in_specs=[pl.BlockSpec((rows, d), lambda l, j: (0, 0)),
                  pl.BlockSpec((1, d, tn), lambda l, j: (l, 0, j)),
                  pl.BlockSpec((1, 1, tn), lambda l, j: (l, 0, j))],
        out_specs=pl.BlockSpec((1, rows, tn), lambda l, j: (l, 0, j)),
        out_shape=jax.ShapeDtypeStruct((depth, rows, n), F32),
        compiler_params=_cparams(("arbitrary", "arbitrary")),
        name="ada",
    )(c_all, w_ada, b_ada.reshape(depth, 1, n))


SUBLANES = 8


def _wprep_kernel(tab_ref, wt_ref, wm_ref):
    del tab_ref
    wm_ref[0] = wt_ref[0].T.astype(wm_ref.dtype)


def _wprep(w_in, segs_main, n_main):
    depth, d, _ = w_in.shape
    tn = 512
    src_rows = []
    for src, size, dst in segs_main:
        assert size % tn == 0 and dst % tn == 0 and src % SUBLANES == 0
        src_rows += [(src + i) // SUBLANES for i in range(0, size, tn)]
    table = jnp.asarray(src_rows, jnp.int32)
    elem = lambda n: pl.Element(n)
    return pl.pallas_call(
        _wprep_kernel,
        grid_spec=pltpu.PrefetchScalarGridSpec(
            num_scalar_prefetch=1,
            grid=(depth, n_main // tn),
            in_specs=[pl.BlockSpec((elem(1), elem(tn), elem(d)), lambda l, j, tab: (l, tab[j] * SUBLANES, 0))],
            out_specs=pl.BlockSpec((1, d, tn), lambda l, j, tab: (l, 0, j))),
        out_shape=jax.ShapeDtypeStruct((depth, d, n_main), BF16),
        compiler_params=_cparams(("arbitrary", "arbitrary")),
        name="weight_prep",
    )(table, jnp.swapaxes(w_in, 1, 2))


def _modnorm_kernel(x_ref, sc_ref, sh_ref, nw_ref, h_ref):
    x = x_ref[...]
    y = x * lax.rsqrt(jnp.mean(x * x, axis=-1, keepdims=True) + RMS_EPS) * nw_ref[...]
    h_ref[...] = (y * (1.0 + sc_ref[0]) + sh_ref[0]).astype(h_ref.dtype)


def _modnorm(x2, scale3, shift3, nw, tm, rows_per_mod):
    m, d = x2.shape
    r = scale3.shape[1]
    mod_spec = pl.BlockSpec((1, r, d), lambda i: (i // rows_per_mod, 0, 0))
    return pl.pallas_call(
        _modnorm_kernel,
        grid=(m // tm,),
        in_specs=[pl.BlockSpec((tm, d), lambda i: (i, 0)), mod_spec, mod_spec,
                  pl.BlockSpec((1, d), lambda i: (0, 0))],
        out_specs=pl.BlockSpec((tm, d), lambda i: (i, 0)),
        out_shape=jax.ShapeDtypeStruct((m, d), BF16),
        compiler_params=_cparams(("arbitrary",)),
        name="modnorm",
    )(x2, scale3, shift3, nw)


def _inproj_kernel(h_ref, wm_ref, bm_ref, ws_ref, bs_ref, o_ref, os_ref):
    @pl.when(pl.program_id(1) == 0)
    def _():
        os_ref[...] = jnp.dot(h_ref[...], ws_ref[0], preferred_element_type=F32) + bs_ref[...]

    o_ref[...] = jnp.dot(h_ref[...], wm_ref[0], preferred_element_type=F32) + bm_ref[...]


def _inproj(h2, w_main, b_main, w_small, b_small, layer):
    m, d = h2.shape
    nm = w_main.shape[2]
    tm, tn = _tile(m, 1024), _tile(nm, 1024)
    return pl.pallas_call(
        _inproj_kernel,
        grid=(m // tm, nm // tn),
        in_specs=[pl.BlockSpec((tm, d), lambda i, j: (i, 0)),
                  pl.BlockSpec((1, d, tn), lambda i, j: (layer, 0, j)),
                  pl.BlockSpec((1, tn), lambda i, j: (0, j)),
                  pl.BlockSpec((1, d, LANES), lambda i, j: (layer, 0, 0)),
                  pl.BlockSpec((1, LANES), lambda i, j: (0, 0))],
        out_specs=[pl.BlockSpec((tm, tn), lambda i, j: (i, j)),
                   pl.BlockSpec((tm, LANES), lambda i, j: (i, 0))],
        out_shape=[jax.ShapeDtypeStruct((m, nm), F32), jax.ShapeDtypeStruct((m, LANES), F32)],
        compiler_params=_cparams(("arbitrary", "arbitrary")),
        name="inproj",
    )(h2, w_main, b_main, w_small, b_small)


def _rope_kernel(q_ref, kc_ref, ks_ref, kw_ref, cos_ref, sin_ref, qr_ref, ksr_ref, kwr_ref, co_ref, so_ref, wo_ref,
                 *, n_q, n_kv):
    cos, sin = cos_ref[...], sin_ref[...]
    tr, n_ch = q_ref.shape[0], 2 * n_kv

    def rot(x):
        return x * cos + pltpu.roll(x, HEAD_DIM // 2, axis=1) * sin

    for h in range(n_q):
        sl = slice(h * HEAD_DIM, (h + 1) * HEAD_DIM)
        qr_ref[:, sl] = (rot(q_ref[:, sl]) * HEAD_DIM ** -0.5).astype(qr_ref.dtype)
    for src, dense, state in ((kc_ref, None, co_ref), (ks_ref, ksr_ref, so_ref), (kw_ref, kwr_ref, wo_ref)):
        for ch in range(n_ch):
            sl = slice(ch * HEAD_DIM, (ch + 1) * HEAD_DIM)
            x = src[:, sl]
            if dense is not None and ch < n_kv:
                x = rot(x)
            if dense is not None:
                dense[:, sl] = x.astype(dense.dtype)
            state[pl.ds(ch, tr, stride=n_ch), :] = x


def _rope(proj, cos, sin, mo, nsa_w, kv_w):
    m = proj.shape[0]
    tr = _tile(m, 256)
    n_q, n_kv = nsa_w // HEAD_DIM, kv_w // (2 * HEAD_DIM)
    n_ch = 2 * n_kv
    kv_in = lambda name: pl.BlockSpec((tr, kv_w), lambda i: (i, mo[name] // kv_w))
    state_spec = pl.BlockSpec((tr * n_ch, HEAD_DIM), lambda i: (i, 0))
    state_sds = jax.ShapeDtypeStruct((m * n_ch, HEAD_DIM), F32)
    return pl.pallas_call(
        functools.partial(_rope_kernel, n_q=n_q, n_kv=n_kv),
        grid=(m // tr,),
        in_specs=[pl.BlockSpec((tr, nsa_w), lambda i: (i, mo['nsa_q'] // nsa_w)),
                  kv_in('kv_cmp'), kv_in('kv_sel'), kv_in('kv_win'),
                  pl.BlockSpec((tr, HEAD_DIM), lambda i: (i, 0)),
                  pl.BlockSpec((tr, HEAD_DIM), lambda i: (i, 0))],
        out_specs=[pl.BlockSpec((tr, nsa_w), lambda i: (i, 0)),
                   pl.BlockSpec((tr, kv_w), lambda i: (i, 0)),
                   pl.BlockSpec((tr, kv_w), lambda i: (i, 0)),
                   state_spec, state_spec, state_spec],
        out_shape=[jax.ShapeDtypeStruct((m, nsa_w), BF16),
                   jax.ShapeDtypeStruct((m, kv_w), BF16),
                   jax.ShapeDtypeStruct((m, kv_w), BF16),
                   state_sds, state_sds, state_sds],
        compiler_params=_cparams(("arbitrary",)),
        name="rope",
    )(proj, proj, proj, proj, cos, sin)


def _rope_tables(pos):
    half = HEAD_DIM // 2
    inv = ROPE_THETA ** (-jnp.arange(half, dtype=F32) / half)
    ang = pos.astype(F32)[:, None] * inv[None, :]
    cos, sin = jnp.cos(ang), jnp.sin(ang)
    return jnp.concatenate([cos, cos], axis=1), jnp.concatenate([-sin, sin], axis=1)


def _flash_step(carry, s, bias, v_bf):
    m, l, acc = carry
    if bias is not None:
        s = s + bias[None]
    m_new = jnp.maximum(m, jnp.max(s, axis=-1, keepdims=True))
    p = jnp.exp(s - m_new)
    alpha = jnp.exp(m - m_new)
    l = alpha * l + jnp.sum(p, axis=-1, keepdims=True)
    pv = jnp.dot(p.reshape(-1, p.shape[-1]).astype(BF16), v_bf, preferred_element_type=F32)
    acc = alpha * acc + pv.reshape(acc.shape)
    return m_new, l, acc


def _flash_init(shape_rows):
    return (jnp.full(shape_rows + (1,), NEG_BIG, F32), jnp.zeros(shape_rows + (1,), F32),
            jnp.zeros(shape_rows + (HEAD_DIM,), F32))


def _topk_rank(score, blk, n_blocks, axis=1):
    rank = jnp.zeros(score.shape, F32)
    for i in range(n_blocks):
        si = score[:, i:i + 1] if axis == 1 else score[i:i + 1, :]
        ge = jnp.where(si >= score, 1.0, 0.0)
        gt = jnp.where(si > score, 1.0, 0.0)
        rank = rank + jnp.where(blk > i, ge, gt)
    return rank


def _nsa_prompt_kernel(qc_ref, qr_ref, kc_ref, vc_ref, ks_ref, vs_ref, kw_ref, vw_ref, nz_ref, sm_ref,
                       pe_ref, wphi_ref, o_ref, kcs, vcs, *, seq, tq, tk, n_heads):
    kvh, qi = pl.program_id(1), pl.program_id(2)
    n_sel = seq // SEL_BLOCK
    n_cmp = 2 * n_sel
    grp = NSA_GROUP
    scale = HEAD_DIM ** -0.5

    @pl.when(qi == 0)
    def _():
        for c, (src, dst) in enumerate(((kc_ref, kcs), (vc_ref, vcs))):
            x = src[0].reshape(n_sel, 2 * CMP_BLOCK, HEAD_DIM)
            pe = pe_ref[c][None]
            ev = jnp.mean(x[:, :CMP_BLOCK, :] + pe, axis=1)
            od = jnp.mean(x[:, CMP_BLOCK:, :] + pe, axis=1)
            summ = jnp.concatenate([ev, od], axis=0).astype(BF16)
            dst[...] = jnp.dot(summ, wphi_ref[c].astype(BF16), preferred_element_type=F32)

    t0 = qi * tq
    pos = t0 + lax.broadcasted_iota(jnp.int32, (tq, 1), 0)
    pos_l = t0 + lax.broadcasted_iota(jnp.int32, (1, tq), 1)

    row_c = lax.broadcasted_iota(jnp.int32, (n_cmp, 1), 0)
    cblk = jnp.where(row_c < n_sel, 2 * row_c, 2 * (row_c - n_sel) + 1)
    cmp_ok = (cblk + 1) * CMP_BLOCK - 1 <= pos_l
    kcb, vcb = kcs[...].astype(BF16), vcs[...].astype(BF16)
    qc = qc_ref[0]
    imp = jnp.zeros((n_cmp, tq), F32)
    o_cmp = []
    for g in range(grp):
        s = _nt(kcb, qc[:, g * HEAD_DIM:(g + 1) * HEAD_DIM].astype(BF16)) * scale
        s = jnp.where(cmp_ok, s, NEG_BIG)
        s = s - jnp.max(s, axis=0, keepdims=True)
        p = jnp.where(cmp_ok, jnp.exp(s), 0.0)
        p = p / jnp.maximum(jnp.sum(p, axis=0, keepdims=True), TINY)
        imp = imp + p
        o_cmp.append(lax.dot_general(p.astype(BF16), vcb, (((0,), (0,)), ((), ())), preferred_element_type=F32))

    imp_sel = imp[:n_sel] + imp[n_sel:]
    blk = lax.broadcasted_iota(jnp.int32, (n_sel, 1), 0)
    q_blk = pos_l // SEL_BLOCK
    forced = (blk == 0) | (blk == q_blk) | (blk == q_blk - 1)
    score = jnp.where(forced, imp_sel + FORCE_BONUS, imp_sel)
    score = jnp.where(blk * SEL_BLOCK > pos_l, -jnp.inf, score)
    rank = _topk_rank(score, blk, n_sel, axis=0)
    sel_neg = jnp.concatenate([jnp.where(rank < N_SELECT, 0.0, NEG_BIG), jnp.zeros((LANES - n_sel, tq), F32)], axis=0)
    sel_neg = sel_neg.T.astype(BF16)

    qr = qr_ref[0]
    q4 = jnp.concatenate([qr[:, g * HEAD_DIM:(g + 1) * HEAD_DIM] for g in range(grp)], axis=0)
    q4_sel = jnp.concatenate([q4, jnp.concatenate([sel_neg] * grp, axis=0)], axis=1)

    def sel_keys(k0):
        kch = ks_ref[0, pl.ds(k0, tk), :].astype(BF16)
        kblk = (k0 + lax.broadcasted_iota(jnp.int32, (tk, 1), 0)) // SEL_BLOCK
        one_hot = jnp.where(kblk == lax.broadcasted_iota(jnp.int32, (1, LANES), 1), 1.0, 0.0).astype(BF16)
        return jnp.concatenate([kch, one_hot], axis=1), vs_ref[0, pl.ds(k0, tk), :].astype(BF16)

    def sel_step(k0, carry, bias):
        k_aug, vch = sel_keys(k0)
        return _flash_step(carry, _nt(q4_sel, k_aug).reshape(grp, tq, tk), bias, vch)

    n_chunks = (t0 + tq + tk - 1) // tk
    carry = lax.fori_loop(0, n_chunks - 1, lambda c, cr: sel_step(pl.multiple_of(c * tk, tk), cr, None),
                          _flash_init((grp, tq)))
    k_last = pl.multiple_of((n_chunks - 1) * tk, tk)
    kpos = k_last + lax.broadcasted_iota(jnp.int32, (1, tk), 1)
    _, l_s, acc_s = sel_step(k_last, carry, jnp.where(kpos <= pos, 0.0, NEG_BIG))

    def win_body(i, carry):
        k0 = pl.multiple_of((qi - i) * tq, tq)
        kpos = k0 + lax.broadcasted_iota(jnp.int32, (1, tq), 1)
        bias = jnp.where(jnp.where(kpos <= pos, pos - kpos, WINDOW) < WINDOW, 0.0, NEG_BIG)
        s = _nt(q4, kw_ref[0, pl.ds(k0, tq), :].astype(BF16)).reshape(grp, tq, tq)
        return _flash_step(carry, s, bias, vw_ref[0, pl.ds(k0, tq), :].astype(BF16))

    n_back = jnp.minimum(qi, (WINDOW + tq - 1) // tq)
    _, l_w, acc_w = lax.fori_loop(0, n_back + 1, win_body, _flash_init((grp, tq)))

    gates = jax.nn.sigmoid(sm_ref[0])
    nz = nz_ref[0]
    for g in range(grp):
        head = kvh * grp + g
        w_cmp = _lane_pick(gates, head)
        w_sel = _lane_pick(gates, n_heads + head) / jnp.maximum(l_s[g], TINY)
        w_win = _lane_pick(gates, 2 * n_heads + head) / jnp.maximum(l_w[g], TINY)
        o = w_cmp * o_cmp[g] + (w_sel * acc_s[g] + w_win * acc_w[g])
        sl = slice(g * HEAD_DIM, (g + 1) * HEAD_DIM)
        o_ref[0, :, sl] = (o * _silu(nz[:, sl])).astype(o_ref.dtype)


def _nsa_prompt(proj3, small3, qr3, ksr3, kwr3, pe, wphi, mo, nsa_w):
    b, seq, _ = proj3.shape
    n_heads = nsa_w // HEAD_DIM
    kvh = n_heads // NSA_GROUP
    gw = NSA_GROUP * HEAD_DIM
    tq = _tile(seq, 512)
    tk = _tile(seq, 512)
    assert seq % (2 * CMP_BLOCK) == 0 and tk % SEL_BLOCK == 0
    assert N_SELECT >= 3, "the forced blocks (first, current, previous) must always fit in the selection"
    assert seq // SEL_BLOCK <= LANES
    hb = lambda name: mo[name] // HEAD_DIM

    def head_spec(off_blocks):
        return pl.BlockSpec((1, seq, HEAD_DIM), lambda bi, k, qi: (bi, 0, off_blocks + k))

    return pl.pallas_call(
        functools.partial(_nsa_prompt_kernel, seq=seq, tq=tq, tk=tk, n_heads=n_heads),
        grid=(b, kvh, seq // tq),
        in_specs=[pl.BlockSpec((1, tq, gw), lambda bi, k, qi: (bi, qi, mo['nsa_q'] // gw + k)),
                  pl.BlockSpec((1, tq, gw), lambda bi, k, qi: (bi, qi, k)),
                  head_spec(hb('kv_cmp')), head_spec(hb('kv_cmp') + kvh),
                  head_spec(0), head_spec(kvh), head_spec(0), head_spec(kvh),
                  pl.BlockSpec((1, tq, gw), lambda bi, k, qi: (bi, qi, mo['nsa_z'] // gw + k)),
                  pl.BlockSpec((1, tq, LANES), lambda bi, k, qi: (bi, qi, 0)),
                  pl.BlockSpec((2, CMP_BLOCK, HEAD_DIM), lambda bi, k, qi: (0, 0, 0)),
                  pl.BlockSpec((2, HEAD_DIM, HEAD_DIM), lambda bi, k, qi: (0, 0, 0))],
        out_specs=pl.BlockSpec((1, tq, gw), lambda bi, k, qi: (bi, qi, k)),
        out_shape=jax.ShapeDtypeStruct((b, seq, nsa_w), BF16),
        scratch_shapes=[pltpu.VMEM((seq // CMP_BLOCK, HEAD_DIM), F32),
                        pltpu.VMEM((seq // CMP_BLOCK, HEAD_DIM), F32)],
        compiler_params=_cparams(("arbitrary", "arbitrary", "arbitrary")),
        name="nsa_prompt",
    )(proj3, qr3, proj3, proj3, ksr3, ksr3, kwr3, kwr3, proj3, small3, pe, wphi)


def _summ_kernel(pt_ref, *refs, n_pg, n_ch):
    del pt_ref
    x_refs, pe_ref, o_ref = refs[:n_pg], refs[n_pg], refs[n_pg + 1]
    pe = pe_ref[...][None]
    outs = []
    for r in x_refs:
        page = r[0, 0]
        n_blk = page.shape[0] // (CMP_BLOCK * n_ch)
        x = page.reshape(n_blk, CMP_BLOCK, n_ch, HEAD_DIM)
        outs.append(jnp.mean(x + pe, axis=1).reshape(n_blk * n_ch, HEAD_DIM))
    o_ref[0] = jnp.concatenate(outs, axis=0)


def _summaries(cache4, layer, pt_flat, pe_ch, n_batch, n_pages, n_ch):
    page_rows = cache4.shape[2]
    blk_per_page = page_rows // (CMP_BLOCK * n_ch)
    n_pg = _tile(n_pages, 16)
    rows = n_pg * blk_per_page * n_ch

    def page_spec(k):
        return pl.BlockSpec((1, 1, page_rows, HEAD_DIM),
                            lambda bi, j, pt: (layer, pt[bi * n_pages + j * n_pg + k], 0, 0))

    return pl.pallas_call(
        functools.partial(_summ_kernel, n_pg=n_pg, n_ch=n_ch),
        grid_spec=pltpu.PrefetchScalarGridSpec(
            num_scalar_prefetch=1,
            grid=(n_batch, n_pages // n_pg),
            in_specs=[page_spec(k) for k in range(n_pg)]
            + [pl.BlockSpec((CMP_BLOCK, n_ch, HEAD_DIM), lambda bi, j, pt: (0, 0, 0))],
            out_specs=pl.BlockSpec((1, rows, HEAD_DIM), lambda bi, j, pt: (bi, j, 0))),
        out_shape=jax.ShapeDtypeStruct((n_batch, n_pages * blk_per_page * n_ch, HEAD_DIM), F32),
        compiler_params=_cparams(("arbitrary", "arbitrary")),
        name="cmp_summaries",
    )(pt_flat, *([cache4] * n_pg), pe_ch)


def _cmpsel_kernel(summ_ref, q_ref, wphi_ref, ocmp_ref, sel_ref, *, past, t_new, kvh_n, n_selp):
    grp = NSA_GROUP
    scale = HEAD_DIM ** -0.5
    n_ch = 2 * kvh_n
    n_half = summ_ref.shape[1] // (2 * n_ch)
    n_sel = n_half + 1

    def summaries(ch, odd):
        return summ_ref[0, pl.ds(odd * n_ch + ch, n_half, stride=2 * n_ch), :]
    tpos = lax.broadcasted_iota(jnp.int32, (t_new, 1), 0) + past
    pos = jnp.concatenate([tpos] * grp, axis=0)
    lane = lax.broadcasted_iota(jnp.int32, (1, n_half), 1)
    ok_e = (2 * lane + 1) * CMP_BLOCK - 1 <= pos
    ok_o = (2 * lane + 2) * CMP_BLOCK - 1 <= pos
    w_k, w_v = wphi_ref[0].astype(BF16), wphi_ref[1].astype(BF16)
    q = q_ref[0]
    scores = []
    for k in range(kvh_n):
        proj = lambda ch, odd, w: jnp.dot(summaries(ch, odd).astype(BF16), w,
                                          preferred_element_type=F32).astype(BF16)
        k_e, k_o = proj(k, 0, w_k), proj(k, 1, w_k)
        v_e, v_o = proj(kvh_n + k, 0, w_v), proj(kvh_n + k, 1, w_v)
        qk = jnp.concatenate(
            [q[:, (k * grp + g) * HEAD_DIM:(k * grp + g + 1) * HEAD_DIM] for g in range(grp)], axis=0).astype(BF16)
        s_e = jnp.where(ok_e, _nt(qk, k_e) * scale, NEG_BIG)
        s_o = jnp.where(ok_o, _nt(qk, k_o) * scale, NEG_BIG)
        m = jnp.maximum(jnp.max(s_e, axis=-1, keepdims=True), jnp.max(s_o, axis=-1, keepdims=True))
        p_e = jnp.where(ok_e, jnp.exp(s_e - m), 0.0)
        p_o = jnp.where(ok_o, jnp.exp(s_o - m), 0.0)
        den = jnp.maximum(jnp.sum(p_e, axis=-1, keepdims=True) + jnp.sum(p_o, axis=-1, keepdims=True), TINY)
        p_e, p_o = p_e / den, p_o / den
        o = (jnp.dot(p_e.astype(BF16), v_e, preferred_element_type=F32)
             + jnp.dot(p_o.astype(BF16), v_o, preferred_element_type=F32))
        imp = jnp.zeros((t_new, n_half), F32)
        for g in range(grp):
            hsl = slice((k * grp + g) * HEAD_DIM, (k * grp + g + 1) * HEAD_DIM)
            ocmp_ref[0, :, hsl] = o[g * t_new:(g + 1) * t_new]
            imp = imp + (p_e[g * t_new:(g + 1) * t_new] + p_o[g * t_new:(g + 1) * t_new])
        tail_lane = lax.broadcasted_iota(jnp.int32, (t_new, n_selp - n_half), 1)
        tail = jnp.where(tail_lane == 0, 0.0, -jnp.inf)
        scores.append(jnp.concatenate([imp, tail], axis=1))
    score = jnp.concatenate(scores, axis=0)
    pos_r = jnp.concatenate([tpos] * kvh_n, axis=0)
    blk = lax.broadcasted_iota(jnp.int32, (1, n_selp), 1)
    q_blk = pos_r // SEL_BLOCK
    forced = (blk == 0) | (blk == q_blk) | (blk == q_blk - 1)
    score = jnp.where(forced, score + FORCE_BONUS, score)
    score = jnp.where((blk * SEL_BLOCK > pos_r) | (blk >= n_sel), -jnp.inf, score)
    rank = _topk_rank(score, blk, n_sel)

    n_slot = sel_ref.shape[3]
    n_top = min(N_SELECT, n_sel)
    blk_f = blk.astype(F32)
    blk_hi = jnp.floor(blk_f * (1.0 / 16.0))
    blk_lo = blk_f - 16.0 * blk_hi
    before = jnp.where(lax.broadcasted_iota(jnp.int32, (n_selp, 1), 0) < blk, 1.0, 0.0).astype(BF16)
    slot = lax.broadcasted_iota(jnp.int32, (n_slot, 1), 0).astype(F32)
    for k in range(kvh_n):
        member = jnp.where(rank[k * t_new:(k + 1) * t_new] < n_top, 1.0, 0.0)
        member_past = jnp.where(blk < n_half, member, 0.0)
        used = jnp.max(member_past, axis=0, keepdims=True)
        n_before = jnp.dot(jnp.broadcast_to(used, (SUBLANES, n_selp)).astype(BF16), before,
                           preferred_element_type=F32)[0:1]
        place = jnp.where(used > 0.5, jnp.where(n_before == slot, 1.0, 0.0), 0.0).astype(BF16)
        rows = jnp.concatenate([blk_hi, blk_lo, used, jnp.zeros((SUBLANES - 3, n_selp), F32), member_past], axis=0)
        packed = _nt(rows.astype(BF16), place)
        has_new = jnp.broadcast_to(member[:, n_half:n_half + 1], (t_new, n_slot))
        sel_ref[0, k] = jnp.concatenate([packed, has_new], axis=0)


def _cmpsel(summ, proj3, wphi, mo, nsa_w, past):
    b, summ_rows, _ = summ.shape
    t_new = proj3.shape[1]
    kvh_n = nsa_w // HEAD_DIM // NSA_GROUP
    n_half = summ_rows // (4 * kvh_n)
    assert n_half == past // SEL_BLOCK
    n_selp = -(-(n_half + 1) // LANES) * LANES
    n_slot = t_new * min(N_SELECT, n_half + 1)
    sel_rows = SUBLANES + 2 * t_new
    assert t_new == SUBLANES
    return pl.pallas_call(
        functools.partial(_cmpsel_kernel, past=past, t_new=t_new, kvh_n=kvh_n, n_selp=n_selp),
        grid=(b,),
        in_specs=[pl.BlockSpec((1, summ_rows, HEAD_DIM), lambda bi: (bi, 0, 0)),
                  pl.BlockSpec((1, t_new, nsa_w), lambda bi: (bi, 0, mo['nsa_q'] // nsa_w)),
                  pl.BlockSpec((2, HEAD_DIM, HEAD_DIM), lambda bi: (0, 0, 0))],
        out_specs=[pl.BlockSpec((1, t_new, nsa_w), lambda bi: (bi, 0, 0)),
                   pl.BlockSpec((1, kvh_n, sel_rows, n_slot), lambda bi: (bi, 0, 0, 0))],
        out_shape=[jax.ShapeDtypeStruct((b, t_new, nsa_w), F32),
                   jax.ShapeDtypeStruct((b, kvh_n, sel_rows, n_slot), F32)],
        compiler_params=_cparams(("arbitrary",)),
        name="cmp_select",
    )(summ, proj3, wphi)


def _two_part_attention(parts):
    m = None
    for s, msk, _ in parts:
        mx = jnp.max(jnp.where(msk, s, NEG_BIG), axis=-1, keepdims=True)
        m = mx if m is None else jnp.maximum(m, mx)
    den, acc = 0.0, 0.0
    for s, msk, v in parts:
        p = jnp.where(msk, jnp.exp(jnp.where(msk, s, NEG_BIG) - m), 0.0)
        den = den + jnp.sum(p, axis=-1, keepdims=True)
        acc = acc + jnp.dot(p.astype(BF16), v, preferred_element_type=F32)
    return acc / jnp.maximum(den, TINY)


SLOT_CHUNK = 16


def _selwin_kernel(idx_ref, cnt_ref, pt_ref, cache_ref, qr_ref, ksn_ref, vsn_ref, wbuf_ref, kwn_ref, vwn_ref,
                   sm_ref, ocmp_ref, nz_ref, sel_ref, o_ref, kbuf, vbuf, sem,
                   *, layer, past, t_new, kvh_n, n_pages, page, n_heads):
    bi, kvh = pl.program_id(0), pl.program_id(1)
    grp = NSA_GROUP
    n_ch = 2 * kvh_n
    n_slot = kbuf.shape[1]
    n_past_blk = past // SEL_BLOCK
    per_page = page // SEL_BLOCK
    step = bi * kvh_n + kvh
    n_steps = pl.num_programs(0) * kvh_n
    buf = step % 2

    def n_chunks_of(s):
        return (cnt_ref[s] + SLOT_CHUNK - 1) // SLOT_CHUNK

    def copies(half, i, row0, head):
        return (pltpu.make_async_copy(cache_ref.at[layer, pl.ds(row0, SEL_BLOCK), head, :], kbuf.at[half, i],
                                      sem.at[half, 0]),
                pltpu.make_async_copy(cache_ref.at[layer, pl.ds(row0, SEL_BLOCK), kvh_n + head, :], vbuf.at[half, i],
                                      sem.at[half, 1]))

    def gather(s, half):
        b_s, head = s // kvh_n, s % kvh_n

        def issue(i, carry):
            idc = jnp.minimum(idx_ref[s * n_slot + i], n_past_blk - 1)
            phys = pt_ref[b_s * n_pages + idc // per_page]
            row0 = pl.multiple_of(phys * page + (idc % per_page) * SEL_BLOCK, SEL_BLOCK)
            for cp in copies(half, i, row0, head):
                cp.start()
            return carry

        lax.fori_loop(0, n_chunks_of(s) * SLOT_CHUNK, issue, 0)

    @pl.when(step == 0)
    def _():
        gather(0, 0)

    @pl.when(step + 1 < n_steps)
    def _():
        gather(step + 1, 1 - buf)

    n_chunk = n_chunks_of(step)

    def wait(i, carry):
        for cp in copies(buf, i, 0, kvh):
            cp.wait()
        return carry

    lax.fori_loop(0, n_chunk * SLOT_CHUNK, wait, 0)

    qr = qr_ref[0]
    q4 = jnp.concatenate([qr[:, g * HEAD_DIM:(g + 1) * HEAD_DIM] for g in range(grp)], axis=0)
    rows = grp * t_new
    row_t = jnp.concatenate([lax.broadcasted_iota(jnp.int32, (t_new, 1), 0)] * grp, axis=0)
    pos = row_t + past

    def padded(ref):
        x = ref[0].astype(F32)
        return jnp.concatenate([x, jnp.zeros((SEL_BLOCK - t_new, HEAD_DIM), F32)], axis=0).astype(BF16)

    new_j = lax.broadcasted_iota(jnp.int32, (1, SEL_BLOCK), 1)
    new_pos = past + new_j
    new_real = new_j < t_new

    flags = sel_ref[0, 0]
    uses_slot = jnp.concatenate([flags[SUBLANES:SUBLANES + t_new]] * grp, axis=0).astype(BF16)
    has_new = jnp.concatenate([flags[SUBLANES + t_new:SUBLANES + 2 * t_new, 0:1]] * grp, axis=0)
    width = SLOT_CHUNK * SEL_BLOCK

    def sel_body(c, carry):
        s0 = pl.multiple_of(c * SLOT_CHUNK, SLOT_CHUNK)
        k_c = kbuf[buf, pl.ds(s0, SLOT_CHUNK)].reshape(width, HEAD_DIM).astype(BF16)
        v_c = vbuf[buf, pl.ds(s0, SLOT_CHUNK)].reshape(width, HEAD_DIM).astype(BF16)
        slot_of_key = s0 + lax.broadcasted_iota(jnp.int32, (1, width), 1) // SEL_BLOCK
        expand = jnp.where(slot_of_key == lax.broadcasted_iota(jnp.int32, (n_slot, 1), 0), 1.0, 0.0).astype(BF16)
        bias = jnp.where(jnp.dot(uses_slot, expand, preferred_element_type=F32) > 0.5, 0.0, NEG_BIG)
        return _flash_step(carry, _nt(q4, k_c)[None], bias, v_c)

    carry = lax.fori_loop(0, n_chunk, sel_body, _flash_init((1, rows)))
    bias_new = jnp.where(jnp.where(new_real & (new_pos <= pos), has_new, 0.0) > 0.5, 0.0, NEG_BIG)
    _, l_sel, acc_sel = _flash_step(carry, _nt(q4, padded(ksn_ref))[None], bias_new, padded(vsn_ref))
    o_sel = acc_sel[0] / jnp.maximum(l_sel[0], TINY)

    n_buf = wbuf_ref.shape[1] // n_ch
    kwb = wbuf_ref[0, pl.ds(kvh, n_buf, stride=n_ch), :].astype(BF16)
    vwb = wbuf_ref[0, pl.ds(kvh_n + kvh, n_buf, stride=n_ch), :].astype(BF16)
    buf_pos = past - n_buf + lax.broadcasted_iota(jnp.int32, (1, n_buf), 1)
    msk_buf = jnp.where(buf_pos <= pos, pos - buf_pos, WINDOW) < WINDOW
    msk_wn = jnp.where(new_real & (new_pos <= pos), pos - new_pos, WINDOW) < WINDOW
    o_win = _two_part_attention([(_nt(q4, kwb), msk_buf, vwb),
                                 (_nt(q4, padded(kwn_ref)), msk_wn, padded(vwn_ref))])

    gates = jax.nn.sigmoid(sm_ref[0])
    ocmp, nz = ocmp_ref[0], nz_ref[0]
    for g in range(grp):
        head = kvh * grp + g
        rs = slice(g * t_new, (g + 1) * t_new)
        sl = slice(g * HEAD_DIM, (g + 1) * HEAD_DIM)
        o = (_lane_pick(gates, head) * ocmp[:, sl]
             + (_lane_pick(gates, n_heads + head) * o_sel[rs] + _lane_pick(gates, 2 * n_heads + head) * o_win[rs]))
        o_ref[0, :, sl] = (o * _silu(nz[:, sl])).astype(o_ref.dtype)


def _selwin(sel, pt_flat, cache4, layer, qr3, ksr3, win_state3, kwr3, small3, ocmp3, proj3, mo, nsa_w,
            past, n_pages, page):
    b, t_new, _ = qr3.shape
    n_heads = nsa_w // HEAD_DIM
    kvh_n = n_heads // NSA_GROUP
    gw = NSA_GROUP * HEAD_DIM
    sel_rows, n_slot = sel.shape[2], sel.shape[3]
    assert n_slot % SLOT_CHUNK == 0
    slot_blocks = (16.0 * sel[:, :, 0] + sel[:, :, 1]).astype(jnp.int32).reshape(-1)
    slot_count = jnp.sum(sel[:, :, 2], axis=-1).astype(jnp.int32).reshape(-1)
    win_rows = win_state3.shape[1]
    new_k = pl.BlockSpec((1, t_new, HEAD_DIM), lambda bi, k, *_: (bi, 0, k))
    new_v = pl.BlockSpec((1, t_new, HEAD_DIM), lambda bi, k, *_: (bi, 0, kvh_n + k))
    return pl.pallas_call(
        functools.partial(_selwin_kernel, layer=layer, past=past, t_new=t_new, kvh_n=kvh_n, n_pages=n_pages,
                          page=page, n_heads=n_heads),
        grid_spec=pltpu.PrefetchScalarGridSpec(
            num_scalar_prefetch=3,
            grid=(b, kvh_n),
            in_specs=[pl.BlockSpec(memory_space=pl.ANY),
                      pl.BlockSpec((1, t_new, gw), lambda bi, k, *_: (bi, 0, k)),
                      new_k, new_v,
                      pl.BlockSpec((1, win_rows, HEAD_DIM), lambda bi, k, *_: (layer * b + bi, 0, 0)),
                      new_k, new_v,
                      pl.BlockSpec((1, t_new, LANES), lambda bi, k, *_: (bi, 0, 0)),
                      pl.BlockSpec((1, t_new, gw), lambda bi, k, *_: (bi, 0, k)),
                      pl.BlockSpec((1, t_new, gw), lambda bi, k, *_: (bi, 0, mo['nsa_z'] // gw + k)),
                      pl.BlockSpec((1, 1, sel_rows, n_slot), lambda bi, k, *_: (bi, k, 0, 0))],
            out_specs=pl.BlockSpec((1, t_new, gw), lambda bi, k, *_: (bi, 0, k)),
            scratch_shapes=[pltpu.VMEM((2, n_slot, SEL_BLOCK, HEAD_DIM), F32),
                            pltpu.VMEM((2, n_slot, SEL_BLOCK, HEAD_DIM), F32),
                            pltpu.SemaphoreType.DMA((2, 2))]),
        out_shape=jax.ShapeDtypeStruct((b, t_new, nsa_w), BF16),
        compiler_params=_cparams(("arbitrary", "arbitrary")),
        name="sel_win_sample",
    )(slot_blocks, slot_count, pt_flat, cache4, qr3, ksr3, ksr3, win_state3, kwr3, kwr3, small3, ocmp3, proj3, sel)


def _conv_kernel(x_ref, b_ref, c_ref, z_ref, buf_ref, w_ref, o_ref, st_ref, prev):
    tc = x_ref.shape[1]

    @pl.when(pl.program_id(1) == 0)
    def _():
        prev[0:CONV_K - 1, :] = buf_ref[0]

    u = c_ref[0] * x_ref[0]
    row = lax.broadcasted_iota(jnp.int32, (tc, 1), 0)
    w = w_ref[...]
    y = u * w[CONV_K - 1:CONV_K, :]
    for back in range(1, CONV_K):
        shifted = pltpu.roll(u, back, axis=0)
        for r in range(back):
            shifted = jnp.where(row == r, prev[CONV_K - 1 - back + r:CONV_K - back + r, :], shifted)
        y = y + shifted * w[CONV_K - 1 - back:CONV_K - back, :]
    o_ref[0] = (b_ref[0] * y * _silu(z_ref[0])).astype(o_ref.dtype)
    tail = u[tc - (CONV_K - 1):, :]
    prev[0:CONV_K - 1, :] = tail
    st_ref[0] = tail


def _conv(proj3, buf, w, mo, conv_w):
    b, seq, _ = proj3.shape
    tc = _tile(seq, 512)
    assert tc >= CONV_K - 1
    col = lambda name: pl.BlockSpec((1, tc, conv_w), lambda bi, ti: (bi, ti, mo[name] // conv_w))
    return pl.pallas_call(
        _conv_kernel,
        grid=(b, seq // tc),
        in_specs=[col('conv_x'), col('conv_b'), col('conv_c'), col('conv_z'),
                  pl.BlockSpec((1, CONV_K - 1, conv_w), lambda bi, ti: (bi, 0, 0)),
                  pl.BlockSpec((CONV_K, conv_w), lambda bi, ti: (0, 0))],
        out_specs=[pl.BlockSpec((1, tc, conv_w), lambda bi, ti: (bi, ti, 0)),
                   pl.BlockSpec((1, CONV_K - 1, conv_w), lambda bi, ti: (bi, 0, 0))],
        out_shape=[jax.ShapeDtypeStruct((b, seq, conv_w), BF16),
                   jax.ShapeDtypeStruct((b, CONV_K - 1, conv_w), F32)],
        scratch_shapes=[pltpu.VMEM((8, conv_w), F32)],
        compiler_params=_cparams(("arbitrary", "arbitrary")),
        name="short_conv",
    )(proj3, proj3, proj3, proj3, buf, w)


def _log_sigmoid(x):
    return jnp.minimum(x, 0.0) - jnp.log(1.0 + jnp.exp(-jnp.abs(x)))


def _mlstm_kernel(q_ref, k_ref, v_ref, og_ref, z_ref, sm_ref, gr_ref, c0_ref, n0_ref, m0_ref, nw_ref,
                  o_ref, c_out, n_out, m_out, c_s, n_s, m_s, *, n_chunks, gate_i, gate_f):
    ci = pl.program_id(1)

    @pl.when(ci == 0)
    def _():
        c_s[...] = c0_ref[0]
        n_s[...] = n0_ref[0]
        m_s[...] = m0_ref[0]

    small, gr = sm_ref[0], gr_ref[0]
    dk = c_s.shape[1]
    for h in range(MLSTM_HEADS):
        hs = slice(h * dk, (h + 1) * dk)
        c_new, n_new, m_new, out = _mlstm_head(
            q_ref[0, :, hs], k_ref[0, :, hs], v_ref[0, :, hs], og_ref[0, :, hs], z_ref[0, :, hs],
            small[:, gate_i + h:gate_i + h + 1], small[:, gate_f + h:gate_f + h + 1],
            gr[h:h + 1, :], gr[MLSTM_HEADS + h:MLSTM_HEADS + h + 1, :],
            c_s[h], n_s[h], m_s[h], nw_ref[h])
        c_s[h] = c_new
        n_s[h] = n_new
        m_s[h] = m_new
        o_ref[0, :, hs] = out.astype(o_ref.dtype)

    @pl.when(ci == n_chunks - 1)
    def _():
        c_out[0] = c_s[...]
        n_out[0] = n_s[...]
        m_out[0] = m_s[...]


def _mlstm_head(q, k, v, og, z, i_col, f_col, i_row, f_row, c_prev, n_prev, m_prev, norm_w):
    lc, dk = q.shape
    lf_col, lf_row = _log_sigmoid(f_col), _log_sigmoid(f_row)
    ti = lax.broadcasted_iota(jnp.int32, (lc, 1), 0)
    si = lax.broadcasted_iota(jnp.int32, (1, lc), 1)
    causal = si <= ti
    b_col = jnp.sum(jnp.where(causal, lf_row, 0.0), axis=1, keepdims=True)
    b_row = jnp.sum(jnp.where(ti <= si, lf_col, 0.0), axis=0, keepdims=True)
    log_d = jnp.where(causal, b_col - b_row + i_row, -jnp.inf)
    m_t = jnp.maximum(b_col + m_prev, jnp.max(log_d, axis=1, keepdims=True))
    inter = jnp.exp(b_col + m_prev - m_t)
    kf = k * (dk ** -0.5)
    qb, kb, vb = q.astype(BF16), kf.astype(BF16), v.astype(BF16)
    s = _nt(qb, kb) * jnp.exp(log_d - m_t)
    num = (inter * jnp.dot(qb, c_prev.astype(BF16), preferred_element_type=F32)
           + jnp.dot(s.astype(BF16), vb, preferred_element_type=F32))
    den = inter * jnp.sum(q * n_prev, axis=1, keepdims=True) + jnp.sum(s, axis=1, keepdims=True)
    hid = num / jnp.maximum(jnp.abs(den), jnp.exp(-m_t))

    m_new = m_t[lc - 1:lc, :]
    b_last = b_col[lc - 1:lc, :]
    wdec = jnp.exp(b_last - b_col + i_col - m_new)
    cdec = jnp.exp(b_last + m_prev - m_new)
    c_new = cdec * c_prev + lax.dot_general(kb, (wdec * v).astype(BF16), (((0,), (0,)), ((), ())),
                                            preferred_element_type=F32)
    n_new = cdec * n_prev + jnp.sum(wdec * kf, axis=0, keepdims=True)
    hn = hid * lax.rsqrt(jnp.mean(hid * hid, axis=-1, keepdims=True) + RMS_EPS) * norm_w
    return c_new, n_new, m_new, jax.nn.sigmoid(og) * hn * _silu(z)


def _mlstm(proj3, small3, gates_row, c0, n0, m0, norm_w, mo, ml_w, gate_i, gate_f):
    b, seq, _ = proj3.shape
    nh = MLSTM_HEADS
    dk = ml_w // nh
    lc = _tile(seq, 256)
    n_chunks = seq // lc
    col = lambda name: pl.BlockSpec((1, lc, ml_w), lambda bi, ci: (bi, ci, mo[name] // ml_w))
    state = lambda shape: pl.BlockSpec((1, nh) + shape, lambda bi, ci: (bi, 0, 0, 0))
    return pl.pallas_call(
        functools.partial(_mlstm_kernel, n_chunks=n_chunks, gate_i=gate_i, gate_f=gate_f),
        grid=(b, n_chunks),
        in_specs=[col('m_q'), col('m_k'), col('m_v'), col('m_o'), col('m_z'),
                  pl.BlockSpec((1, lc, LANES), lambda bi, ci: (bi, ci, 0)),
                  pl.BlockSpec((1, 2 * nh, lc), lambda bi, ci: (bi, 0, ci)),
                  state((dk, dk)), state((1, dk)), state((1, 1)),
                  pl.BlockSpec((nh, 1, dk), lambda bi, ci: (0, 0, 0))],
        out_specs=[pl.BlockSpec((1, lc, ml_w), lambda bi, ci: (bi, ci, 0)),
                   state((dk, dk)), state((1, dk)), state((1, 1))],
        out_shape=[jax.ShapeDtypeStruct((b, seq, ml_w), BF16),
                   jax.ShapeDtypeStruct((b, nh, dk, dk), F32),
                   jax.ShapeDtypeStruct((b, nh, 1, dk), F32),
                   jax.ShapeDtypeStruct((b, nh, 1, 1), F32)],
        scratch_shapes=[pltpu.VMEM((nh, dk, dk), F32), pltpu.VMEM((nh, 1, dk), F32), pltpu.VMEM((nh, 1, 1), F32)],
        compiler_params=_cparams(("arbitrary", "arbitrary")),
        name="mlstm",
    )(proj3, proj3, proj3, proj3, proj3, small3, gates_row, c0, n0, m0, norm_w)


def _outproj_kernel(x_ref, g_ref, a_ref, c_ref, m_ref, w_ref, o_ref):
    ka, kc = a_ref.shape[1], c_ref.shape[1]
    acc = jnp.dot(a_ref[...], w_ref[0, 0:ka, :], preferred_element_type=F32)
    acc = acc + jnp.dot(c_ref[...], w_ref[0, ka:ka + kc, :], preferred_element_type=F32)
    acc = acc + jnp.dot(m_ref[...], w_ref[0, ka + kc:, :], preferred_element_type=F32)
    o_ref[...] = x_ref[...] + g_ref[0] * acc


def _outproj(x2, gate3, o_nsa, o_conv, o_ml, w_out, layer, tm, rows_per_mod):
    m, d = x2.shape
    tn = _tile(d, 1024)
    r = gate3.shape[1]
    kdim = w_out.shape[1]
    lhs = lambda a: pl.BlockSpec((tm, a.shape[1]), lambda i, j: (i, 0))
    return pl.pallas_call(
        _outproj_kernel,
        grid=(m // tm, d // tn),
        in_specs=[pl.BlockSpec((tm, tn), lambda i, j: (i, j)),
                  pl.BlockSpec((1, r, tn), lambda i, j: (i // rows_per_mod, 0, j)),
                  lhs(o_nsa), lhs(o_conv), lhs(o_ml),
                  pl.BlockSpec((1, kdim, tn), lambda i, j: (layer, 0, j))],
        out_specs=pl.BlockSpec((tm, tn), lambda i, j: (i, j)),
        out_shape=jax.ShapeDtypeStruct((m, d), F32),
        compiler_params=_cparams(("arbitrary", "arbitrary")),
        name="outproj",
    )(x2, gate3, o_nsa, o_conv, o_ml, w_out)


def _rmsnorm_kernel(x_ref, w_ref, o_ref):
    x = x_ref[...]
    o_ref[...] = x * lax.rsqrt(jnp.mean(x * x, axis=-1, keepdims=True) + RMS_EPS) * w_ref[...]


def _rmsnorm(x2, w):
    m, d = x2.shape
    tm = _tile(m, 512)
    return pl.pallas_call(
        _rmsnorm_kernel,
        grid=(m // tm,),
        in_specs=[pl.BlockSpec((tm, d), lambda i: (i, 0)), pl.BlockSpec((1, d), lambda i: (0, 0))],
        out_specs=pl.BlockSpec((tm, d), lambda i: (i, 0)),
        out_shape=jax.ShapeDtypeStruct((m, d), F32),
        compiler_params=_cparams(("arbitrary",)),
        name="final_norm",
    )(x2, w.reshape(1, d))


def _layer(x3, mod, lw, cfg, past, states):
    b, seq, d = x3.shape
    m = b * seq
    mo, nsa_w, kv_w, conv_w, ml_w = cfg['mo'], cfg['nsa_w'], cfg['kv_w'], cfg['conv_w'], cfg['ml_w']
    shift, scale, gate = mod
    x2 = x3.reshape(m, d)
    if seq >= 512:
        tm, rows_per_mod = 512, seq // 512
        tm_out = _tile(seq, 1024)
        per = lambda a: a.reshape(b, 1, d)
    else:
        tm, rows_per_mod = m, 1
        tm_out = m
        per = lambda a: jnp.repeat(a, seq, axis=0).reshape(1, m, d)
    h2 = _modnorm(x2, per(scale), per(shift), lw['norm_w'], tm, rows_per_mod)
    proj, small = _inproj(h2, lw['w_main'], lw['b_main'], lw['w_small'], lw['b_small'], lw['layer'])
    nm = proj.shape[1]
    proj3, small3 = proj.reshape(b, seq, nm), small.reshape(b, seq, LANES)

    pos = past + jnp.arange(seq, dtype=jnp.int32)
    cos, sin = _rope_tables(pos)
    qr, ksr, kwr, cmp_rows, sel_rows, win_rows = _rope(proj, jnp.tile(cos, (b, 1)), jnp.tile(sin, (b, 1)), mo,
                                                       nsa_w, kv_w)
    qr3, ksr3, kwr3 = qr.reshape(b, seq, nsa_w), ksr.reshape(b, seq, kv_w), kwr.reshape(b, seq, kv_w)

    if states is None:
        o_nsa = _nsa_prompt(proj3, small3, qr3, ksr3, kwr3, lw['cmp_pe'], lw['cmp_w'], mo, nsa_w)
        conv_buf = jnp.zeros((b, CONV_K - 1, conv_w), F32)
        dk = ml_w // MLSTM_HEADS
        c0 = jnp.zeros((b, MLSTM_HEADS, dk, dk), F32)
        n0 = jnp.zeros((b, MLSTM_HEADS, 1, dk), F32)
        m0 = jnp.zeros((b, MLSTM_HEADS, 1, 1), F32)
    else:
        n_pages, page = states['n_pages'], states['page']
        summ = _summaries(states['cache_cmp'], states['layer'], states['pt_flat'], lw['pe_ch'], b, n_pages,
                          kv_w // HEAD_DIM)
        o_cmp, sel = _cmpsel(summ, proj3, lw['cmp_w'], mo, nsa_w, past)
        o_nsa = _selwin(sel, states['pt_flat'], states['cache_sel'], states['layer'], qr3, ksr3,
                        states['win'], kwr3, small3, o_cmp, proj3, mo, nsa_w, past, n_pages, page)
        conv_buf = states['conv']
        dk = ml_w // MLSTM_HEADS
        c0 = states['C']
        n0 = states['n'].reshape(b, MLSTM_HEADS, 1, dk)
        m0 = states['m'].reshape(b, MLSTM_HEADS, 1, 1)

    o_conv, new_conv = _conv(proj3, conv_buf, lw['conv_w'], mo, conv_w)
    gates_row = jnp.swapaxes(small3[:, :, cfg['gate_i']:cfg['gate_i'] + 2 * MLSTM_HEADS], 1, 2)
    o_ml, c1, n1, m1 = _mlstm(proj3, small3, gates_row, c0, n0, m0, lw['mlstm_norm_w'], mo, ml_w,
                              cfg['gate_i'], cfg['gate_f'])

    y = _outproj(x2, per(gate), o_nsa.reshape(m, nsa_w), o_conv.reshape(m, conv_w), o_ml.reshape(m, ml_w),
                 lw['w_out'], lw['layer'], tm_out, max(seq // tm_out, 1))
    kv_shape = (b, seq, 2, kv_w // (2 * HEAD_DIM), HEAD_DIM)
    dk = ml_w // MLSTM_HEADS
    st = (cmp_rows.reshape(kv_shape), sel_rows.reshape(kv_shape), win_rows.reshape(kv_shape), new_conv, c1,
          n1.reshape(b, MLSTM_HEADS, dk), m1.reshape(b, MLSTM_HEADS))
    return y.reshape(b, seq, d), st


def kernel(x_prompt, x_sample, cache_cmp_kv, cache_sel_kv, state_win_kv, state_conv, state_mlstm_C, state_mlstm_n,
           state_mlstm_m, page_table, c_prompt, c_sample, norm_w, w_ada, b_ada, w_in, b_in, cmp_pe, cmp_w, conv_w,
           mlstm_norm_w, w_out, final_norm_w):
    bp, seq, d = x_prompt.shape
    bs, t_new, _ = x_sample.shape
    depth = w_in.shape[0]
    layout = _in_layout(d)
    sizes = dict(layout)
    in_off, _ = _offsets([n for n, _ in layout], sizes)
    mo, n_main = _offsets(MAIN_NAMES, sizes)
    so, n_small = _offsets(SMALL_NAMES, sizes)
    assert n_small <= LANES
    nsa_w, kv_w, conv_wd, ml_w = sizes['nsa_q'], sizes['kv_cmp'], sizes['conv_x'], sizes['m_q']
    cfg = dict(mo=mo, nsa_w=nsa_w, kv_w=kv_w, conv_w=conv_wd, ml_w=ml_w, gate_i=so['m_i'], gate_f=so['m_f'])
    n_kv = kv_w // (2 * HEAD_DIM)

    def gather_cols(a, names, pad_to):
        parts = [a[..., in_off[n]:in_off[n] + sizes[n]] for n in names]
        width = sum(sizes[n] for n in names)
        if pad_to > width:
            parts.append(jnp.zeros(a.shape[:-1] + (pad_to - width,), a.dtype))
        return jnp.concatenate(parts, axis=-1)

    segs_main = tuple((in_off[n], sizes[n], mo[n]) for n in MAIN_NAMES)
    w_main_all = _wprep(w_in, segs_main, n_main)
    w_small_all = gather_cols(w_in, SMALL_NAMES, LANES).astype(BF16)
    w_out_bf = w_out.astype(BF16)

    layers = []
    for l in range(depth):
        pe_ch = jnp.repeat(jnp.transpose(cmp_pe[l], (1, 0, 2)), n_kv, axis=1)
        layers.append(dict(
            layer=l, norm_w=norm_w[l].reshape(1, d),
            w_main=w_main_all,
            b_main=gather_cols(b_in[l], MAIN_NAMES, n_main).reshape(1, n_main),
            w_small=w_small_all,
            b_small=gather_cols(b_in[l], SMALL_NAMES, LANES).reshape(1, LANES),
            cmp_pe=cmp_pe[l], cmp_w=cmp_w[l], pe_ch=pe_ch, conv_w=conv_w[l],
            mlstm_norm_w=mlstm_norm_w[l].reshape(MLSTM_HEADS, 1, ml_w // MLSTM_HEADS),
            w_out=w_out_bf))

    n_rows = bp + bs
    pad_rows = -(-n_rows // 8) * 8
    c_all = jnp.concatenate([c_prompt, c_sample, jnp.zeros((pad_rows - n_rows, d), F32)], axis=0)
    mod = _ada(c_all, w_ada, b_ada)

    def mods(l, lo, hi):
        return tuple(mod[l, lo:hi, i * d:(i + 1) * d] for i in range(3))

    xp = x_prompt
    ps = []
    for l in range(depth):
        xp, st = _layer(xp, mods(l, 0, bp), layers[l], cfg, 0, None)
        ps.append(st)

    n_phys, page = cache_cmp_kv.shape[1], cache_cmp_kv.shape[2]
    n_pages = page_table.shape[1]
    past = n_pages * page
    assert past % SEL_BLOCK == 0 and t_new <= CMP_BLOCK and state_win_kv.shape[2] <= WINDOW
    pt_flat = page_table.reshape(-1).astype(jnp.int32)
    n_ch = 2 * n_kv
    cache_cmp4 = cache_cmp_kv.reshape(depth, n_phys, page * n_ch, HEAD_DIM)
    cache_sel3 = cache_sel_kv.reshape(depth, n_phys * page, n_ch, HEAD_DIM)
    win4 = state_win_kv.reshape(depth * bs, state_win_kv.shape[2] * n_ch, HEAD_DIM)
    xs = x_sample
    ss = []
    for l in range(depth):
        states = dict(layer=l, n_pages=n_pages, page=page, pt_flat=pt_flat, cache_cmp=cache_cmp4,
                      cache_sel=cache_sel3, win=win4, conv=state_conv[l], C=state_mlstm_C[l], n=state_mlstm_n[l],
                      m=state_mlstm_m[l])
        xs, st = _layer(xs, mods(l, bp, bp + bs), layers[l], cfg, past, states)
        ss.append(st)

    y_prompt = _rmsnorm(xp.reshape(bp * seq, d), final_norm_w).reshape(bp, seq, d)
    y_sample = _rmsnorm(xs.reshape(bs * t_new, d), final_norm_w).reshape(bs, t_new, d)
    n_keep = min(WINDOW, seq)
    stack = lambda sts, i: jnp.stack([s[i] for s in sts])
    p_win = jnp.stack([s[2][:, seq - n_keep:] for s in ps])
    return (y_prompt, y_sample, stack(ps, 0), stack(ps, 1), p_win, stack(ps, 3), stack(ps, 4), stack(ps, 5),
            stack(ps, 6), stack(ss, 0), stack(ss, 1), stack(ss, 2), stack(ss, 3), stack(ss, 4), stack(ss, 5),
            stack(ss, 6))
```
